```python
import math
import jax, jax.numpy as jnp
from jax import lax
import numpy as np

D_MODEL = 1024
BATCH = 4
SEQ = 4096
DEPTH = 2

GRID_W = 64
CTX_LEN = 256
N_EVEN = (DEPTH + 1) // 2
N_ODD = DEPTH // 2
HEAD_DIM = 64
ROPE_THETA = 10000.0
EPS = 1e-6
BLOCK = 128
A_HEADS = 8
A_KV_HEADS = 2
A_GROUP = A_HEADS // A_KV_HEADS
A_WINDOW = 128
B_HEADS = 4
B_HEAD_DIM = 64
B_LAMBDA_DECAY = 0.3
A_Q = A_HEADS * HEAD_DIM
A_KV = A_KV_HEADS * HEAD_DIM
B_QK = B_HEADS * 2 * B_HEAD_DIM
B_V = B_HEADS * 2 * B_HEAD_DIM
ATTN_SPLITS = (A_Q, A_Q + A_KV, A_Q + 2 * A_KV, A_Q + 2 * A_KV + B_QK, A_Q + 2 * A_KV + 2 * B_QK)
ATTN_IN = A_Q + 2 * A_KV + 2 * B_QK + B_V
ATTN_OUT = A_HEADS * HEAD_DIM + B_HEADS * 2 * B_HEAD_DIM
C_HEADS = 4
C_DK = D_MODEL // 2 // C_HEADS
C_DV = D_MODEL // C_HEADS
C_GATE_RANK = 16
C_GATE_NORM = 16.0
C_CHUNK = 64
GLA_SPLITS = (C_HEADS * C_DK, 2 * C_HEADS * C_DK, 2 * C_HEADS * C_DK + C_HEADS * C_DV)
GLA_IN = 2 * C_HEADS * C_DK + 2 * C_HEADS * C_DV
D_FF = 2816
CONV_W = 3

kernel_name = 'hybrid_dit_swa_diff_gla_convffn'


def rms_norm(x, g):
    xf = x.astype(jnp.float32)
    y = xf * lax.rsqrt(jnp.mean(xf * xf, axis=-1, keepdims=True) + EPS)
    return (y * g.astype(jnp.float32)).astype(x.dtype)


def axial_rope_tables(n_tok):
    rows = n_tok // GRID_W
    row = jnp.repeat(jnp.arange(rows, dtype=jnp.float32), GRID_W)
    col = jnp.tile(jnp.arange(GRID_W, dtype=jnp.float32), rows)
    axis_dim = HEAD_DIM // 2
    inv_freq = ROPE_THETA ** (-jnp.arange(0, axis_dim, 2, dtype=jnp.float32) / axis_dim)
    ang = jnp.concatenate([row[:, None] * inv_freq, col[:, None] * inv_freq], axis=-1)
    return jnp.cos(ang), jnp.sin(ang)


def apply_rope(x, cos, sin):
    bshape = (cos.shape[0],) + (1,) * (x.ndim - 3) + (cos.shape[1],)
    c = cos.reshape(bshape).astype(x.dtype)
    s = sin.reshape(bshape).astype(x.dtype)
    x1, x2 = x[..., 0::2], x[..., 1::2]
    return jnp.stack([x1 * c - x2 * s, x1 * s + x2 * c], axis=-1).reshape(x.shape)


def window_gqa_latent(q, k, v, kc, vc, sink):
    bsz, n_tok = q.shape[:2]
    nb = n_tok // BLOCK
    scale = HEAD_DIM ** -0.5
    qb = q.reshape(bsz, nb, BLOCK, A_KV_HEADS, A_GROUP, HEAD_DIM)

    def band(a):
        a = jnp.pad(a, ((0, 0), (BLOCK, BLOCK), (0, 0), (0, 0))).reshape(bsz, nb + 2, BLOCK, A_KV_HEADS, HEAD_DIM)
        return jnp.concatenate([a[:, :-2], a[:, 1:-1], a[:, 2:]], axis=2)

    kb, vb = band(k), band(v)
    s_win = jnp.einsum('bnqkgd,bnskd->bnkgqs', qb, kb).astype(jnp.float32) * scale
    blk = jnp.arange(nb)[:, None] * BLOCK
    qpos = blk + jnp.arange(BLOCK)[None, :]
    kpos = blk - BLOCK + jnp.arange(3 * BLOCK)[None, :]
    valid = ((kpos[:, None, :] >= 0) & (kpos[:, None, :] < n_tok)
             & (jnp.abs(qpos[:, :, None] - kpos[:, None, :]) <= A_WINDOW))
    s_win = jnp.where(valid[None, :, None, None], s_win, -jnp.inf)
    s_ctx = jnp.einsum('bnqkgd,bskd->bnkgqs', qb, kc).astype(jnp.float32) * scale
    s_sink = jnp.broadcast_to(sink.astype(jnp.float32).reshape(1, 1, A_KV_HEADS, A_GROUP, 1, 1), s_ctx.shape[:-1] + (1,))
    p = jax.nn.softmax(jnp.concatenate([s_win, s_ctx, s_sink], axis=-1), axis=-1).astype(v.dtype)
    n_ctx = kc.shape[1]
    o = (jnp.einsum('bnkgqs,bnskd->bnqkgd', p[..., :3 * BLOCK], vb)
         + jnp.einsum('bnkgqs,bskd->bnqkgd', p[..., 3 * BLOCK:3 * BLOCK + n_ctx], vc))
    return o.reshape(bsz, n_tok, A_HEADS * HEAD_DIM)


def gqa_context(qc, kc, vc, sink):
    bsz, n_ctx = qc.shape[:2]
    qg = qc.reshape(bsz, n_ctx, A_KV_HEADS, A_GROUP, HEAD_DIM)
    s = jnp.einsum('bqkgd,bskd->bkgqs', qg, kc).astype(jnp.float32) * (HEAD_DIM ** -0.5)
    s_sink = jnp.broadcast_to(sink.astype(jnp.float32).reshape(1, A_KV_HEADS, A_GROUP, 1, 1), s.shape[:-1] + (1,))
    p = jax.nn.softmax(jnp.concatenate([s, s_sink], axis=-1), axis=-1)[..., :n_ctx].astype(vc.dtype)
    o = jnp.einsum('bkgqs,bskd->bqkgd', p, vc)
    return o.reshape(bsz, n_ctx, A_HEADS * HEAD_DIM)


def diff_attend(q, keys, vals, lam):
    s = jnp.einsum('bqhmd,bshmd->bhmqs', q, keys).astype(jnp.float32) * (B_HEAD_DIM ** -0.5)
    p = jax.nn.softmax(s, axis=-1)
    w = p[:, :, 0] - lam * p[:, :, 1]
    return jnp.einsum('bhqs,bshe->bqhe', w.astype(vals.dtype), vals)


def diff_attn_latent(q, k, v, kc, vc, lam):
    bsz, n_tok = q.shape[:2]
    nb = n_tok // BLOCK
    keys = jnp.concatenate([k, kc], axis=1)
    vals = jnp.concatenate([v, vc], axis=1)
    qb = jnp.moveaxis(q.reshape(bsz, nb, BLOCK, B_HEADS, 2, B_HEAD_DIM), 1, 0)
    o = lax.map(lambda qblk: diff_attend(qblk, keys, vals, lam), qb)
    return jnp.moveaxis(o, 0, 1).reshape(bsz, n_tok, B_HEADS, 2 * B_HEAD_DIM)


def diff_head_norm(o, g, lam_init):
    bsz, n = o.shape[:2]
    return (rms_norm(o, g) * (1.0 - lam_init)).reshape(bsz, n, B_HEADS * 2 * B_HEAD_DIM)


def attn_mixer(xn, cn, w_in, w_out, sink, lam_vec, subln_g, lam_init, cos, sin, need_ctx):
    def proj(t):
        bsz, n, _ = t.shape
        aq, ak, av, bq, bk, bv = jnp.split(t @ w_in, ATTN_SPLITS, axis=-1)
        return (aq.reshape(bsz, n, A_HEADS, HEAD_DIM), ak.reshape(bsz, n, A_KV_HEADS, HEAD_DIM),
                av.reshape(bsz, n, A_KV_HEADS, HEAD_DIM), bq.reshape(bsz, n, B_HEADS, 2, B_HEAD_DIM),
                bk.reshape(bsz, n, B_HEADS, 2, B_HEAD_DIM), bv.reshape(bsz, n, B_HEADS, 2 * B_HEAD_DIM))

    aq, ak, av, bq, bk, bv = proj(xn)
    caq, cak, cav, cbq, cbk, cbv = proj(cn)
    aq, ak, bq, bk = (apply_rope(t, cos, sin) for t in (aq, ak, bq, bk))
    lv = lam_vec.astype(jnp.float32)
    lam = jnp.exp(jnp.sum(lv[0] * lv[1])) - jnp.exp(jnp.sum(lv[2] * lv[3])) + lam_init
    oa = window_gqa_latent(aq, ak, av, cak, cav, sink)
    ob = diff_head_norm(diff_attn_latent(bq, bk, bv, cbk, cbv, lam), subln_g, lam_init)
    y = jnp.concatenate([oa, ob], axis=-1) @ w_out
    yc = None
    if need_ctx:
        oca = gqa_context(caq, cak, cav, sink)
        ocb = diff_head_norm(diff_attend(cbq, cbk, cbv, lam), subln_g, lam_init)
        yc = jnp.concatenate([oca, ocb], axis=-1) @ w_out
    return y, yc


def gla_chunk_scan(q, k, v, loga, s0):
    bsz, n_tok = q.shape[:2]
    nc = n_tok // C_CHUNK

    def chunks(a):
        return a.reshape(bsz, nc, C_CHUNK, C_HEADS, a.shape[-1]).transpose(1, 0, 3, 2, 4).astype(jnp.float32)

    lower = jnp.tril(jnp.ones((C_CHUNK, C_CHUNK), dtype=bool))

    def step(state, inp):
        qc, kc, vc, ac = inp
        b = jnp.cumsum(ac, axis=2)
        rel = jnp.where(lower[:, :, None], b[:, :, :, None, :] - b[:, :, None, :, :], -jnp.inf)
        scores = jnp.einsum('bhtd,bhsd,bhtsd->bhts', qc, kc, jnp.exp(rel))
        out = (jnp.einsum('bhts,bhse->bhte', scores, vc)
               + jnp.einsum('bhtd,bhde->bhte', qc * jnp.exp(b), state))
        b_last = b[:, :, -1:, :]
        state = (state * jnp.exp(b_last[:, :, 0, :, None])
                 + jnp.einsum('bhsd,bhse->bhde', kc * jnp.exp(b_last - b), vc))
        return state, out

    _, o = lax.scan(step, s0, (chunks(q), chunks(k), chunks(v), chunks(loga)))
    return o.transpose(1, 0, 3, 2, 4).reshape(bsz, n_tok, C_HEADS, C_DV).astype(v.dtype)


def gla_final_state(k, v, loga):
    b = jnp.cumsum(loga.astype(jnp.float32), axis=1)
    w = jnp.exp(b[:, -1:] - b)
    return jnp.einsum('blhd,blhe->bhde', k.astype(jnp.float32) * w, v.astype(jnp.float32))


def gla_mixer(xn, cn, w_in, gate_w1, gate_w2, gate_b, norm_g, w_out, need_ctx):
    def proj(t):
        bsz, n, _ = t.shape
        q, k, v, g = jnp.split(t @ w_in, GLA_SPLITS, axis=-1)
        q = q.reshape(bsz, n, C_HEADS, C_DK) * (C_DK ** -0.5)
        k = k.reshape(bsz, n, C_HEADS, C_DK)
        v = v.reshape(bsz, n, C_HEADS, C_DV)
        loga = [(jax.nn.log_sigmoid(((t @ gate_w1[d]) @ gate_w2[d] + gate_b[d]).astype(jnp.float32))
                 / C_GATE_NORM).reshape(bsz, n, C_HEADS, C_DK) for d in range(2)]
        return q, k, v, g, loga

    def out_proj(o, g):
        bsz, n = o.shape[:2]
        return (rms_norm(o, norm_g).reshape(bsz, n, C_HEADS * C_DV) * jax.nn.silu(g)) @ w_out

    flip = lambda a: jnp.flip(a, axis=1)
    q, k, v, g, (la_f, la_b) = proj(xn)
    qc, kc, vc, gc, (lac_f, lac_b) = proj(cn)
    s_f = gla_final_state(kc, vc, lac_f)
    s_b = gla_final_state(flip(kc), flip(vc), flip(lac_b))
    o = (gla_chunk_scan(q, k, v, la_f, s_f)
         + flip(gla_chunk_scan(flip(q), flip(k), flip(v), flip(la_b), s_b)))
    y = out_proj(o, g)
    yc = None
    if need_ctx:
        z = jnp.zeros((cn.shape[0], C_HEADS, C_DK, C_DV), jnp.float32)
        oc = (gla_chunk_scan(qc, kc, vc, lac_f, z)
              + flip(gla_chunk_scan(flip(qc), flip(kc), flip(vc), flip(lac_b), z)))
        yc = out_proj(oc, gc)
    return y, yc


def conv_ffn(x, w_up, conv_w, conv_b, w_down):
    h = x @ w_up
    ch = h.shape[-1]
    h = lax.conv_general_dilated(h, conv_w[:, None, :].astype(h.dtype), window_strides=(1,),
                                 padding=((CONV_W // 2, CONV_W // 2),), dimension_numbers=('NWC', 'WIO', 'NWC'),
                                 feature_group_count=ch) + conv_b
    u, gt = jnp.split(h, 2, axis=-1)
    return (jax.nn.silu(gt) * u) @ w_down


def setup_inputs(seed: int = 0) -> dict:
    key = jax.random.key(seed)
    ks = jax.random.split(key, 24)
    nrm = lambda k, shape, s: jax.random.normal(k, shape, jnp.float32) * s
    D = D_MODEL
    return {
        'x': nrm(ks[0], (BATCH, SEQ, D), 1.0),
        'c': nrm(ks[1], (BATCH, D), 1.0),
        'ctx': nrm(ks[2], (BATCH, CTX_LEN, D), 1.0),
        'c_ctx': nrm(ks[3], (D,), 1.0),
        'mod_w': nrm(ks[4], (DEPTH, D, 6 * D), D ** -0.5),
        'mod_b': nrm(ks[5], (DEPTH, 6 * D), 0.02),
        'norm1_g': 1.0 + nrm(ks[6], (DEPTH, D), 0.02),
        'norm2_g': 1.0 + nrm(ks[7], (DEPTH, D), 0.02),
        'attn_w_in': nrm(ks[8], (N_EVEN, D, ATTN_IN), D ** -0.5),
        'attn_w_out': nrm(ks[9], (N_EVEN, ATTN_OUT, D), ATTN_OUT ** -0.5),
        'attn_sink': nrm(ks[10], (N_EVEN, A_HEADS), 0.5),
        'diff_lambda': nrm(ks[11], (N_EVEN, 4, B_HEAD_DIM), 0.1),
        'diff_subln_g': 1.0 + nrm(ks[12], (N_EVEN, 2 * B_HEAD_DIM), 0.02),
        'gla_w_in': nrm(ks[13], (N_ODD, D, GLA_IN), D ** -0.5),
        'gla_gate_w1': nrm(ks[14], (N_ODD, 2, D, C_GATE_RANK), D ** -0.5),
        'gla_gate_w2': nrm(ks[15], (N_ODD, 2, C_GATE_RANK, C_HEADS * C_DK), C_GATE_RANK ** -0.5),
        'gla_gate_b': nrm(ks[16], (N_ODD, 2, C_HEADS * C_DK), 0.1),
        'gla_norm_g': 1.0 + nrm(ks[17], (N_ODD, C_DV), 0.02),
        'gla_w_out': nrm(ks[18], (N_ODD, C_HEADS * C_DV, D), (C_HEADS * C_DV) ** -0.5),
        'ffn_w_up': nrm(ks[19], (DEPTH, D, 2 * D_FF), D ** -0.5),
        'ffn_conv_w': nrm(ks[20], (DEPTH, CONV_W, 2 * D_FF), CONV_W ** -0.5),
        'ffn_conv_b': nrm(ks[21], (DEPTH, 2 * D_FF), 0.02),
        'ffn_w_down': nrm(ks[22], (DEPTH, D_FF, D), D_FF ** -0.5),
        'final_norm_g': 1.0 + nrm(ks[23], (D,), 0.02),
    }


def reference(x, c, ctx, c_ctx, mod_w, mod_b, norm1_g, norm2_g, attn_w_in, attn_w_out, attn_sink,
              diff_lambda, diff_subln_g, gla_w_in, gla_gate_w1, gla_gate_w2, gla_gate_b, gla_norm_g,
              gla_w_out, ffn_w_up, ffn_conv_w, ffn_conv_b, ffn_w_down, final_norm_g):
    cos, sin = axial_rope_tables(x.shape[1])
    silu_c = jax.nn.silu(c)
    silu_cc = jax.nn.silu(c_ctx)
    h, hc = x, ctx
    for layer in range(DEPTH):
        need_ctx = layer < DEPTH - 1
        mod = (silu_c @ mod_w[layer] + mod_b[layer])[:, None, :]
        mod_c = (silu_cc @ mod_w[layer] + mod_b[layer])[None, None, :]
        sh1, sc1, g1, sh2, sc2, g2 = jnp.split(mod, 6, axis=-1)
        csh1, csc1, cg1, csh2, csc2, cg2 = jnp.split(mod_c, 6, axis=-1)
        xn = rms_norm(h, norm1_g[layer]) * (1.0 + sc1) + sh1
        cn = rms_norm(hc, norm1_g[layer]) * (1.0 + csc1) + csh1
        i = layer // 2
        if layer % 2 == 0:
            lam_init = 0.8 - 0.6 * math.exp(-B_LAMBDA_DECAY * layer)
            y, yc = attn_mixer(xn, cn, attn_w_in[i], attn_w_out[i], attn_sink[i], diff_lambda[i],
                               diff_subln_g[i], lam_init, cos, sin, need_ctx)
        else:
            y, yc = gla_mixer(xn, cn, gla_w_in[i], gla_gate_w1[i], gla_gate_w2[i], gla_gate_b[i],
                              gla_norm_g[i], gla_w_out[i], need_ctx)
        h = h + g1 * y
        h = h + g2 * conv_ffn(rms_norm(h, norm2_g[layer]) * (1.0 + sc2) + sh2, ffn_w_up[layer],
                              ffn_conv_w[layer], ffn_conv_b[layer], ffn_w_down[layer])
        if need_ctx:
            hc = hc + cg1 * yc
            hc = hc + cg2 * conv_ffn(rms_norm(hc, norm2_g[layer]) * (1.0 + csc2) + csh2, ffn_w_up[layer],
                                     ffn_conv_w[layer], ffn_conv_b[layer], ffn_w_down[layer])
    return rms_norm(h, final_norm_g)
```

```python
import functools
import math

import numpy as np
import jax
import jax.numpy as jnp
from jax import lax
from jax.experimental import pallas as pl
from jax.experimental.pallas import tpu as pltpu

F32 = jnp.float32
BF16 = jnp.bfloat16

D_MODEL = 1024
DEPTH = 2
GRID_W = 64
HEAD_DIM = 64
ROPE_THETA = 10000.0
EPS = 1e-6
BLOCK = 128
A_HEADS = 8
A_KV_HEADS = 2
A_GROUP = A_HEADS // A_KV_HEADS
B_HEADS = 4
B_LAMBDA_DECAY = 0.3
A_Q = A_HEADS * HEAD_DIM
A_KV = A_KV_HEADS * HEAD_DIM
B_QK = B_HEADS * 2 * HEAD_DIM
B_V = B_HEADS * 2 * HEAD_DIM
C_HEADS = 4
C_DK = D_MODEL // 2 // C_HEADS
C_DV = D_MODEL // C_HEADS
C_GATE_RANK = 16
C_GATE_NORM = 16.0
C_CHUNK = 64
C_QK = C_HEADS * C_DK
C_V = C_HEADS * C_DV
D_FF = 2816
LANES = 128
MOD_ROWS = 8
VMEM_LIMIT = 56 * 1024 * 1024


def _cparams(sem):
    return pltpu.CompilerParams(dimension_semantics=sem, vmem_limit_bytes=VMEM_LIMIT)


def _dot(a, b):
    return jnp.dot(a, b, preferred_element_type=F32)


def _dot_nt(a, b):
    return lax.dot_general(a, b, (((1,), (1,)), ((), ())), preferred_element_type=F32)


def _dot_tn(a, b):
    return lax.dot_general(a, b, (((0,), (0,)), ((), ())), preferred_element_type=F32)


def _sigmoid(x):
    return 1.0 / (1.0 + jnp.exp(-x))


def _norm_mod(x, g, sc, sh):
    ms = jnp.mean(x * x, axis=-1, keepdims=True)
    return (x * lax.rsqrt(ms + EPS) * g) * (1.0 + sc) + sh


def _mod_kernel(c_ref, w_ref, b_ref, o_ref):
    c = c_ref[...]
    s = (c * _sigmoid(c)).astype(BF16)
    o_ref[...] = _dot(s, w_ref[...].astype(BF16)) + b_ref[...]


def _modulation(c_rows, mod_w, mod_b):
    d = D_MODEL
    return pl.pallas_call(
        _mod_kernel,
        grid=(DEPTH, 6),
        in_specs=[
            pl.BlockSpec((MOD_ROWS, d), lambda l, n: (0, 0)),
            pl.BlockSpec((None, d, d), lambda l, n: (l, 0, n)),
            pl.BlockSpec((None, 1, d), lambda l, n: (l, 0, n)),
        ],
        out_specs=pl.BlockSpec((None, MOD_ROWS, d), lambda l, n: (l, 0, n)),
        out_shape=jax.ShapeDtypeStruct((DEPTH, MOD_ROWS, 6 * d), F32),
        compiler_params=_cparams(("parallel", "parallel")),
        name="modulation",
    )(c_rows, mod_w, mod_b.reshape(DEPTH, 1, 6 * d))


def _mod_spec(k, row_of):
    return pl.BlockSpec((None, None, 1, D_MODEL), lambda i, *_: (row_of(i), k, 0, 0))


_ATTN_GROUPS = (("aq", A_Q, True, 0.125), ("bq", B_QK, True, 0.125), ("bk", B_QK, True, 1.0),
                ("ak", A_KV, True, 1.0), ("av", A_KV, False, 1.0), ("bv", B_V, False, 1.0))


def _rope_chunk(v, cos, sin, first):
    partner = jnp.where(first, pltpu.roll(v, 96, 1), pltpu.roll(v, 32, 1))
    return v * cos + partner * sin


def _attn_inproj_kernel(x_ref, sh_ref, sc_ref, g_ref, w_ref, cos_ref, sin_ref, *out_refs, rope):
    xn = _norm_mod(x_ref[...], g_ref[...], sc_ref[...], sh_ref[...]).astype(BF16)
    tm = xn.shape[0]
    if rope:
        cos = cos_ref[...]
        sin = sin_ref[...]
        lane = lax.broadcasted_iota(jnp.int32, (tm, LANES), 1)
        first = (lane % HEAD_DIM) < (HEAD_DIM // 2)
    col = 0
    for (name, width, roped, scale), o_ref in zip(_ATTN_GROUPS, out_refs):
        y = _dot(xn, w_ref[:, col:col + width])
        col += width
        for c in range(width // LANES):
            v = y[:, c * LANES:(c + 1) * LANES]
            if rope and roped:
                v = _rope_chunk(v, cos, sin, first)
            if scale != 1.0:
                v = v * scale
            o_ref[:, c * LANES:(c + 1) * LANES] = v.astype(BF16)


def _attn_inproj(x2d, mod, norm_g, w, cos_t, sin_t, *, tm, row_of, rope, tiles_per_seq):
    m, d = x2d.shape
    n_all = w.shape[1]
    out_shape = [jax.ShapeDtypeStruct((m, width), BF16) for (_, width, _, _) in _ATTN_GROUPS]
    out_specs = [pl.BlockSpec((tm, width), lambda i: (i, 0)) for (_, width, _, _) in _ATTN_GROUPS]
    return pl.pallas_call(
        functools.partial(_attn_inproj_kernel, rope=rope),
        grid=(m // tm,),
        in_specs=[
            pl.BlockSpec((tm, d), lambda i: (i, 0)),
            _mod_spec(0, row_of), _mod_spec(1, row_of),
            pl.BlockSpec((1, d), lambda i: (0, 0)),
            pl.BlockSpec((d, n_all), lambda i: (0, 0)),
            pl.BlockSpec((tm, LANES), lambda i: (i % tiles_per_seq, 0)),
            pl.BlockSpec((tm, LANES), lambda i: (i % tiles_per_seq, 0)),
        ],
        out_specs=out_specs,
        out_shape=out_shape,
        compiler_params=_cparams(("parallel",)),
        name="attn_inproj_rope" if rope else "attn_inproj_ctx",
    )(x2d, mod, mod, norm_g, w, cos_t, sin_t)


def _gqa_kernel(sink_ref, q_ref, *refs, window, nb):
    if window:
        kp_ref, kc_ref, kn_ref, vp_ref, vc_ref, vn_ref, kx_ref, vx_ref, o_ref = refs
        keys = jnp.concatenate([kp_ref[...], kc_ref[...], kn_ref[...], kx_ref[...]], axis=0)
        vals = jnp.concatenate([vp_ref[...], vc_ref[...], vn_ref[...], vx_ref[...]], axis=0)
    else:
        kx_ref, vx_ref, o_ref = refs
        keys = kx_ref[...]
        vals = vx_ref[...]
    tq = q_ref.shape[0]
    ns = keys.shape[0]
    if window:
        n = pl.program_id(1)
        t = lax.broadcasted_iota(jnp.int32, (tq, ns), 0)
        s = lax.broadcasted_iota(jnp.int32, (tq, ns), 1)
        has_prev = jnp.where(n > 0, 1, 0)
        has_next = jnp.where(n < nb - 1, 1, 0)
        lower = t * has_prev + BLOCK * (1 - has_prev)
        upper = (t + 1) * has_next + (2 * BLOCK - 1)
        ninf = jnp.float32(-jnp.inf)
        bias = jnp.where(s < lower, ninf, jnp.where(s > upper, jnp.where(s < 3 * BLOCK, ninf, 0.0), 0.0))
    lane = lax.broadcasted_iota(jnp.int32, (1, LANES), 1)
    lo = lane < HEAD_DIM
    half = (jnp.where(lo, 1.0, 0.0).astype(F32), jnp.where(lo, 0.0, 1.0).astype(F32))
    for j in range(A_GROUP):
        qc = q_ref[:, j * LANES:(j + 1) * LANES].astype(F32)
        outs = []
        for kv in range(A_KV_HEADS):
            qm = (qc * half[kv]).astype(BF16)
            sc = _dot_nt(qm, keys)
            if window:
                sc = sc + bias
            sk = sink_ref[kv * A_GROUP + j]
            mx = jnp.maximum(jnp.max(sc, axis=-1, keepdims=True), sk)
            p = jnp.exp(sc - mx)
            den = jnp.sum(p, axis=-1, keepdims=True) + jnp.exp(sk - mx)
            pn = (p * (1.0 / den)).astype(BF16)
            outs.append(_dot(pn, vals))
        o_ref[:, j * LANES:(j + 1) * LANES] = jnp.where(lo, outs[0], outs[1]).astype(BF16)


def _gqa_window(sink, aq, ak, av, cak, cav):
    bsz, n_tok, _ = aq.shape
    n_ctx = cak.shape[1]
    nb = n_tok // BLOCK
    kv_prev = pl.BlockSpec((None, BLOCK, A_KV), lambda b, n: (b, jnp.maximum(n - 1, 0), 0))
    kv_cur = pl.BlockSpec((None, BLOCK, A_KV), lambda b, n: (b, n, 0))
    kv_next = pl.BlockSpec((None, BLOCK, A_KV), lambda b, n: (b, jnp.minimum(n + 1, nb - 1), 0))
    kv_ctx = pl.BlockSpec((None, n_ctx, A_KV), lambda b, n: (b, 0, 0))
    return pl.pallas_call(
        functools.partial(_gqa_kernel, window=True, nb=nb),
        grid=(bsz, nb),
        in_specs=[pl.BlockSpec(memory_space=pltpu.SMEM),
                  pl.BlockSpec((None, BLOCK, A_Q), lambda b, n: (b, n, 0)),
                  kv_prev, kv_cur, kv_next, kv_prev, kv_cur, kv_next, kv_ctx, kv_ctx],
        out_specs=pl.BlockSpec((None, BLOCK, A_Q), lambda b, n: (b, n, 0)),
        out_shape=jax.ShapeDtypeStruct((bsz, n_tok, A_Q), BF16),
        compiler_params=_cparams(("parallel", "parallel")),
        name="gqa_window",
    )(sink, aq, ak, ak, ak, av, av, av, cak, cav)


def _gqa_context(sink, caq, cak, cav):
    bsz, n_ctx, _ = caq.shape
    nb = n_ctx // BLOCK
    kv_ctx = pl.BlockSpec((None, n_ctx, A_KV), lambda b, n: (b, 0, 0))
    return pl.pallas_call(
        functools.partial(_gqa_kernel, window=False, nb=nb),
        grid=(bsz, nb),
        in_specs=[pl.BlockSpec(memory_space=pltpu.SMEM),
                  pl.BlockSpec((None, BLOCK, A_Q), lambda b, n: (b, n, 0)),
                  kv_ctx, kv_ctx],
        out_specs=pl.BlockSpec((None, BLOCK, A_Q), lambda b, n: (b, n, 0)),
        out_shape=jax.ShapeDtypeStruct((bsz, n_ctx, A_Q), BF16),
        compiler_params=_cparams(("parallel", "parallel")),
        name="gqa_context",
    )(sink, caq, cak, cav)


def _diff_attn_kernel(lam_ref, g_ref, q_ref, *refs, nseg, lam_init):
    k_refs = refs[0:2 * nseg:2]
    v_refs = refs[1:2 * nseg:2]
    o_ref = refs[2 * nseg]
    lv = lam_ref[...]
    lam = (jnp.exp(jnp.sum(lv[0:1] * lv[1:2], axis=-1, keepdims=True))
           - jnp.exp(jnp.sum(lv[2:3] * lv[3:4], axis=-1, keepdims=True)) + lam_init)
    q = q_ref[...].astype(F32)
    lane = lax.broadcasted_iota(jnp.int32, (1, LANES), 1)
    lo = lane < HEAD_DIM
    q0 = (q * jnp.where(lo, 1.0, 0.0).astype(F32)).astype(BF16)
    q1 = (q * jnp.where(lo, 0.0, 1.0).astype(F32)).astype(BF16)
    probs = []
    for qm in (q0, q1):
        sc = [_dot_nt(qm, k[...]) for k in k_refs]
        mx = functools.reduce(jnp.maximum, [jnp.max(s, axis=-1, keepdims=True) for s in sc])
        p = [jnp.exp(s - mx) for s in sc]
        den = functools.reduce(lambda a, b: a + b, [jnp.sum(x, axis=-1, keepdims=True) for x in p])
        probs.append((p, den))
    r0 = 1.0 / probs[0][1]
    r1 = lam / probs[1][1]
    o = None
    for i in range(nseg):
        w = (probs[0][0][i] * r0 - probs[1][0][i] * r1).astype(BF16)
        c = _dot(w, v_refs[i][...])
        o = c if o is None else o + c
    ms = jnp.mean(o * o, axis=-1, keepdims=True)
    o_ref[...] = ((o * lax.rsqrt(ms + EPS) * g_ref[...]) * (1.0 - lam_init)).astype(BF16)


def _diff_attn(lam_vec, subln_g, q, kv_list, *, tq, lam_init):
    bsz, n_q, _ = q.shape
    in_specs = [pl.BlockSpec((4, HEAD_DIM), lambda b, h, i: (0, 0)),
                pl.BlockSpec((1, LANES), lambda b, h, i: (0, 0)),
                pl.BlockSpec((None, tq, LANES), lambda b, h, i: (b, i, h))]
    args = [lam_vec, subln_g, q]
    for k, v in kv_list:
        ns = k.shape[1]
        in_specs += [pl.BlockSpec((None, ns, LANES), lambda b, h, i: (b, 0, h))] * 2
        args += [k, v]
    return pl.pallas_call(
        functools.partial(_diff_attn_kernel, nseg=len(kv_list), lam_init=lam_init),
        grid=(bsz, B_HEADS, n_q // tq),
        in_specs=in_specs,
        out_specs=pl.BlockSpec((None, tq, LANES), lambda b, h, i: (b, i, h)),
        out_shape=jax.ShapeDtypeStruct((bsz, n_q, B_V), BF16),
        compiler_params=_cparams(("parallel", "parallel", "parallel")),
        name="diff_attn_%dseg" % len(kv_list),
    )(*args)


def _outproj_kernel(*refs, n_in):
    a_refs = refs[:n_in]
    w_refs = refs[n_in:2 * n_in]
    h_ref, gt_ref, o_ref = refs[2 * n_in:]
    y = None
    for a, w in zip(a_refs, w_refs):
        c = _dot(a[...], w[...])
        y = c if y is None else y + c
    o_ref[...] = h_ref[...] + gt_ref[...] * y


def _outproj(acts, ws, h2d, mod, *, tm, row_of):
    m, d = h2d.shape
    n_in = len(acts)
    in_specs = [pl.BlockSpec((tm, a.shape[1]), lambda i: (i, 0)) for a in acts]
    in_specs += [pl.BlockSpec(w.shape, lambda i: (0, 0)) for w in ws]
    in_specs += [pl.BlockSpec((tm, d), lambda i: (i, 0)), _mod_spec(2, row_of)]
    return pl.pallas_call(
        functools.partial(_outproj_kernel, n_in=n_in),
        grid=(m // tm,),
        in_specs=in_specs,
        out_specs=pl.BlockSpec((tm, d), lambda i: (i, 0)),
        out_shape=jax.ShapeDtypeStruct((m, d), F32),
        compiler_params=_cparams(("parallel",)),
        name="outproj_residual",
    )(*acts, *ws, h2d, mod)


FFN_HALO = 16


def _ffn_kernel(h_ref, hp_ref, hn_ref, sh_ref, sc_ref, gt_ref, ng_ref, wu_ref, wg_ref, cwu_ref, cwg_ref,
                cbu_ref, cbg_ref, wd_ref, fg_ref, o_ref, xn_ref, hu_ref, hg_ref, acc_ref,
                *, tiles_per_seq, final_norm):
    i = pl.program_id(0)
    j = pl.program_id(1)
    tm = h_ref.shape[0]
    hl = FFN_HALO

    @pl.when(j == 0)
    def _():
        pos = i % tiles_per_seq
        g, sc, sh = ng_ref[...], sc_ref[...], sh_ref[...]
        xn_ref[hl:hl + tm, :] = _norm_mod(h_ref[...], g, sc, sh).astype(BF16)
        prev = _norm_mod(hp_ref[...], g, sc, sh) * jnp.where(pos > 0, 1.0, 0.0)
        nxt = _norm_mod(hn_ref[...], g, sc, sh) * jnp.where(pos < tiles_per_seq - 1, 1.0, 0.0)
        xn_ref[0:hl, :] = prev.astype(BF16)
        xn_ref[hl + tm:2 * hl + tm, :] = nxt.astype(BF16)
        acc_ref[...] = jnp.zeros_like(acc_ref)

    xn = xn_ref[...]
    hu_ref[...] = _dot(xn, wu_ref[...])
    hg_ref[...] = _dot(xn, wg_ref[...])

    def conv(ref, cw_ref, cb_ref):
        cw = cw_ref[...]
        return (ref[hl - 1:hl - 1 + tm, :] * cw[0:1] + ref[hl:hl + tm, :] * cw[1:2]
                + ref[hl + 1:hl + 1 + tm, :] * cw[2:3] + cb_ref[...])

    u = conv(hu_ref, cwu_ref, cbu_ref)
    gt = conv(hg_ref, cwg_ref, cbg_ref)
    a = (gt * _sigmoid(gt) * u).astype(BF16)
    acc_ref[...] += _dot(a, wd_ref[...])

    @pl.when(j == pl.num_programs(1) - 1)
    def _():
        y = h_ref[...] + gt_ref[...] * acc_ref[...]
        if final_norm:
            ms = jnp.mean(y * y, axis=-1, keepdims=True)
            y = y * lax.rsqrt(ms + EPS) * fg_ref[...]
        o_ref[...] = y


def _conv_ffn(h2d, mod, norm_g, w_up_u, w_up_g, conv_w, conv_b, w_down, final_g, *, tm, tf, row_of,
              tiles_per_seq, final_norm):
    m, d = h2d.shape
    hl = FFN_HALO
    nj = D_FF // tf
    nhb = m // hl
    r = tm // hl
    cwu, cwg = conv_w[:, :D_FF], conv_w[:, D_FF:]
    cb = conv_b.reshape(1, 2 * D_FF)
    cbu, cbg = cb[:, :D_FF], cb[:, D_FF:]
    return pl.pallas_call(
        functools.partial(_ffn_kernel, tiles_per_seq=tiles_per_seq, final_norm=final_norm),
        grid=(m // tm, nj),
        in_specs=[
            pl.BlockSpec((tm, d), lambda i, j: (i, 0)),
            pl.BlockSpec((hl, d), lambda i, j: (jnp.maximum(i * r - 1, 0), 0)),
            pl.BlockSpec((hl, d), lambda i, j: (jnp.minimum((i + 1) * r, nhb - 1), 0)),
            _mod_spec(3, row_of), _mod_spec(4, row_of), _mod_spec(5, row_of),
            pl.BlockSpec((1, d), lambda i, j: (0, 0)),
            pl.BlockSpec((d, tf), lambda i, j: (0, j)),
            pl.BlockSpec((d, tf), lambda i, j: (0, j)),
            pl.BlockSpec((3, tf), lambda i, j: (0, j)),
            pl.BlockSpec((3, tf), lambda i, j: (0, j)),
            pl.BlockSpec((1, tf), lambda i, j: (0, j)),
            pl.BlockSpec((1, tf), lambda i, j: (0, j)),
            pl.BlockSpec((tf, d), lambda i, j: (j, 0)),
            pl.BlockSpec((1, d), lambda i, j: (0, 0)),
        ],
        out_specs=pl.BlockSpec((tm, d), lambda i, j: (i, 0)),
        out_shape=jax.ShapeDtypeStruct((m, d), F32),
        scratch_shapes=[pltpu.VMEM((tm + 2 * hl, d), BF16),
                        pltpu.VMEM((tm + 2 * hl, tf), F32),
                        pltpu.VMEM((tm + 2 * hl, tf), F32),
                        pltpu.VMEM((tm, d), F32)],
        compiler_params=_cparams(("parallel", "arbitrary")),
        name="conv_ffn",
    )(h2d, h2d, h2d, mod, mod, mod, norm_g, w_up_u, w_up_g, cwu, cwg, cbu, cbg, w_down, final_g)


def _gla_inproj_kernel(x_ref, sh_ref, sc_ref, g_ref, w_ref, w2_ref, gb_ref,
                       q_ref, k_ref, v_ref, sg_ref, laf_ref, lab_ref):
    xn = _norm_mod(x_ref[...], g_ref[...], sc_ref[...], sh_ref[...]).astype(BF16)
    q_ref[...] = (_dot(xn, w_ref[:, 0:C_QK]) * (C_DK ** -0.5)).astype(BF16)
    k_ref[...] = _dot(xn, w_ref[:, C_QK:2 * C_QK]).astype(BF16)
    for c in range(C_V // C_QK):
        lo = 2 * C_QK + c * C_QK
        v_ref[:, c * C_QK:(c + 1) * C_QK] = _dot(xn, w_ref[:, lo:lo + C_QK]).astype(BF16)
    for c in range(C_V // C_QK):
        lo = 2 * C_QK + C_V + c * C_QK
        gg = _dot(xn, w_ref[:, lo:lo + C_QK])
        sg_ref[:, c * C_QK:(c + 1) * C_QK] = (gg * _sigmoid(gg)).astype(BF16)
    lo = 2 * C_QK + 2 * C_V
    r = _dot(xn, w_ref[:, lo:lo + LANES]).astype(BF16)
    for dr, la_ref in enumerate((laf_ref, lab_ref)):
        z = _dot(r, w2_ref[:, dr * C_QK:(dr + 1) * C_QK]) + gb_ref[:, dr * C_QK:(dr + 1) * C_QK]
        la_ref[...] = (jnp.minimum(z, 0.0) - jnp.log(1.0 + jnp.exp(-jnp.abs(z)))) * (1.0 / C_GATE_NORM)


def _gla_inproj(x2d, mod, norm_g, w, w2, gb, *, tm, row_of):
    m, d = x2d.shape
    widths = (C_QK, C_QK, C_V, C_V, C_QK, C_QK)
    dts = (BF16, BF16, BF16, BF16, F32, F32)
    return pl.pallas_call(
        _gla_inproj_kernel,
        grid=(m // tm,),
        in_specs=[
            pl.BlockSpec((tm, d), lambda i: (i, 0)),
            _mod_spec(0, row_of), _mod_spec(1, row_of),
            pl.BlockSpec((1, d), lambda i: (0, 0)),
            pl.BlockSpec(w.shape, lambda i: (0, 0)),
            pl.BlockSpec(w2.shape, lambda i: (0, 0)),
            pl.BlockSpec(gb.shape, lambda i: (0, 0)),
        ],
        out_specs=[pl.BlockSpec((tm, wd), lambda i: (i, 0)) for wd in widths],
        out_shape=[jax.ShapeDtypeStruct((m, wd), dt) for wd, dt in zip(widths, dts)],
        compiler_params=_cparams(("parallel",)),
        name="gla_inproj",
    )(x2d, mod, mod, norm_g, w, w2, gb)


def _tri(n, reverse):
    r = lax.broadcasted_iota(jnp.int32, (n, n), 0)
    c = lax.broadcasted_iota(jnp.int32, (n, n), 1)
    return (c >= r) if reverse else (c <= r)


def _cumsum_rows(la, tri_bf):
    hi = la.astype(BF16)
    r1 = la - hi.astype(F32)
    mid = r1.astype(BF16)
    lo = (r1 - mid.astype(F32)).astype(BF16)
    return _dot(tri_bf, hi) + _dot(tri_bf, mid) + _dot(tri_bf, lo)


def _gla_ctx_state_kernel(k_ref, v_ref, laf_ref, lab_ref, sf_ref, sb_ref):
    n = k_ref.shape[0]
    for reverse, la_ref, s_ref in ((False, laf_ref, sf_ref), (True, lab_ref, sb_ref)):
        tri = jnp.where(_tri(n, reverse), 1.0, 0.0).astype(BF16)
        b = _cumsum_rows(la_ref[...], tri)
        b_end = b[0:1, :] if reverse else b[n - 1:n, :]
        kw = (k_ref[...].astype(F32) * jnp.exp(b_end - b)).astype(BF16)
        for h in range(C_HEADS):
            s_ref[h] = _dot_tn(v_ref[:, h * C_DV:(h + 1) * C_DV], kw[:, h * C_DK:(h + 1) * C_DK])


def _gla_ctx_state(kc, vc, lac_f, lac_b):
    bsz, n, _ = kc.shape
    s_shape = jax.ShapeDtypeStruct((bsz, C_HEADS, C_DV, C_DK), F32)
    s_spec = pl.BlockSpec((None, C_HEADS, C_DV, C_DK), lambda b: (b, 0, 0, 0))
    return pl.pallas_call(
        _gla_ctx_state_kernel,
        grid=(bsz,),
        in_specs=[pl.BlockSpec((None, n, C_QK), lambda b: (b, 0, 0)),
                  pl.BlockSpec((None, n, C_V), lambda b: (b, 0, 0)),
                  pl.BlockSpec((None, n, C_QK), lambda b: (b, 0, 0)),
                  pl.BlockSpec((None, n, C_QK), lambda b: (b, 0, 0))],
        out_specs=[s_spec, s_spec],
        out_shape=[s_shape, s_shape],
        compiler_params=_cparams(("parallel",)),
        name="gla_ctx_state",
    )(kc, vc, lac_f, lac_b)


def _gla_scan_kernel(q_ref, k_ref, v_ref, la_ref, s0_ref, *refs, reverse, final):
    if final:
        ob_ref, sg_ref, ng_ref, o_ref, st_ref = refs
    else:
        o_ref, st_ref = refs
    gidx = pl.program_id(1)

    @pl.when(gidx == 0)
    def _():
        st_ref[...] = s0_ref[...]

    gt = q_ref.shape[0]
    nchunk = gt // C_CHUNK
    c = C_CHUNK
    tri = _tri(c, reverse)
    tri_bf = jnp.where(tri, 1.0, 0.0).astype(BF16)
    order = range(nchunk - 1, -1, -1) if reverse else range(nchunk)
    for ci in order:
        rows = slice(ci * c, (ci + 1) * c)
        for h in range(C_HEADS):
            kcols = slice(h * C_DK, (h + 1) * C_DK)
            vcols = slice(h * C_DV, (h + 1) * C_DV)
            b = _cumsum_rows(la_ref[rows, kcols], tri_bf)
            b_mid = b[c // 2:c // 2 + 1, :]
            b_end = b[0:1, :] if reverse else b[c - 1:c, :]
            qf = q_ref[rows, kcols].astype(F32)
            kf = k_ref[rows, kcols].astype(F32)
            vv = v_ref[rows, vcols]
            q_in = (qf * jnp.exp(b - b_mid)).astype(BF16)
            k_in = (kf * jnp.exp(b_mid - b)).astype(BF16)
            sc = jnp.where(tri, _dot_nt(q_in, k_in), 0.0).astype(BF16)
            st = st_ref[h]
            o = _dot(sc, vv) + _dot_nt((qf * jnp.exp(b)).astype(BF16), st.astype(BF16))
            k_out = (kf * jnp.exp(b_end - b)).astype(BF16)
            st_ref[h] = st * jnp.exp(b_end) + _dot_tn(vv, k_out)
            if final:
                o = o + ob_ref[rows, vcols]
                ms = jnp.mean(o * o, axis=-1, keepdims=True)
                o = (o * lax.rsqrt(ms + EPS) * ng_ref[...]) * sg_ref[rows, vcols].astype(F32)
                o_ref[rows, vcols] = o.astype(BF16)
            else:
                o_ref[rows, vcols] = o


def _gla_scan(q, k, v, la, s0, *, gt, reverse, o_other=None, sg=None, norm_g=None):
    bsz, n_tok, _ = q.shape
    ng = n_tok // gt
    final = o_other is not None
    gi = (lambda b, g: (b, ng - 1 - g, 0)) if reverse else (lambda b, g: (b, g, 0))
    in_specs = [pl.BlockSpec((None, gt, C_QK), gi), pl.BlockSpec((None, gt, C_QK), gi),
                pl.BlockSpec((None, gt, C_V), gi), pl.BlockSpec((None, gt, C_QK), gi),
                pl.BlockSpec((None, C_HEADS, C_DV, C_DK), lambda b, g: (b, 0, 0, 0))]
    args = [q, k, v, la, s0]
    if final:
        in_specs += [pl.BlockSpec((None, gt, C_V), gi), pl.BlockSpec((None, gt, C_V), gi),
                     pl.BlockSpec((1, C_DV), lambda b, g: (0, 0))]
        args += [o_other, sg, norm_g]
    return pl.pallas_call(
        functools.partial(_gla_scan_kernel, reverse=reverse, final=final),
        grid=(bsz, ng),
        in_specs=in_specs,
        out_specs=pl.BlockSpec((None, gt, C_V), gi),
        out_shape=jax.ShapeDtypeStruct((bsz, n_tok, C_V), BF16 if final else F32),
        scratch_shapes=[pltpu.VMEM((C_HEADS, C_DV, C_DK), F32)],
        compiler_params=_cparams(("parallel", "arbitrary")),
        name="gla_scan_fwd_final" if final else "gla_scan_bwd",
    )(*args)


def _pair_split(n_heads):
    base = np.concatenate([np.arange(0, HEAD_DIM, 2), np.arange(1, HEAD_DIM, 2)])
    return np.concatenate([h * HEAD_DIM + base for h in range(n_heads)])


_A_HEAD_ORDER = np.array([kv * A_GROUP + j for j in range(A_GROUP) for kv in range(A_KV_HEADS)])


def _attn_in_cols():
    aq = (_A_HEAD_ORDER[:, None] * HEAD_DIM + _pair_split(1)[None, :]).reshape(-1)
    o_ak, o_av, o_bq = A_Q, A_Q + A_KV, A_Q + 2 * A_KV
    o_bk, o_bv = o_bq + B_QK, o_bq + 2 * B_QK
    return np.concatenate([aq, o_bq + _pair_split(2 * B_HEADS), o_bk + _pair_split(2 * B_HEADS),
                           o_ak + _pair_split(A_KV_HEADS), o_av + np.arange(A_KV), o_bv + np.arange(B_V)])


def _attn_out_rows():
    oa = (_A_HEAD_ORDER[:, None] * HEAD_DIM + np.arange(HEAD_DIM)[None, :]).reshape(-1)
    return oa


def _rope_tables(n_tok):
    rows = n_tok // GRID_W
    row = jnp.repeat(jnp.arange(rows, dtype=F32), GRID_W)
    col = jnp.tile(jnp.arange(GRID_W, dtype=F32), rows)
    axis_dim = HEAD_DIM // 2
    inv_freq = ROPE_THETA ** (-jnp.arange(0, axis_dim, 2, dtype=F32) / axis_dim)
    ang = jnp.concatenate([row[:, None] * inv_freq, col[:, None] * inv_freq], axis=-1)
    cos, sin = jnp.cos(ang), jnp.sin(ang)
    cos_t = jnp.tile(cos, (1, LANES // (HEAD_DIM // 2)))
    sin_t = jnp.tile(jnp.concatenate([-sin, sin], axis=-1), (1, LANES // HEAD_DIM))
    return cos_t, sin_t


def _pick(n, pref):
    return pref if n % pref == 0 else n


def kernel(x, c, ctx, c_ctx, mod_w, mod_b, norm1_g, norm2_g, attn_w_in, attn_w_out, attn_sink, diff_lambda, diff_subln_g, gla_w_in, gla_gate_w1, gla_gate_w2, gla_gate_b, gla_norm_g, gla_w_out, ffn_w_up, ffn_conv_w, ffn_conv_b, ffn_w_down, final_norm_g):
    bsz, n_tok, d = x.shape
    n_ctx = ctx.shape[1]
    assert d == D_MODEL and bsz + 1 <= MOD_ROWS
    m_lat, m_ctx = bsz * n_tok, bsz * n_ctx

    c_rows = jnp.concatenate([c, c_ctx[None, :], jnp.zeros((MOD_ROWS - bsz - 1, d), F32)], axis=0)
    mod_all = _modulation(c_rows, mod_w, mod_b)
    cos_t, sin_t = _rope_tables(n_tok)

    tm = _pick(n_tok, 512)
    tmc = _pick(n_ctx, 256)
    lat_tiles = n_tok // tm
    lat_row = lambda i: i // lat_tiles
    ctx_row = lambda i: bsz
    tm_ffn = _pick(n_tok, 1024)
    ffn_row = lambda i: i // (n_tok // tm_ffn)

    h = x.reshape(m_lat, d)
    hc = ctx.reshape(m_ctx, d)
    for layer in range(DEPTH):
        need_ctx = layer < DEPTH - 1
        last = layer == DEPTH - 1
        mod = mod_all[layer].reshape(MOD_ROWS, 6, 1, d)
        n1 = norm1_g[layer].reshape(1, d)
        n2 = norm2_g[layer].reshape(1, d)
        i = layer // 2
        if layer % 2 == 0:
            lam_init = 0.8 - 0.6 * math.exp(-B_LAMBDA_DECAY * layer)
            w_in = attn_w_in[i][:, _attn_in_cols()].astype(BF16)
            w_out = attn_w_out[i]
            w_oa = w_out[_attn_out_rows()].astype(BF16)
            w_ob = w_out[A_Q:].astype(BF16)
            sink = attn_sink[i]
            subln = diff_subln_g[i].reshape(1, LANES)
            aq, bq, bk, ak, av, bv = _attn_inproj(h, mod, n1, w_in, cos_t, sin_t, tm=tm, row_of=lat_row,
                                                  rope=True, tiles_per_seq=lat_tiles)
            caq, cbq, cbk, cak, cav, cbv = _attn_inproj(hc, mod, n1, w_in, cos_t, sin_t, tm=tmc, row_of=ctx_row,
                                                        rope=False, tiles_per_seq=1)
            r3 = lambda a, n: a.reshape(bsz, n, a.shape[-1])
            cak3, cav3, cbk3, cbv3 = r3(cak, n_ctx), r3(cav, n_ctx), r3(cbk, n_ctx), r3(cbv, n_ctx)
            oa = _gqa_window(sink, r3(aq, n_tok), r3(ak, n_tok), r3(av, n_tok), cak3, cav3)
            ob = _diff_attn(diff_lambda[i], subln, r3(bq, n_tok), [(r3(bk, n_tok), r3(bv, n_tok)), (cbk3, cbv3)],
                            tq=_pick(n_tok, 256), lam_init=lam_init)
            h = _outproj([oa.reshape(m_lat, A_Q), ob.reshape(m_lat, B_V)], [w_oa, w_ob], h, mod,
                         tm=tm, row_of=lat_row)
            if need_ctx:
                oca = _gqa_context(sink, r3(caq, n_ctx), cak3, cav3)
                ocb = _diff_attn(diff_lambda[i], subln, r3(cbq, n_ctx), [(cbk3, cbv3)],
                                 tq=_pick(n_ctx, 256), lam_init=lam_init)
                hc = _outproj([oca.reshape(m_ctx, A_Q), ocb.reshape(m_ctx, B_V)], [w_oa, w_ob], hc, mod,
                              tm=tmc, row_of=ctx_row)
        else:
            w1 = gla_gate_w1[i]
            pad = jnp.zeros((d, LANES - 2 * C_GATE_RANK), F32)
            w_in = jnp.concatenate([gla_w_in[i], w1[0], w1[1], pad], axis=1).astype(BF16)
            w2 = gla_gate_w2[i]
            w2bd = jnp.zeros((LANES, 2 * C_QK), F32)
            w2bd = w2bd.at[0:C_GATE_RANK, 0:C_QK].set(w2[0]).at[C_GATE_RANK:2 * C_GATE_RANK, C_QK:].set(w2[1])
            w2bd = w2bd.astype(BF16)
            gb = gla_gate_b[i].reshape(1, 2 * C_QK)
            ng = gla_norm_g[i].reshape(1, C_DV)
            q, k, v, sg, la_f, la_b = _gla_inproj(h, mod, n1, w_in, w2bd, gb, tm=tm, row_of=lat_row)
            qc, kc, vc, sgc, lac_f, lac_b = _gla_inproj(hc, mod, n1, w_in, w2bd, gb, tm=tmc, row_of=ctx_row)
            r3 = lambda a, n: a.reshape(bsz, n, a.shape[-1])
            s_f, s_b = _gla_ctx_state(r3(kc, n_ctx), r3(vc, n_ctx), r3(lac_f, n_ctx), r3(lac_b, n_ctx))
            gt = _pick(n_tok, 256)
            q3, k3, v3 = r3(q, n_tok), r3(k, n_tok), r3(v, n_tok)
            o_b = _gla_scan(q3, k3, v3, r3(la_b, n_tok), s_b, gt=gt, reverse=True)
            og = _gla_scan(q3, k3, v3, r3(la_f, n_tok), s_f, gt=gt, reverse=False,
                           o_other=o_b, sg=r3(sg, n_tok), norm_g=ng)
            h = _outproj([og.reshape(m_lat, C_V)], [gla_w_out[i].astype(BF16)], h, mod, tm=tm, row_of=lat_row)
            if need_ctx:
                z = jnp.zeros((bsz, C_HEADS, C_DV, C_DK), F32)
                qc3, kc3, vc3 = r3(qc, n_ctx), r3(kc, n_ctx), r3(vc, n_ctx)
                gtc = _pick(n_ctx, 256)
                oc_b = _gla_scan(qc3, kc3, vc3, r3(lac_b, n_ctx), z, gt=gtc, reverse=True)
                ogc = _gla_scan(qc3, kc3, vc3, r3(lac_f, n_ctx), z, gt=gtc, reverse=False,
                                o_other=oc_b, sg=r3(sgc, n_ctx), norm_g=ng)
                hc = _outproj([ogc.reshape(m_ctx, C_V)], [gla_w_out[i].astype(BF16)], hc, mod,
                              tm=tmc, row_of=ctx_row)
        w_up = ffn_w_up[layer].astype(BF16)
        w_uu, w_ug = w_up[:, :D_FF], w_up[:, D_FF:]
        w_dn = ffn_w_down[layer].astype(BF16)
        fg = final_norm_g.reshape(1, d)
        h = _conv_ffn(h, mod, n2, w_uu, w_ug, ffn_conv_w[layer], ffn_conv_b[layer], w_dn, fg,
                      tm=tm_ffn, tf=256, row_of=ffn_row, tiles_per_seq=n_tok // tm_ffn, final_norm=last)
        if need_ctx:
            hc = _conv_ffn(hc, mod, n2, w_uu, w_ug, ffn_conv_w[layer], ffn_conv_b[layer], w_dn, fg,
                           tm=tmc, tf=256, row_of=ctx_row, tiles_per_seq=n_ctx // tmc, final_norm=False)
    return h.reshape(bsz, n_tok, d)
```

```python
import functools
import math

import numpy as np
import jax
import jax.numpy as jnp
from jax import lax
from jax.experimental import pallas as pl
from jax.experimental.pallas import tpu as pltpu

F32 = jnp.float32
BF16 = jnp.bfloat16

D_MODEL = 1024
DEPTH = 2
GRID_W = 64
HEAD_DIM = 64
ROPE_THETA = 10000.0
EPS = 1e-6
BLOCK = 128
A_HEADS = 8
A_KV_HEADS = 2
A_GROUP = A_HEADS // A_KV_HEADS
B_HEADS = 4
B_LAMBDA_DECAY = 0.3
A_Q = A_HEADS * HEAD_DIM
A_KV = A_KV_HEADS * HEAD_DIM
B_QK = B_HEADS * 2 * HEAD_DIM
B_V = B_HEADS * 2 * HEAD_DIM
C_HEADS = 4
C_DK = D_MODEL // 2 // C_HEADS
C_DV = D_MODEL // C_HEADS
C_GATE_RANK = 16
C_GATE_NORM = 16.0
C_CHUNK = 64
C_QK = C_HEADS * C_DK
C_V = C_HEADS * C_DV
D_FF = 2816
LANES = 128
MOD_ROWS = 8
VMEM_LIMIT = 56 * 1024 * 1024


def _cparams(sem):
    return pltpu.CompilerParams(dimension_semantics=sem, vmem_limit_bytes=VMEM_LIMIT)


def _dot(a, b):
    return jnp.dot(a, b, preferred_element_type=F32)


def _dot_nt(a, b):
    return lax.dot_general(a, b, (((1,), (1,)), ((), ())), preferred_element_type=F32)


def _dot_tn(a, b):
    return lax.dot_general(a, b, (((0,), (0,)), ((), ())), preferred_element_type=F32)


def _sigmoid(x):
    return 1.0 / (1.0 + jnp.exp(-x))


def _norm_mod(x, g, sc, sh):
    ms = jnp.mean(x * x, axis=-1, keepdims=True)
    return (x * lax.rsqrt(ms + EPS) * g) * (1.0 + sc) + sh


def _mod_kernel(c_ref, w_ref, b_ref, o_ref):
    c = c_ref[...]
    s = (c * _sigmoid(c)).astype(BF16)
    o_ref[...] = _dot(s, w_ref[...].astype(BF16)) + b_ref[...]


def _modulation(c_rows, mod_w, mod_b):
    d = D_MODEL
    return pl.pallas_call(
        _mod_kernel,
        grid=(DEPTH, 6),
        in_specs=[
            pl.BlockSpec((MOD_ROWS, d), lambda l, n: (0, 0)),
            pl.BlockSpec((None, d, d), lambda l, n: (l, 0, n)),
            pl.BlockSpec((None, 1, d), lambda l, n: (l, 0, n)),
        ],
        out_specs=pl.BlockSpec((None, MOD_ROWS, d), lambda l, n: (l, 0, n)),
        out_shape=jax.ShapeDtypeStruct((DEPTH, MOD_ROWS, 6 * d), F32),
        compiler_params=_cparams(("parallel", "parallel")),
        name="modulation",
    )(c_rows, mod_w, mod_b.reshape(DEPTH, 1, 6 * d))


def _mod_spec(k, row_of):
    return pl.BlockSpec((None, None, 1, D_MODEL), lambda i, *_: (row_of(i), k, 0, 0))


_ATTN_GROUPS = (("aq", A_Q, True, 0.125), ("bq", B_QK, True, 0.125), ("bk", B_QK, True, 1.0),
                ("ak", A_KV, True, 1.0), ("av", A_KV, False, 1.0), ("bv", B_V, False, 1.0))


def _rope_chunk(v, cos, sin, first):
    partner = jnp.where(first, pltpu.roll(v, 96, 1), pltpu.roll(v, 32, 1))
    return v * cos + partner * sin


def _attn_inproj_kernel(x_ref, sh_ref, sc_ref, g_ref, w_ref, cos_ref, sin_ref, *out_refs, rope):
    xn = _norm_mod(x_ref[...], g_ref[...], sc_ref[...], sh_ref[...]).astype(BF16)
    tm = xn.shape[0]
    if rope:
        cos = cos_ref[...]
        sin = sin_ref[...]
        lane = lax.broadcasted_iota(jnp.int32, (tm, LANES), 1)
        first = (lane % HEAD_DIM) < (HEAD_DIM // 2)
    col = 0
    for (name, width, roped, scale), o_ref in zip(_ATTN_GROUPS, out_refs):
        y = _dot(xn, w_ref[:, col:col + width])
        col += width
        for c in range(width // LANES):
            v = y[:, c * LANES:(c + 1) * LANES]
            if rope and roped:
                v = _rope_chunk(v, cos, sin, first)
            if scale != 1.0:
                v = v * scale
            o_ref[:, c * LANES:(c + 1) * LANES] = v.astype(BF16)


def _attn_inproj(x2d, mod, norm_g, w, cos_t, sin_t, *, tm, row_of, rope, tiles_per_seq):
    m, d = x2d.shape
    n_all = w.shape[1]
    out_shape = [jax.ShapeDtypeStruct((m, width), BF16) for (_, width, _, _) in _ATTN_GROUPS]
    out_specs = [pl.BlockSpec((tm, width), lambda i: (i, 0)) for (_, width, _, _) in _ATTN_GROUPS]
    return pl.pallas_call(
        functools.partial(_attn_inproj_kernel, rope=rope),
        grid=(m // tm,),
        in_specs=[
            pl.BlockSpec((tm, d), lambda i: (i, 0)),
            _mod_spec(0, row_of), _mod_spec(1, row_of),
            pl.BlockSpec((1, d), lambda i: (0, 0)),
            pl.BlockSpec((d, n_all), lambda i: (0, 0)),
            pl.BlockSpec((tm, LANES), lambda i: (i % tiles_per_seq, 0)),
            pl.BlockSpec((tm, LANES), lambda i: (i % tiles_per_seq, 0)),
        ],
        out_specs=out_specs,
        out_shape=out_shape,
        compiler_params=_cparams(("parallel",)),
        name="attn_inproj_rope" if rope else "attn_inproj_ctx",
    )(x2d, mod, mod, norm_g, w, cos_t, sin_t)


def _gqa_kernel(sink_ref, q_ref, *refs, window, nb):
    if window:
        kp_ref, kc_ref, kn_ref, vp_ref, vc_ref, vn_ref, kx_ref, vx_ref, o_ref = refs
        keys = jnp.concatenate([kp_ref[...], kc_ref[...], kn_ref[...], kx_ref[...]], axis=0)
        vals = jnp.concatenate([vp_ref[...], vc_ref[...], vn_ref[...], vx_ref[...]], axis=0)
    else:
        kx_ref, vx_ref, o_ref = refs
        keys = kx_ref[...]
        vals = vx_ref[...]
    tq = q_ref.shape[0]
    ns = keys.shape[0]
    nq = A_GROUP * tq
    if window:
        n = pl.program_id(1)
        s = lax.broadcasted_iota(jnp.int32, (ns, nq), 0)
        t = lax.broadcasted_iota(jnp.int32, (ns, nq), 1) & (tq - 1)
        has_prev = jnp.where(n > 0, 1, 0)
        has_next = jnp.where(n < nb - 1, 1, 0)
        lower = t * has_prev + BLOCK * (1 - has_prev)
        upper = (t + 1) * has_next + (2 * BLOCK - 1)
        ninf = jnp.float32(-jnp.inf)
        bias = jnp.where(s < lower, ninf, jnp.where(s > upper, jnp.where(s < 3 * BLOCK, ninf, 0.0), 0.0))
    lane = lax.broadcasted_iota(jnp.int32, (1, LANES), 1)
    lo = lane < HEAD_DIM
    half = (jnp.where(lo, 1.0, 0.0).astype(F32), jnp.where(lo, 0.0, 1.0).astype(F32))
    qf = [q_ref[:, j * LANES:(j + 1) * LANES].astype(F32) for j in range(A_GROUP)]
    outs = []
    scs = [_dot_nt(keys, jnp.concatenate([(q * half[kv]).astype(BF16) for q in qf], axis=0))
           for kv in range(A_KV_HEADS)]
    for kv in range(A_KV_HEADS):
        sc = scs[kv]
        if window:
            sc = sc + bias
        sk = jnp.concatenate([jnp.full((1, tq), sink_ref[kv * A_GROUP + j], F32) for j in range(A_GROUP)], axis=1)
        mx = jnp.maximum(jnp.max(sc, axis=0, keepdims=True), sk)
        p = jnp.exp(sc - mx)
        den = jnp.sum(p, axis=0, keepdims=True) + jnp.exp(sk - mx)
        pn = (p * (1.0 / den)).astype(BF16)
        outs.append(_dot_tn(vals, pn))
    row = lax.broadcasted_iota(jnp.int32, (LANES, 1), 0)
    o_t = jnp.where(row < HEAD_DIM, outs[0], outs[1])
    for j in range(A_GROUP):
        o_ref[:, j * LANES:(j + 1) * LANES] = o_t[:, j * tq:(j + 1) * tq].T.astype(BF16)


def _gqa_window(sink, aq, ak, av, cak, cav):
    bsz, n_tok, _ = aq.shape
    n_ctx = cak.shape[1]
    nb = n_tok // BLOCK
    kv_prev = pl.BlockSpec((None, BLOCK, A_KV), lambda b, n: (b, jnp.maximum(n - 1, 0), 0))
    kv_cur = pl.BlockSpec((None, BLOCK, A_KV), lambda b, n: (b, n, 0))
    kv_next = pl.BlockSpec((None, BLOCK, A_KV), lambda b, n: (b, jnp.minimum(n + 1, nb - 1), 0))
    kv_ctx = pl.BlockSpec((None, n_ctx, A_KV), lambda b, n: (b, 0, 0))
    return pl.pallas_call(
        functools.partial(_gqa_kernel, window=True, nb=nb),
        grid=(bsz, nb),
        in_specs=[pl.BlockSpec(memory_space=pltpu.SMEM),
                  pl.BlockSpec((None, BLOCK, A_Q), lambda b, n: (b, n, 0)),
                  kv_prev, kv_cur, kv_next, kv_prev, kv_cur, kv_next, kv_ctx, kv_ctx],
        out_specs=pl.BlockSpec((None, BLOCK, A_Q), lambda b, n: (b, n, 0)),
        out_shape=jax.ShapeDtypeStruct((bsz, n_tok, A_Q), BF16),
        compiler_params=_cparams(("parallel", "parallel")),
        name="gqa_window",
    )(sink, aq, ak, ak, ak, av, av, av, cak, cav)


def _gqa_context(sink, caq, cak, cav):
    bsz, n_ctx, _ = caq.shape
    nb = n_ctx // BLOCK
    kv_ctx = pl.BlockSpec((None, n_ctx, A_KV), lambda b, n: (b, 0, 0))
    return pl.pallas_call(
        functools.partial(_gqa_kernel, window=False, nb=nb),
        grid=(bsz, nb),
        in_specs=[pl.BlockSpec(memory_space=pltpu.SMEM),
                  pl.BlockSpec((None, BLOCK, A_Q), lambda b, n: (b, n, 0)),
                  kv_ctx, kv_ctx],
        out_specs=pl.BlockSpec((None, BLOCK, A_Q), lambda b, n: (b, n, 0)),
        out_shape=jax.ShapeDtypeStruct((bsz, n_ctx, A_Q), BF16),
        compiler_params=_cparams(("parallel", "parallel")),
        name="gqa_context",
    )(sink, caq, cak, cav)


XPOSE_ROWS = 512
ONES_ROWS = 16


def _diff_attn_kernel(lam_ref, g_ref, q_ref, *refs, nseg, lam_init, kc):
    k_refs = refs[0:2 * nseg:2]
    v_refs = refs[1:2 * nseg:2]
    o_ref, vt_ref = refs[2 * nseg:]
    lv = lam_ref[...]
    lam = (jnp.exp(jnp.sum(lv[0:1] * lv[1:2], axis=-1, keepdims=True))
           - jnp.exp(jnp.sum(lv[2:3] * lv[3:4], axis=-1, keepdims=True)) + lam_init)
    q = q_ref[...].astype(F32)
    lane = lax.broadcasted_iota(jnp.int32, (1, LANES), 1)
    lo = lane < HEAD_DIM
    q0 = (q * jnp.where(lo, 1.0, 0.0).astype(F32)).astype(BF16)
    q1 = (q * jnp.where(lo, 0.0, 1.0).astype(F32)).astype(BF16)
    @pl.when(pl.program_id(2) == 0)
    def _():
        off = 0
        for v_ref in v_refs:
            ns = v_ref.shape[0]
            for c0 in range(0, ns, XPOSE_ROWS):
                n = min(XPOSE_ROWS, ns - c0)
                vt_ref[0:LANES, off + c0:off + c0 + n] = v_ref[c0:c0 + n, :].astype(F32).T.astype(BF16)
            off += ns
        vt_ref[LANES:, :] = jnp.ones((vt_ref.shape[0] - LANES, vt_ref.shape[1]), BF16)

    chunks = []
    off = 0
    for k_ref in k_refs:
        ns = k_ref.shape[0]
        step = min(kc, ns)
        chunks += [(k_ref, c0, step, off + c0) for c0 in range(0, ns, step)]
        off += ns

    qms = (q0, q1)

    def scores(mi, ci):
        k_ref, c0, step, _ = chunks[ci]
        return _dot_nt(k_ref[c0:c0 + step, :], qms[mi])

    run_max = [None, None]
    accs = [None, None]

    def probs(mi, sc):
        mc = jnp.max(sc, axis=0, keepdims=True)
        if run_max[mi] is None:
            run_max[mi], alpha = mc, None
        else:
            m_new = jnp.maximum(run_max[mi], mc)
            alpha = jnp.exp(run_max[mi] - m_new)
            run_max[mi] = m_new
        return jnp.exp((sc - run_max[mi]).astype(BF16)), alpha

    def accumulate(mi, ci, pb, alpha):
        _, _, step, g0 = chunks[ci]
        pv = _dot(vt_ref[:, g0:g0 + step], pb)
        accs[mi] = pv if alpha is None else accs[mi] * alpha + pv

    n_ch = len(chunks)
    sc_q = {ci: [scores(mi, ci) for mi in range(2)] for ci in range(min(2, n_ch))}
    pb_q = {0: [probs(mi, sc_q[0][mi]) for mi in range(2)]}
    for ci in range(n_ch):
        if ci + 2 < n_ch:
            sc_q[ci + 2] = [scores(mi, ci + 2) for mi in range(2)]
        if ci + 1 < n_ch:
            sc_pair = sc_q.pop(ci + 1)
            pb_q[ci + 1] = [probs(mi, sc_pair[mi]) for mi in range(2)]
        for mi, (pb, alpha) in enumerate(pb_q.pop(ci)):
            accumulate(mi, ci, pb, alpha)
    stats = [(run_max[mi], accs[mi]) for mi in range(2)]
    r0 = 1.0 / stats[0][1][LANES:LANES + 1, :]
    r1 = lam / stats[1][1][LANES:LANES + 1, :]
    o = (stats[0][1][0:LANES, :] * r0 - stats[1][1][0:LANES, :] * r1).T
    ms = jnp.mean(o * o, axis=-1, keepdims=True)
    o_ref[...] = ((o * lax.rsqrt(ms + EPS) * g_ref[...]) * (1.0 - lam_init)).astype(BF16)


def _diff_attn(lam_vec, subln_g, q, kv_list, *, tq, lam_init, kc=256):
    bsz, n_q, _ = q.shape
    in_specs = [pl.BlockSpec((4, HEAD_DIM), lambda b, h, i: (0, 0)),
                pl.BlockSpec((1, LANES), lambda b, h, i: (0, 0)),
                pl.BlockSpec((None, tq, LANES), lambda b, h, i: (b, i, h))]
    args = [lam_vec, subln_g, q]
    for k, v in kv_list:
        ns = k.shape[1]
        in_specs += [pl.BlockSpec((None, ns, LANES), lambda b, h, i: (b, 0, h))] * 2
        args += [k, v]
    return pl.pallas_call(
        functools.partial(_diff_attn_kernel, nseg=len(kv_list), lam_init=lam_init, kc=kc),
        grid=(bsz, B_HEADS, n_q // tq),
        in_specs=in_specs,
        out_specs=pl.BlockSpec((None, tq, LANES), lambda b, h, i: (b, i, h)),
        out_shape=jax.ShapeDtypeStruct((bsz, n_q, B_V), BF16),
        scratch_shapes=[pltpu.VMEM((LANES + ONES_ROWS, sum(k.shape[1] for k, _ in kv_list)), BF16)],
        compiler_params=_cparams(("parallel", "parallel", "arbitrary")),
        name="diff_attn_%dseg" % len(kv_list),
    )(*args)


def _outproj_kernel(*refs, n_in):
    a_refs = refs[:n_in]
    w_refs = refs[n_in:2 * n_in]
    h_ref, gt_ref, o_ref = refs[2 * n_in:]
    y = None
    for a, w in zip(a_refs, w_refs):
        c = _dot(a[...], w[...])
        y = c if y is None else y + c
    o_ref[...] = h_ref[...] + gt_ref[...] * y


def _outproj(acts, ws, h2d, mod, *, tm, row_of):
    m, d = h2d.shape
    n_in = len(acts)
    in_specs = [pl.BlockSpec((tm, a.shape[1]), lambda i: (i, 0)) for a in acts]
    in_specs += [pl.BlockSpec(w.shape, lambda i: (0, 0)) for w in ws]
    in_specs += [pl.BlockSpec((tm, d), lambda i: (i, 0)), _mod_spec(2, row_of)]
    return pl.pallas_call(
        functools.partial(_outproj_kernel, n_in=n_in),
        grid=(m // tm,),
        in_specs=in_specs,
        out_specs=pl.BlockSpec((tm, d), lambda i: (i, 0)),
        out_shape=jax.ShapeDtypeStruct((m, d), F32),
        compiler_params=_cparams(("parallel",)),
        name="outproj_residual",
    )(*acts, *ws, h2d, mod)


FFN_HALO = 16
FFN_TF = 256


def _ffn_kernel(h_ref, hp_ref, hn_ref, sh_ref, sc_ref, gt_ref, ng_ref, wu_ref, wg_ref, cwu_ref, cwg_ref,
                wd_ref, fg_ref, o_ref, xn_ref, hu_a, hg_a, hu_b, hg_b, acc_ref,
                *, tiles_per_seq, final_norm):
    i = pl.program_id(0)
    tm = h_ref.shape[0]
    hl = FFN_HALO
    nj = wu_ref.shape[0]

    pos = i % tiles_per_seq
    g, sc, sh = ng_ref[...], sc_ref[...], sh_ref[...]
    xn_ref[hl:hl + tm, :] = _norm_mod(h_ref[...], g, sc, sh).astype(BF16)
    prev = _norm_mod(hp_ref[...], g, sc, sh) * jnp.where(pos > 0, 1.0, 0.0)
    nxt = _norm_mod(hn_ref[...], g, sc, sh) * jnp.where(pos < tiles_per_seq - 1, 1.0, 0.0)
    xn_ref[0:hl, :] = prev.astype(BF16)
    xn_ref[hl + tm:2 * hl + tm, :] = nxt.astype(BF16)
    acc_ref[...] = jnp.zeros_like(acc_ref)

    def up(j, hu_ref, hg_ref):
        xn = xn_ref[...]
        hu_ref[...] = _dot(xn, wu_ref[j])
        hg_ref[...] = _dot(xn, wg_ref[j])

    def conv(ref, cw):
        return (ref[hl - 1:hl - 1 + tm, :] * cw[0:1] + ref[hl:hl + tm, :] * cw[1:2]
                + ref[hl + 1:hl + 1 + tm, :] * cw[2:3] + cw[3:4])

    def act_down(j, hu_ref, hg_ref):
        u = conv(hu_ref, cwu_ref[j])
        gt = conv(hg_ref, cwg_ref[j])
        a = (gt * _sigmoid(gt) * u).astype(BF16)
        acc_ref[...] += _dot(a, wd_ref[j])

    up(0, hu_a, hg_a)

    def pair(jj, carry):
        j = 2 * jj
        up(j + 1, hu_b, hg_b)
        act_down(j, hu_a, hg_a)
        up(j + 2, hu_a, hg_a)
        act_down(j + 1, hu_b, hg_b)
        return carry

    assert nj % 2 == 1
    lax.fori_loop(0, (nj - 1) // 2, pair, 0)
    act_down(nj - 1, hu_a, hg_a)

    y = h_ref[...] + gt_ref[...] * acc_ref[...]
    if final_norm:
        ms = jnp.mean(y * y, axis=-1, keepdims=True)
        y = y * lax.rsqrt(ms + EPS) * fg_ref[...]
    o_ref[...] = y


def _ffn_weights(w_up, conv_w, conv_b, w_down, tf):
    d = w_up.shape[0]
    nj = D_FF // tf
    chunked = lambda w: w.reshape(w.shape[0], nj, tf).transpose(1, 0, 2)
    w_uu = chunked(w_up[:, :D_FF]).astype(BF16)
    w_ug = chunked(w_up[:, D_FF:]).astype(BF16)
    cw = jnp.concatenate([conv_w, conv_b[None, :]], axis=0)
    return (w_uu, w_ug, chunked(cw[:, :D_FF]), chunked(cw[:, D_FF:]), w_down.reshape(nj, tf, d).astype(BF16))


def _conv_ffn(h2d, mod, norm_g, weights, final_g, *, tm, row_of, tiles_per_seq, final_norm):
    m, d = h2d.shape
    hl = FFN_HALO
    nhb = m // hl
    r = tm // hl
    w_uu, w_ug, cwu, cwg, w_dn = weights
    tf = w_uu.shape[2]
    resident = lambda a: pl.BlockSpec(a.shape, lambda i: (0,) * a.ndim, pipeline_mode=pl.Buffered(1))
    return pl.pallas_call(
        functools.partial(_ffn_kernel, tiles_per_seq=tiles_per_seq, final_norm=final_norm),
        grid=(m // tm,),
        in_specs=[
            pl.BlockSpec((tm, d), lambda i: (i, 0)),
            pl.BlockSpec((hl, d), lambda i: (jnp.maximum(i * r - 1, 0), 0)),
            pl.BlockSpec((hl, d), lambda i: (jnp.minimum((i + 1) * r, nhb - 1), 0)),
            _mod_spec(3, row_of), _mod_spec(4, row_of), _mod_spec(5, row_of),
            pl.BlockSpec((1, d), lambda i: (0, 0)),
            resident(w_uu), resident(w_ug), resident(cwu), resident(cwg), resident(w_dn),
            pl.BlockSpec((1, d), lambda i: (0, 0)),
        ],
        out_specs=pl.BlockSpec((tm, d), lambda i: (i, 0)),
        out_shape=jax.ShapeDtypeStruct((m, d), F32),
        scratch_shapes=[pltpu.VMEM((tm + 2 * hl, d), BF16)]
        + [pltpu.VMEM((tm + 2 * hl, tf), F32)] * 4
        + [pltpu.VMEM((tm, d), F32)],
        compiler_params=_cparams(("parallel",)),
        name="conv_ffn",
    )(h2d, h2d, h2d, mod, mod, mod, norm_g, w_uu, w_ug, cwu, cwg, w_dn, final_g)


def _gla_inproj_kernel(x_ref, sh_ref, sc_ref, g_ref, w_ref, w2_ref, gb_ref,
                       q_ref, k_ref, v_ref, sg_ref, laf_ref, lab_ref):
    xn = _norm_mod(x_ref[...], g_ref[...], sc_ref[...], sh_ref[...]).astype(BF16)
    q_ref[...] = (_dot(xn, w_ref[:, 0:C_QK]) * (C_DK ** -0.5)).astype(BF16)
    k_ref[...] = _dot(xn, w_ref[:, C_QK:2 * C_QK]).astype(BF16)
    for c in range(C_V // C_QK):
        lo = 2 * C_QK + c * C_QK
        v_ref[:, c * C_QK:(c + 1) * C_QK] = _dot(xn, w_ref[:, lo:lo + C_QK]).astype(BF16)
    for c in range(C_V // C_QK):
        lo = 2 * C_QK + C_V + c * C_QK
        gg = _dot(xn, w_ref[:, lo:lo + C_QK])
        sg_ref[:, c * C_QK:(c + 1) * C_QK] = (gg * _sigmoid(gg)).astype(BF16)
    lo = 2 * C_QK + 2 * C_V
    r = _dot(xn, w_ref[:, lo:lo + LANES]).astype(BF16)
    for dr, la_ref in enumerate((laf_ref, lab_ref)):
        z = _dot(r, w2_ref[:, dr * C_QK:(dr + 1) * C_QK]) + gb_ref[:, dr * C_QK:(dr + 1) * C_QK]
        la_ref[...] = (jnp.minimum(z, 0.0) - jnp.log(1.0 + jnp.exp(-jnp.abs(z)))) * (1.0 / C_GATE_NORM)


def _gla_inproj(x2d, mod, norm_g, w, w2, gb, *, tm, row_of):
    m, d = x2d.shape
    widths = (C_QK, C_QK, C_V, C_V, C_QK, C_QK)
    dts = (BF16, BF16, BF16, BF16, F32, F32)
    return pl.pallas_call(
        _gla_inproj_kernel,
        grid=(m // tm,),
        in_specs=[
            pl.BlockSpec((tm, d), lambda i: (i, 0)),
            _mod_spec(0, row_of), _mod_spec(1, row_of),
            pl.BlockSpec((1, d), lambda i: (0, 0)),
            pl.BlockSpec(w.shape, lambda i: (0, 0)),
            pl.BlockSpec(w2.shape, lambda i: (0, 0)),
            pl.BlockSpec(gb.shape, lambda i: (0, 0)),
        ],
        out_specs=[pl.BlockSpec((tm, wd), lambda i: (i, 0)) for wd in widths],
        out_shape=[jax.ShapeDtypeStruct((m, wd), dt) for wd, dt in zip(widths, dts)],
        compiler_params=_cparams(("parallel",)),
        name="gla_inproj",
    )(x2d, mod, mod, norm_g, w, w2, gb)


def _tri(n, reverse):
    r = lax.broadcasted_iota(jnp.int32, (n, n), 0)
    c = lax.broadcasted_iota(jnp.int32, (n, n), 1)
    return (c >= r) if reverse else (c <= r)


def _cumsum_rows(la, tri_bf):
    hi = la.astype(BF16)
    r1 = la - hi.astype(F32)
    mid = r1.astype(BF16)
    lo = (r1 - mid.astype(F32)).astype(BF16)
    return _dot(tri_bf, hi) + _dot(tri_bf, mid) + _dot(tri_bf, lo)


def _gla_ctx_state_kernel(k_ref, v_ref, laf_ref, lab_ref, sf_ref, sb_ref):
    n = k_ref.shape[0]
    for reverse, la_ref, s_ref in ((False, laf_ref, sf_ref), (True, lab_ref, sb_ref)):
        tri = jnp.where(_tri(n, reverse), 1.0, 0.0).astype(BF16)
        b = _cumsum_rows(la_ref[...], tri)
        b_end = b[0:1, :] if reverse else b[n - 1:n, :]
        kw = (k_ref[...].astype(F32) * jnp.exp(b_end - b)).astype(BF16)
        for h in range(C_HEADS):
            s_ref[h] = _dot_tn(v_ref[:, h * C_DV:(h + 1) * C_DV], kw[:, h * C_DK:(h + 1) * C_DK])


def _gla_ctx_state(kc, vc, lac_f, lac_b):
    bsz, n, _ = kc.shape
    s_shape = jax.ShapeDtypeStruct((bsz, C_HEADS, C_DV, C_DK), F32)
    s_spec = pl.BlockSpec((None, C_HEADS, C_DV, C_DK), lambda b: (b, 0, 0, 0))
    return pl.pallas_call(
        _gla_ctx_state_kernel,
        grid=(bsz,),
        in_specs=[pl.BlockSpec((None, n, C_QK), lambda b: (b, 0, 0)),
                  pl.BlockSpec((None, n, C_V), lambda b: (b, 0, 0)),
                  pl.BlockSpec((None, n, C_QK), lambda b: (b, 0, 0)),
                  pl.BlockSpec((None, n, C_QK), lambda b: (b, 0, 0))],
        out_specs=[s_spec, s_spec],
        out_shape=[s_shape, s_shape],
        compiler_params=_cparams(("parallel",)),
        name="gla_ctx_state",
    )(kc, vc, lac_f, lac_b)


def _gla_scan_kernel(q_ref, k_ref, v_ref, la_ref, s0_ref, *refs, reverse, final):
    if final:
        ob_ref, sg_ref, ng_ref, o_ref, st_ref = refs
    else:
        o_ref, st_ref = refs
    gidx = pl.program_id(1)

    @pl.when(gidx == 0)
    def _():
        st_ref[...] = s0_ref[...]

    gt = q_ref.shape[0]
    nchunk = gt // C_CHUNK
    c = C_CHUNK
    tri = _tri(c, reverse)
    tri_bf = jnp.where(tri, 1.0, 0.0).astype(BF16)
    order = range(nchunk - 1, -1, -1) if reverse else range(nchunk)
    for ci in order:
        rows = slice(ci * c, (ci + 1) * c)
        for h in range(C_HEADS):
            kcols = slice(h * C_DK, (h + 1) * C_DK)
            vcols = slice(h * C_DV, (h + 1) * C_DV)
            b = _cumsum_rows(la_ref[rows, kcols], tri_bf)
            b_mid = b[c // 2:c // 2 + 1, :]
            b_end = b[0:1, :] if reverse else b[c - 1:c, :]
            qf = q_ref[rows, kcols].astype(F32)
            kf = k_ref[rows, kcols].astype(F32)
            vv = v_ref[rows, vcols]
            q_in = (qf * jnp.exp(b - b_mid)).astype(BF16)
            k_in = (kf * jnp.exp(b_mid - b)).astype(BF16)
            sc = jnp.where(tri, _dot_nt(q_in, k_in), 0.0).astype(BF16)
            st = st_ref[h]
            o = _dot(sc, vv) + _dot_nt((qf * jnp.exp(b)).astype(BF16), st.astype(BF16))
            k_out = (kf * jnp.exp(b_end - b)).astype(BF16)
            st_ref[h] = st * jnp.exp(b_end) + _dot_tn(vv, k_out)
            if final:
                o = o + ob_ref[rows, vcols]
                ms = jnp.mean(o * o, axis=-1, keepdims=True)
                o = (o * lax.rsqrt(ms + EPS) * ng_ref[...]) * sg_ref[rows, vcols].astype(F32)
                o_ref[rows, vcols] = o.astype(BF16)
            else:
                o_ref[rows, vcols] = o


def _gla_scan(q, k, v, la, s0, *, gt, reverse, o_other=None, sg=None, norm_g=None):
    bsz, n_tok, _ = q.shape
    ng = n_tok // gt
    final = o_other is not None
    gi = (lambda b, g: (b, ng - 1 - g, 0)) if reverse else (lambda b, g: (b, g, 0))
    in_specs = [pl.BlockSpec((None, gt, C_QK), gi), pl.BlockSpec((None, gt, C_QK), gi),
                pl.BlockSpec((None, gt, C_V), gi), pl.BlockSpec((None, gt, C_QK), gi),
                pl.BlockSpec((None, C_HEADS, C_DV, C_DK), lambda b, g: (b, 0, 0, 0))]
    args = [q, k, v, la, s0]
    if final:
        in_specs += [pl.BlockSpec((None, gt, C_V), gi), pl.BlockSpec((None, gt, C_V), gi),
                     pl.BlockSpec((1, C_DV), lambda b, g: (0, 0))]
        args += [o_other, sg, norm_g]
    return pl.pallas_call(
        functools.partial(_gla_scan_kernel, reverse=reverse, final=final),
        grid=(bsz, ng),
        in_specs=in_specs,
        out_specs=pl.BlockSpec((None, gt, C_V), gi),
        out_shape=jax.ShapeDtypeStruct((bsz, n_tok, C_V), BF16 if final else F32),
        scratch_shapes=[pltpu.VMEM((C_HEADS, C_DV, C_DK), F32)],
        compiler_params=_cparams(("parallel", "arbitrary")),
        name="gla_scan_fwd_final" if final else "gla_scan_bwd",
    )(*args)


def _pair_split(n_heads):
    base = np.concatenate([np.arange(0, HEAD_DIM, 2), np.arange(1, HEAD_DIM, 2)])
    return np.concatenate([h * HEAD_DIM + base for h in range(n_heads)])


_A_HEAD_ORDER = np.array([kv * A_GROUP + j for j in range(A_GROUP) for kv in range(A_KV_HEADS)])


def _attn_in_cols():
    aq = (_A_HEAD_ORDER[:, None] * HEAD_DIM + _pair_split(1)[None, :]).reshape(-1)
    o_ak, o_av, o_bq = A_Q, A_Q + A_KV, A_Q + 2 * A_KV
    o_bk, o_bv = o_bq + B_QK, o_bq + 2 * B_QK
    return np.concatenate([aq, o_bq + _pair_split(2 * B_HEADS), o_bk + _pair_split(2 * B_HEADS),
                           o_ak + _pair_split(A_KV_HEADS), o_av + np.arange(A_KV), o_bv + np.arange(B_V)])


def _attn_out_rows():
    oa = (_A_HEAD_ORDER[:, None] * HEAD_DIM + np.arange(HEAD_DIM)[None, :]).reshape(-1)
    return oa


def _rope_tables(n_tok):
    rows = n_tok // GRID_W
    row = jnp.repeat(jnp.arange(rows, dtype=F32), GRID_W)
    col = jnp.tile(jnp.arange(GRID_W, dtype=F32), rows)
    axis_dim = HEAD_DIM // 2
    inv_freq = ROPE_THETA ** (-jnp.arange(0, axis_dim, 2, dtype=F32) / axis_dim)
    ang = jnp.concatenate([row[:, None] * inv_freq, col[:, None] * inv_freq], axis=-1)
    cos, sin = jnp.cos(ang), jnp.sin(ang)
    cos_t = jnp.tile(cos, (1, LANES // (HEAD_DIM // 2)))
    sin_t = jnp.tile(jnp.concatenate([-sin, sin], axis=-1), (1, LANES // HEAD_DIM))
    return cos_t, sin_t


def _pick(n, pref):
    return pref if n % pref == 0 else n


def kernel(x, c, ctx, c_ctx, mod_w, mod_b, norm1_g, norm2_g, attn_w_in, attn_w_out, attn_sink, diff_lambda, diff_subln_g, gla_w_in, gla_gate_w1, gla_gate_w2, gla_gate_b, gla_norm_g, gla_w_out, ffn_w_up, ffn_conv_w, ffn_conv_b, ffn_w_down, final_norm_g):
    bsz, n_tok, d = x.shape
    n_ctx = ctx.shape[1]
    assert d == D_MODEL and bsz + 1 <= MOD_ROWS
    m_lat, m_ctx = bsz * n_tok, bsz * n_ctx

    c_rows = jnp.concatenate([c, c_ctx[None, :], jnp.zeros((MOD_ROWS - bsz - 1, d), F32)], axis=0)
    mod_all = _modulation(c_rows, mod_w, mod_b)
    cos_t, sin_t = _rope_tables(n_tok)

    tm = _pick(n_tok, 512)
    tmc = _pick(n_ctx, 256)
    lat_tiles = n_tok // tm
    lat_row = lambda i: i // lat_tiles
    ctx_row = lambda i: bsz
    tm_ffn = _pick(n_tok, 512)
    ffn_row = lambda i: i // (n_tok // tm_ffn)

    h = x.reshape(m_lat, d)
    hc = ctx.reshape(m_ctx, d)
    for layer in range(DEPTH):
        need_ctx = layer < DEPTH - 1
        last = layer == DEPTH - 1
        mod = mod_all[layer].reshape(MOD_ROWS, 6, 1, d)
        n1 = norm1_g[layer].reshape(1, d)
        n2 = norm2_g[layer].reshape(1, d)
        i = layer // 2
        if layer % 2 == 0:
            lam_init = 0.8 - 0.6 * math.exp(-B_LAMBDA_DECAY * layer)
            w_in = attn_w_in[i][:, _attn_in_cols()].astype(BF16)
            w_out = attn_w_out[i]
            w_oa = w_out[_attn_out_rows()].astype(BF16)
            w_ob = w_out[A_Q:].astype(BF16)
            sink = attn_sink[i]
            subln = diff_subln_g[i].reshape(1, LANES)
            aq, bq, bk, ak, av, bv = _attn_inproj(h, mod, n1, w_in, cos_t, sin_t, tm=tm, row_of=lat_row,
                                                  rope=True, tiles_per_seq=lat_tiles)
            caq, cbq, cbk, cak, cav, cbv = _attn_inproj(hc, mod, n1, w_in, cos_t, sin_t, tm=tmc, row_of=ctx_row,
                                                        rope=False, tiles_per_seq=1)
            r3 = lambda a, n: a.reshape(bsz, n, a.shape[-1])
            cak3, cav3, cbk3, cbv3 = r3(cak, n_ctx), r3(cav, n_ctx), r3(cbk, n_ctx), r3(cbv, n_ctx)
            oa = _gqa_window(sink, r3(aq, n_tok), r3(ak, n_tok), r3(av, n_tok), cak3, cav3)
            ob = _diff_attn(diff_lambda[i], subln, r3(bq, n_tok), [(r3(bk, n_tok), r3(bv, n_tok)), (cbk3, cbv3)],
                            tq=_pick(n_tok, 512), lam_init=lam_init)
            h = _outproj([oa.reshape(m_lat, A_Q), ob.reshape(m_lat, B_V)], [w_oa, w_ob], h, mod,
                         tm=tm, row_of=lat_row)
            if need_ctx:
                oca = _gqa_context(sink, r3(caq, n_ctx), cak3, cav3)
                ocb = _diff_attn(diff_lambda[i], subln, r3(cbq, n_ctx), [(cbk3, cbv3)],
                                 tq=_pick(n_ctx, 256), lam_init=lam_init)
                hc = _outproj([oca.reshape(m_ctx, A_Q), ocb.reshape(m_ctx, B_V)], [w_oa, w_ob], hc, mod,
                              tm=tmc, row_of=ctx_row)
        else:
            w1 = gla_gate_w1[i]
            pad = jnp.zeros((d, LANES - 2 * C_GATE_RANK), F32)
            w_in = jnp.concatenate([gla_w_in[i], w1[0], w1[1], pad], axis=1).astype(BF16)
            w2 = gla_gate_w2[i]
            w2bd = jnp.zeros((LANES, 2 * C_QK), F32)
            w2bd = w2bd.at[0:C_GATE_RANK, 0:C_QK].set(w2[0]).at[C_GATE_RANK:2 * C_GATE_RANK, C_QK:].set(w2[1])
            w2bd = w2bd.astype(BF16)
            gb = gla_gate_b[i].reshape(1, 2 * C_QK)
            ng = gla_norm_g[i].reshape(1, C_DV)
            q, k, v, sg, la_f, la_b = _gla_inproj(h, mod, n1, w_in, w2bd, gb, tm=tm, row_of=lat_row)
            qc, kc, vc, sgc, lac_f, lac_b = _gla_inproj(hc, mod, n1, w_in, w2bd, gb, tm=tmc, row_of=ctx_row)
            r3 = lambda a, n: a.reshape(bsz, n, a.shape[-1])
            s_f, s_b = _gla_ctx_state(r3(kc, n_ctx), r3(vc, n_ctx), r3(lac_f, n_ctx), r3(lac_b, n_ctx))
            gt = _pick(n_tok, 256)
            q3, k3, v3 = r3(q, n_tok), r3(k, n_tok), r3(v, n_tok)
            o_b = _gla_scan(q3, k3, v3, r3(la_b, n_tok), s_b, gt=gt, reverse=True)
            og = _gla_scan(q3, k3, v3, r3(la_f, n_tok), s_f, gt=gt, reverse=False,
                           o_other=o_b, sg=r3(sg, n_tok), norm_g=ng)
            h = _outproj([og.reshape(m_lat, C_V)], [gla_w_out[i].astype(BF16)], h, mod, tm=tm, row_of=lat_row)
            if need_ctx:
                z = jnp.zeros((bsz, C_HEADS, C_DV, C_DK), F32)
                qc3, kc3, vc3 = r3(qc, n_ctx), r3(kc, n_ctx), r3(vc, n_ctx)
                gtc = _pick(n_ctx, 256)
                oc_b = _gla_scan(qc3, kc3, vc3, r3(lac_b, n_ctx), z, gt=gtc, reverse=True)
                ogc = _gla_scan(qc3, kc3, vc3, r3(lac_f, n_ctx), z, gt=gtc, reverse=False,
                                o_other=oc_b, sg=r3(sgc, n_ctx), norm_g=ng)
                hc = _outproj([ogc.reshape(m_ctx, C_V)], [gla_w_out[i].astype(BF16)], hc, mod,
                              tm=tmc, row_of=ctx_row)
        ffn_w = _ffn_weights(ffn_w_up[layer], ffn_conv_w[layer], ffn_conv_b[layer], ffn_w_down[layer], FFN_TF)
        fg = final_norm_g.reshape(1, d)
        h = _conv_ffn(h, mod, n2, ffn_w, fg, tm=tm_ffn, row_of=ffn_row, tiles_per_seq=n_tok // tm_ffn,
                      final_norm=last)
        if need_ctx:
            hc = _conv_ffn(hc, mod, n2, ffn_w, fg, tm=tmc, row_of=ctx_row, tiles_per_seq=n_ctx // tmc,
                           final_norm=False)
    return h.reshape(bsz, n_tok, d)
```

```python
import functools
import math

import numpy as np
import jax
import jax.numpy as jnp
from jax import lax
from jax.experimental import pallas as pl
from jax.experimental.pallas import tpu as pltpu

F32 = jnp.float32
BF16 = jnp.bfloat16

D_MODEL = 1024
DEPTH = 2
GRID_W = 64
HEAD_DIM = 64
ROPE_THETA = 10000.0
EPS = 1e-6
BLOCK = 128
A_HEADS = 8
A_KV_HEADS = 2
A_GROUP = A_HEADS // A_KV_HEADS
B_HEADS = 4
B_LAMBDA_DECAY = 0.3
A_Q = A_HEADS * HEAD_DIM
A_KV = A_KV_HEADS * HEAD_DIM
B_QK = B_HEADS * 2 * HEAD_DIM
B_V = B_HEADS * 2 * HEAD_DIM
C_HEADS = 4
C_DK = D_MODEL // 2 // C_HEADS
C_DV = D_MODEL // C_HEADS
C_GATE_RANK = 16
C_GATE_NORM = 16.0
C_CHUNK = 64
C_QK = C_HEADS * C_DK
C_V = C_HEADS * C_DV
D_FF = 2816
LANES = 128
MOD_ROWS = 8
VMEM_LIMIT = 56 * 1024 * 1024


def _cparams(sem):
    return pltpu.CompilerParams(dimension_semantics=sem, vmem_limit_bytes=VMEM_LIMIT)


def _dot(a, b):
    return jnp.dot(a, b, preferred_element_type=F32)


def _dot_nt(a, b):
    return lax.dot_general(a, b, (((1,), (1,)), ((), ())), preferred_element_type=F32)


def _dot_tn(a, b):
    return lax.dot_general(a, b, (((0,), (0,)), ((), ())), preferred_element_type=F32)


def _sigmoid(x):
    return 1.0 / (1.0 + jnp.exp(-x))


def _norm_mod(x, g, sc, sh):
    ms = jnp.mean(x * x, axis=-1, keepdims=True)
    return (x * lax.rsqrt(ms + EPS) * g) * (1.0 + sc) + sh


def _mod_kernel(c_ref, w_ref, b_ref, o_ref):
    c = c_ref[...]
    s = (c * _sigmoid(c)).astype(BF16)
    o_ref[...] = _dot(s, w_ref[...].astype(BF16)) + b_ref[...]


def _modulation(c_rows, mod_w, mod_b):
    d = D_MODEL
    return pl.pallas_call(
        _mod_kernel,
        grid=(DEPTH, 6),
        in_specs=[
            pl.BlockSpec((MOD_ROWS, d), lambda l, n: (0, 0)),
            pl.BlockSpec((None, d, d), lambda l, n: (l, 0, n)),
            pl.BlockSpec((None, 1, d), lambda l, n: (l, 0, n)),
        ],
        out_specs=pl.BlockSpec((None, MOD_ROWS, d), lambda l, n: (l, 0, n)),
        out_shape=jax.ShapeDtypeStruct((DEPTH, MOD_ROWS, 6 * d), F32),
        compiler_params=_cparams(("parallel", "parallel")),
        name="modulation",
    )(c_rows, mod_w, mod_b.reshape(DEPTH, 1, 6 * d))


def _mod_spec(k, row_of):
    return pl.BlockSpec((None, None, 1, D_MODEL), lambda i, *_: (row_of(i), k, 0, 0))


LOG2E = math.log2(math.e)
_Q_SCALE = HEAD_DIM ** -0.5 * LOG2E
_ATTN_GROUPS = (("aq", A_Q, True, _Q_SCALE), ("bq", B_QK, True, _Q_SCALE), ("bk", B_QK, True, 1.0),
                ("ak", A_KV, True, 1.0), ("av", A_KV, False, 1.0), ("bv", B_V, False, 1.0))


def _rope_chunk(v, cos, sin, first):
    partner = jnp.where(first, pltpu.roll(v, 96, 1), pltpu.roll(v, 32, 1))
    return v * cos + partner * sin


def _attn_inproj_kernel(x_ref, sh_ref, sc_ref, g_ref, w_ref, cos_ref, sin_ref, *out_refs, rope):
    xn = _norm_mod(x_ref[...], g_ref[...], sc_ref[...], sh_ref[...]).astype(BF16)
    tm = xn.shape[0]
    if rope:
        cos = cos_ref[...]
        sin = sin_ref[...]
        lane = lax.broadcasted_iota(jnp.int32, (tm, LANES), 1)
        first = (lane % HEAD_DIM) < (HEAD_DIM // 2)
    col = 0
    for (name, width, roped, scale), o_ref in zip(_ATTN_GROUPS, out_refs):
        y = _dot(xn, w_ref[:, col:col + width])
        col += width
        for c in range(width // LANES):
            v = y[:, c * LANES:(c + 1) * LANES]
            if rope and roped:
                v = _rope_chunk(v, cos, sin, first)
            if scale != 1.0:
                v = v * scale
            o_ref[:, c * LANES:(c + 1) * LANES] = v.astype(BF16)


def _attn_inproj(x2d, mod, norm_g, w, cos_t, sin_t, *, tm, row_of, rope, tiles_per_seq):
    m, d = x2d.shape
    n_all = w.shape[1]
    out_shape = [jax.ShapeDtypeStruct((m, width), BF16) for (_, width, _, _) in _ATTN_GROUPS]
    out_specs = [pl.BlockSpec((tm, width), lambda i: (i, 0)) for (_, width, _, _) in _ATTN_GROUPS]
    return pl.pallas_call(
        functools.partial(_attn_inproj_kernel, rope=rope),
        grid=(m // tm,),
        in_specs=[
            pl.BlockSpec((tm, d), lambda i: (i, 0)),
            _mod_spec(0, row_of), _mod_spec(1, row_of),
            pl.BlockSpec((1, d), lambda i: (0, 0)),
            pl.BlockSpec((d, n_all), lambda i: (0, 0)),
            pl.BlockSpec((tm, LANES), lambda i: (i % tiles_per_seq, 0)),
            pl.BlockSpec((tm, LANES), lambda i: (i % tiles_per_seq, 0)),
        ],
        out_specs=out_specs,
        out_shape=out_shape,
        compiler_params=_cparams(("parallel",)),
        name="attn_inproj_rope" if rope else "attn_inproj_ctx",
    )(x2d, mod, mod, norm_g, w, cos_t, sin_t)


def _gqa_kernel(sink_ref, q_ref, *refs, window, nb):
    if window:
        kp_ref, kc_ref, kn_ref, vp_ref, vc_ref, vn_ref, kx_ref, vx_ref, o_ref = refs
        keys = jnp.concatenate([kp_ref[...], kc_ref[...], kn_ref[...], kx_ref[...]], axis=0)
        vals = jnp.concatenate([vp_ref[...], vc_ref[...], vn_ref[...], vx_ref[...]], axis=0)
    else:
        kx_ref, vx_ref, o_ref = refs
        keys = kx_ref[...]
        vals = vx_ref[...]
    tq = q_ref.shape[0]
    ns = keys.shape[0]
    nq = A_GROUP * tq
    if window:
        n = pl.program_id(1)
        s = lax.broadcasted_iota(jnp.int32, (BLOCK, tq), 0)
        t = lax.broadcasted_iota(jnp.int32, (BLOCK, tq), 1)
        has_prev = jnp.where(n > 0, 1, 0)
        has_next = jnp.where(n < nb - 1, 1, 0)
        lower = t * has_prev + BLOCK * (1 - has_prev)
        upper = (t + 1) * has_next - 1
        ninf = jnp.float32(-jnp.inf)
        bias_prev = jnp.concatenate([jnp.where(s < lower, ninf, 0.0)] * A_GROUP, axis=1)
        bias_next = jnp.concatenate([jnp.where(s > upper, ninf, 0.0)] * A_GROUP, axis=1)
    lane = lax.broadcasted_iota(jnp.int32, (1, LANES), 1)
    lo = lane < HEAD_DIM
    half = (jnp.where(lo, 1.0, 0.0).astype(F32), jnp.where(lo, 0.0, 1.0).astype(F32))
    qf = [q_ref[:, j * LANES:(j + 1) * LANES].astype(F32) for j in range(A_GROUP)]
    outs = []
    scs = [_dot_nt(keys, jnp.concatenate([(q * half[kv]).astype(BF16) for q in qf], axis=0))
           for kv in range(A_KV_HEADS)]
    for kv in range(A_KV_HEADS):
        sc = scs[kv]
        if window:
            sc = jnp.concatenate([sc[0:BLOCK] + bias_prev, sc[BLOCK:2 * BLOCK],
                                  sc[2 * BLOCK:3 * BLOCK] + bias_next, sc[3 * BLOCK:]], axis=0)
        sk = jnp.concatenate([jnp.full((1, tq), sink_ref[kv * A_GROUP + j] * LOG2E, F32)
                              for j in range(A_GROUP)], axis=1)
        mx = jnp.maximum(jnp.max(sc, axis=0, keepdims=True), sk)
        p = jnp.exp2(sc - mx)
        den = jnp.sum(p, axis=0, keepdims=True) + jnp.exp2(sk - mx)
        pn = (p * (1.0 / den)).astype(BF16)
        outs.append(_dot_tn(vals, pn))
    row = lax.broadcasted_iota(jnp.int32, (LANES, 1), 0)
    o_t = jnp.where(row < HEAD_DIM, outs[0], outs[1])
    for j in range(A_GROUP):
        o_ref[:, j * LANES:(j + 1) * LANES] = o_t[:, j * tq:(j + 1) * tq].T.astype(BF16)


def _gqa_window(sink, aq, ak, av, cak, cav):
    bsz, n_tok, _ = aq.shape
    n_ctx = cak.shape[1]
    nb = n_tok // BLOCK
    kv_prev = pl.BlockSpec((None, BLOCK, A_KV), lambda b, n: (b, jnp.maximum(n - 1, 0), 0))
    kv_cur = pl.BlockSpec((None, BLOCK, A_KV), lambda b, n: (b, n, 0))
    kv_next = pl.BlockSpec((None, BLOCK, A_KV), lambda b, n: (b, jnp.minimum(n + 1, nb - 1), 0))
    kv_ctx = pl.BlockSpec((None, n_ctx, A_KV), lambda b, n: (b, 0, 0))
    return pl.pallas_call(
        functools.partial(_gqa_kernel, window=True, nb=nb),
        grid=(bsz, nb),
        in_specs=[pl.BlockSpec(memory_space=pltpu.SMEM),
                  pl.BlockSpec((None, BLOCK, A_Q), lambda b, n: (b, n, 0)),
                  kv_prev, kv_cur, kv_next, kv_prev, kv_cur, kv_next, kv_ctx, kv_ctx],
        out_specs=pl.BlockSpec((None, BLOCK, A_Q), lambda b, n: (b, n, 0)),
        out_shape=jax.ShapeDtypeStruct((bsz, n_tok, A_Q), BF16),
        compiler_params=_cparams(("parallel", "parallel")),
        name="gqa_window",
    )(sink, aq, ak, ak, ak, av, av, av, cak, cav)


def _gqa_context(sink, caq, cak, cav):
    bsz, n_ctx, _ = caq.shape
    nb = n_ctx // BLOCK
    kv_ctx = pl.BlockSpec((None, n_ctx, A_KV), lambda b, n: (b, 0, 0))
    return pl.pallas_call(
        functools.partial(_gqa_kernel, window=False, nb=nb),
        grid=(bsz, nb),
        in_specs=[pl.BlockSpec(memory_space=pltpu.SMEM),
                  pl.BlockSpec((None, BLOCK, A_Q), lambda b, n: (b, n, 0)),
                  kv_ctx, kv_ctx],
        out_specs=pl.BlockSpec((None, BLOCK, A_Q), lambda b, n: (b, n, 0)),
        out_shape=jax.ShapeDtypeStruct((bsz, n_ctx, A_Q), BF16),
        compiler_params=_cparams(("parallel", "parallel")),
        name="gqa_context",
    )(sink, caq, cak, cav)


XPOSE_ROWS = 512
ONES_ROWS = 16


def _diff_attn_kernel(lam_ref, g_ref, q_ref, *refs, nseg, lam_init, kc):
    k_refs = refs[0:2 * nseg:2]
    v_refs = refs[1:2 * nseg:2]
    o_ref, vt_ref = refs[2 * nseg:]
    lv = lam_ref[...]
    lam = (jnp.exp(jnp.sum(lv[0:1] * lv[1:2], axis=-1, keepdims=True))
           - jnp.exp(jnp.sum(lv[2:3] * lv[3:4], axis=-1, keepdims=True)) + lam_init)
    q = q_ref[...].astype(F32)
    lane = lax.broadcasted_iota(jnp.int32, (1, LANES), 1)
    lo = lane < HEAD_DIM
    q0 = (q * jnp.where(lo, 1.0, 0.0).astype(F32)).astype(BF16)
    q1 = (q * jnp.where(lo, 0.0, 1.0).astype(F32)).astype(BF16)
    @pl.when(pl.program_id(2) == 0)
    def _():
        off = 0
        for v_ref in v_refs:
            ns = v_ref.shape[0]
            for c0 in range(0, ns, XPOSE_ROWS):
                n = min(XPOSE_ROWS, ns - c0)
                vt_ref[0:LANES, off + c0:off + c0 + n] = v_ref[c0:c0 + n, :].astype(F32).T.astype(BF16)
            off += ns
        vt_ref[LANES:, :] = jnp.ones((vt_ref.shape[0] - LANES, vt_ref.shape[1]), BF16)

    chunks = []
    off = 0
    for k_ref in k_refs:
        ns = k_ref.shape[0]
        step = min(kc, ns)
        chunks += [(k_ref, c0, step, off + c0) for c0 in range(0, ns, step)]
        off += ns

    qms = (q0, q1)

    def scores(mi, ci):
        k_ref, c0, step, _ = chunks[ci]
        return _dot_nt(k_ref[c0:c0 + step, :], qms[mi])

    run_max = [None, None]
    accs = [None, None]

    def probs(mi, sc):
        mc = jnp.max(sc, axis=0, keepdims=True)
        if run_max[mi] is None:
            run_max[mi], alpha = mc, None
        else:
            m_new = jnp.maximum(run_max[mi], mc)
            alpha = jnp.exp2(run_max[mi] - m_new)
            run_max[mi] = m_new
        return jnp.exp2((sc - run_max[mi]).astype(BF16)), alpha

    def accumulate(mi, ci, pb, alpha):
        _, _, step, g0 = chunks[ci]
        pv = _dot(vt_ref[:, g0:g0 + step], pb)
        accs[mi] = pv if alpha is None else accs[mi] * alpha + pv

    n_ch = len(chunks)
    sc_q = {ci: [scores(mi, ci) for mi in range(2)] for ci in range(min(2, n_ch))}
    pb_q = {0: [probs(mi, sc_q[0][mi]) for mi in range(2)]}
    for ci in range(n_ch):
        if ci + 2 < n_ch:
            sc_q[ci + 2] = [scores(mi, ci + 2) for mi in range(2)]
        if ci + 1 < n_ch:
            sc_pair = sc_q.pop(ci + 1)
            pb_q[ci + 1] = [probs(mi, sc_pair[mi]) for mi in range(2)]
        for mi, (pb, alpha) in enumerate(pb_q.pop(ci)):
            accumulate(mi, ci, pb, alpha)
    stats = [(run_max[mi], accs[mi]) for mi in range(2)]
    r0 = 1.0 / stats[0][1][LANES:LANES + 1, :]
    r1 = lam / stats[1][1][LANES:LANES + 1, :]
    o = (stats[0][1][0:LANES, :] * r0 - stats[1][1][0:LANES, :] * r1).T
    ms = jnp.mean(o * o, axis=-1, keepdims=True)
    o_ref[...] = ((o * lax.rsqrt(ms + EPS) * g_ref[...]) * (1.0 - lam_init)).astype(BF16)


def _diff_attn(lam_vec, subln_g, q, kv_list, *, tq, lam_init, kc=256):
    bsz, n_q, _ = q.shape
    in_specs = [pl.BlockSpec((4, HEAD_DIM), lambda b, h, i: (0, 0)),
                pl.BlockSpec((1, LANES), lambda b, h, i: (0, 0)),
                pl.BlockSpec((None, tq, LANES), lambda b, h, i: (b, i, h))]
    args = [lam_vec, subln_g, q]
    for k, v in kv_list:
        ns = k.shape[1]
        in_specs += [pl.BlockSpec((None, ns, LANES), lambda b, h, i: (b, 0, h))] * 2
        args += [k, v]
    return pl.pallas_call(
        functools.partial(_diff_attn_kernel, nseg=len(kv_list), lam_init=lam_init, kc=kc),
        grid=(bsz, B_HEADS, n_q // tq),
        in_specs=in_specs,
        out_specs=pl.BlockSpec((None, tq, LANES), lambda b, h, i: (b, i, h)),
        out_shape=jax.ShapeDtypeStruct((bsz, n_q, B_V), BF16),
        scratch_shapes=[pltpu.VMEM((LANES + ONES_ROWS, sum(k.shape[1] for k, _ in kv_list)), BF16)],
        compiler_params=_cparams(("parallel", "parallel", "arbitrary")),
        name="diff_attn_%dseg" % len(kv_list),
    )(*args)


def _outproj_kernel(*refs, n_in):
    a_refs = refs[:n_in]
    w_refs = refs[n_in:2 * n_in]
    h_ref, gt_ref, o_ref = refs[2 * n_in:]
    y = None
    for a, w in zip(a_refs, w_refs):
        c = _dot(a[...], w[...])
        y = c if y is None else y + c
    o_ref[...] = h_ref[...] + gt_ref[...] * y


def _outproj(acts, ws, h2d, mod, *, tm, row_of):
    m, d = h2d.shape
    n_in = len(acts)
    in_specs = [pl.BlockSpec((tm, a.shape[1]), lambda i: (i, 0)) for a in acts]
    in_specs += [pl.BlockSpec(w.shape, lambda i: (0, 0)) for w in ws]
    in_specs += [pl.BlockSpec((tm, d), lambda i: (i, 0)), _mod_spec(2, row_of)]
    return pl.pallas_call(
        functools.partial(_outproj_kernel, n_in=n_in),
        grid=(m // tm,),
        in_specs=in_specs,
        out_specs=pl.BlockSpec((tm, d), lambda i: (i, 0)),
        out_shape=jax.ShapeDtypeStruct((m, d), F32),
        compiler_params=_cparams(("parallel",)),
        name="outproj_residual",
    )(*acts, *ws, h2d, mod)


FFN_HALO = 16
FFN_TF = 256


def _ffn_kernel(h_ref, hp_ref, hn_ref, sh_ref, sc_ref, gt_ref, ng_ref, wu_ref, wg_ref, cwu_ref, cwg_ref,
                wd_ref, fg_ref, o_ref, xn_ref, hu_a, hg_a, hu_b, hg_b, acc_ref,
                *, tiles_per_seq, final_norm):
    i = pl.program_id(0)
    tm = h_ref.shape[0]
    hl = FFN_HALO
    nj = wu_ref.shape[0]

    pos = i % tiles_per_seq
    g, sc, sh = ng_ref[...], sc_ref[...], sh_ref[...]
    xn_ref[hl:hl + tm, :] = _norm_mod(h_ref[...], g, sc, sh).astype(BF16)
    prev = _norm_mod(hp_ref[...], g, sc, sh) * jnp.where(pos > 0, 1.0, 0.0)
    nxt = _norm_mod(hn_ref[...], g, sc, sh) * jnp.where(pos < tiles_per_seq - 1, 1.0, 0.0)
    xn_ref[0:hl, :] = prev.astype(BF16)
    xn_ref[hl + tm:2 * hl + tm, :] = nxt.astype(BF16)
    acc_ref[...] = jnp.zeros_like(acc_ref)

    def up(j, hu_ref, hg_ref):
        xn = xn_ref[...]
        hu_ref[...] = _dot(xn, wu_ref[j])
        hg_ref[...] = _dot(xn, wg_ref[j])

    def conv(ref, cw):
        return (ref[hl - 1:hl - 1 + tm, :] * cw[0:1] + ref[hl:hl + tm, :] * cw[1:2]
                + ref[hl + 1:hl + 1 + tm, :] * cw[2:3] + cw[3:4])

    def act(j, hu_ref, hg_ref):
        u = conv(hu_ref, cwu_ref[j])
        gt = conv(hg_ref, cwg_ref[j])
        return (gt * _sigmoid(gt) * u).astype(BF16)

    up(0, hu_a, hg_a)

    def pair(jj, carry):
        j = 2 * jj
        up(j + 1, hu_b, hg_b)
        acc_ref[...] += _dot(act(j, hu_a, hg_a), wd_ref[j])
        up(j + 2, hu_a, hg_a)
        acc_ref[...] += _dot(act(j + 1, hu_b, hg_b), wd_ref[j + 1])
        return carry

    assert nj % 2 == 1
    lax.fori_loop(0, (nj - 1) // 2, pair, 0)
    acc_ref[...] += _dot(act(nj - 1, hu_a, hg_a), wd_ref[nj - 1])

    y = h_ref[...] + gt_ref[...] * acc_ref[...]
    if final_norm:
        ms = jnp.mean(y * y, axis=-1, keepdims=True)
        y = y * lax.rsqrt(ms + EPS) * fg_ref[...]
    o_ref[...] = y


def _ffn_weights(w_up, conv_w, conv_b, w_down, tf):
    d = w_up.shape[0]
    nj = D_FF // tf
    chunked = lambda w: w.reshape(w.shape[0], nj, tf).transpose(1, 0, 2)
    w_uu = chunked(w_up[:, :D_FF]).astype(BF16)
    w_ug = chunked(w_up[:, D_FF:]).astype(BF16)
    cw = jnp.concatenate([conv_w, conv_b[None, :]], axis=0)
    return (w_uu, w_ug, chunked(cw[:, :D_FF]), chunked(cw[:, D_FF:]), w_down.reshape(nj, tf, d).astype(BF16))


def _conv_ffn(h2d, mod, norm_g, weights, final_g, *, tm, row_of, tiles_per_seq, final_norm):
    m, d = h2d.shape
    hl = FFN_HALO
    nhb = m // hl
    r = tm // hl
    w_uu, w_ug, cwu, cwg, w_dn = weights
    tf = w_uu.shape[2]
    resident = lambda a: pl.BlockSpec(a.shape, lambda i: (0,) * a.ndim, pipeline_mode=pl.Buffered(1))
    return pl.pallas_call(
        functools.partial(_ffn_kernel, tiles_per_seq=tiles_per_seq, final_norm=final_norm),
        grid=(m // tm,),
        in_specs=[
            pl.BlockSpec((tm, d), lambda i: (i, 0)),
            pl.BlockSpec((hl, d), lambda i: (jnp.maximum(i * r - 1, 0), 0)),
            pl.BlockSpec((hl, d), lambda i: (jnp.minimum((i + 1) * r, nhb - 1), 0)),
            _mod_spec(3, row_of), _mod_spec(4, row_of), _mod_spec(5, row_of),
            pl.BlockSpec((1, d), lambda i: (0, 0)),
            resident(w_uu), resident(w_ug), resident(cwu), resident(cwg), resident(w_dn),
            pl.BlockSpec((1, d), lambda i: (0, 0)),
        ],
        out_specs=pl.BlockSpec((tm, d), lambda i: (i, 0)),
        out_shape=jax.ShapeDtypeStruct((m, d), F32),
        scratch_shapes=[pltpu.VMEM((tm + 2 * hl, d), BF16)]
        + [pltpu.VMEM((tm + 2 * hl, tf), F32)] * 4
        + [pltpu.VMEM((tm, d), F32)],
        compiler_params=_cparams(("parallel",)),
        name="conv_ffn",
    )(h2d, h2d, h2d, mod, mod, mod, norm_g, w_uu, w_ug, cwu, cwg, w_dn, final_g)


def _gla_inproj_kernel(x_ref, sh_ref, sc_ref, g_ref, w_ref, w2_ref, gb_ref,
                       q_ref, k_ref, v_ref, sg_ref, laf_ref, lab_ref):
    xn = _norm_mod(x_ref[...], g_ref[...], sc_ref[...], sh_ref[...]).astype(BF16)
    q_ref[...] = (_dot(xn, w_ref[:, 0:C_QK]) * (C_DK ** -0.5)).astype(BF16)
    k_ref[...] = _dot(xn, w_ref[:, C_QK:2 * C_QK]).astype(BF16)
    for c in range(C_V // C_QK):
        lo = 2 * C_QK + c * C_QK
        v_ref[:, c * C_QK:(c + 1) * C_QK] = _dot(xn, w_ref[:, lo:lo + C_QK]).astype(BF16)
    for c in range(C_V // C_QK):
        lo = 2 * C_QK + C_V + c * C_QK
        gg = _dot(xn, w_ref[:, lo:lo + C_QK])
        sg_ref[:, c * C_QK:(c + 1) * C_QK] = (gg * _sigmoid(gg)).astype(BF16)
    lo = 2 * C_QK + 2 * C_V
    r = _dot(xn, w_ref[:, lo:lo + LANES]).astype(BF16)
    for dr, la_ref in enumerate((laf_ref, lab_ref)):
        z = _dot(r, w2_ref[:, dr * C_QK:(dr + 1) * C_QK]) + gb_ref[:, dr * C_QK:(dr + 1) * C_QK]
        la_ref[...] = (jnp.minimum(z, 0.0) - jnp.log(1.0 + jnp.exp(-jnp.abs(z)))) * (1.0 / C_GATE_NORM)


def _gla_inproj(x2d, mod, norm_g, w, w2, gb, *, tm, row_of):
    m, d = x2d.shape
    widths = (C_QK, C_QK, C_V, C_V, C_QK, C_QK)
    dts = (BF16, BF16, BF16, BF16, F32, F32)
    return pl.pallas_call(
        _gla_inproj_kernel,
        grid=(m // tm,),
        in_specs=[
            pl.BlockSpec((tm, d), lambda i: (i, 0)),
            _mod_spec(0, row_of), _mod_spec(1, row_of),
            pl.BlockSpec((1, d), lambda i: (0, 0)),
            pl.BlockSpec(w.shape, lambda i: (0, 0)),
            pl.BlockSpec(w2.shape, lambda i: (0, 0)),
            pl.BlockSpec(gb.shape, lambda i: (0, 0)),
        ],
        out_specs=[pl.BlockSpec((tm, wd), lambda i: (i, 0)) for wd in widths],
        out_shape=[jax.ShapeDtypeStruct((m, wd), dt) for wd, dt in zip(widths, dts)],
        compiler_params=_cparams(("parallel",)),
        name="gla_inproj",
    )(x2d, mod, mod, norm_g, w, w2, gb)


def _tri(n, reverse):
    r = lax.broadcasted_iota(jnp.int32, (n, n), 0)
    c = lax.broadcasted_iota(jnp.int32, (n, n), 1)
    return (c >= r) if reverse else (c <= r)


def _cumsum_rows(la, tri_bf):
    hi = la.astype(BF16)
    r1 = la - hi.astype(F32)
    mid = r1.astype(BF16)
    lo = (r1 - mid.astype(F32)).astype(BF16)
    return _dot(tri_bf, hi) + _dot(tri_bf, mid) + _dot(tri_bf, lo)


def _gla_ctx_state_kernel(k_ref, v_ref, laf_ref, lab_ref, sf_ref, sb_ref):
    n = k_ref.shape[0]
    for reverse, la_ref, s_ref in ((False, laf_ref, sf_ref), (True, lab_ref, sb_ref)):
        tri = jnp.where(_tri(n, reverse), 1.0, 0.0).astype(BF16)
        b = _cumsum_rows(la_ref[...], tri)
        b_end = b[0:1, :] if reverse else b[n - 1:n, :]
        kw = (k_ref[...].astype(F32) * jnp.exp(b_end - b)).astype(BF16)
        for h in range(C_HEADS):
            s_ref[h] = _dot_tn(v_ref[:, h * C_DV:(h + 1) * C_DV], kw[:, h * C_DK:(h + 1) * C_DK])


def _gla_ctx_state(kc, vc, lac_f, lac_b):
    bsz, n, _ = kc.shape
    s_shape = jax.ShapeDtypeStruct((bsz, C_HEADS, C_DV, C_DK), F32)
    s_spec = pl.BlockSpec((None, C_HEADS, C_DV, C_DK), lambda b: (b, 0, 0, 0))
    return pl.pallas_call(
        _gla_ctx_state_kernel,
        grid=(bsz,),
        in_specs=[pl.BlockSpec((None, n, C_QK), lambda b: (b, 0, 0)),
                  pl.BlockSpec((None, n, C_V), lambda b: (b, 0, 0)),
                  pl.BlockSpec((None, n, C_QK), lambda b: (b, 0, 0)),
                  pl.BlockSpec((None, n, C_QK), lambda b: (b, 0, 0))],
        out_specs=[s_spec, s_spec],
        out_shape=[s_shape, s_shape],
        compiler_params=_cparams(("parallel",)),
        name="gla_ctx_state",
    )(kc, vc, lac_f, lac_b)


def _gla_scan_kernel(q_ref, k_ref, v_ref, la_ref, s0_ref, *refs, reverse, final):
    if final:
        ob_ref, sg_ref, ng_ref, o_ref, st_ref = refs
    else:
        o_ref, st_ref = refs
    gidx = pl.program_id(1)

    @pl.when(gidx == 0)
    def _():
        st_ref[...] = s0_ref[...]

    gt = q_ref.shape[0]
    nchunk = gt // C_CHUNK
    c = C_CHUNK
    r = lax.broadcasted_iota(jnp.int32, (gt, gt), 0)
    s = lax.broadcasted_iota(jnp.int32, (gt, gt), 1)
    shift = c.bit_length() - 1
    same_chunk = (r >> shift) == (s >> shift)
    tri = same_chunk & ((s >= r) if reverse else (s <= r))
    tri_bf = jnp.where(tri, 1.0, 0.0).astype(BF16)

    def per_chunk_row(x, row):
        return jnp.concatenate([jnp.broadcast_to(x[ci * c + row:ci * c + row + 1, :], (c, x.shape[1]))
                                for ci in range(nchunk)], axis=0)

    b = _cumsum_rows(la_ref[...], tri_bf)
    b_mid = per_chunk_row(b, c // 2)
    b_end = per_chunk_row(b, 0 if reverse else c - 1)
    qf = q_ref[...].astype(F32)
    kf = k_ref[...].astype(F32)
    q_in = (qf * jnp.exp(b - b_mid)).astype(BF16)
    k_in = (kf * jnp.exp(b_mid - b)).astype(BF16)
    q_out = (qf * jnp.exp(b)).astype(BF16)
    k_out = (kf * jnp.exp(b_end - b)).astype(BF16)
    decay = jnp.exp(b_end)

    order = range(nchunk - 1, -1, -1) if reverse else range(nchunk)
    for h in range(C_HEADS):
        kcols = slice(h * C_DK, (h + 1) * C_DK)
        vcols = slice(h * C_DV, (h + 1) * C_DV)
        vv = v_ref[:, vcols]
        sc = jnp.where(tri, _dot_nt(q_in[:, kcols], k_in[:, kcols]), 0.0).astype(BF16)
        o_intra = _dot(sc, vv)
        upd = {ci: _dot_tn(vv[ci * c:(ci + 1) * c, :], k_out[ci * c:(ci + 1) * c, kcols]) for ci in order}
        st = st_ref[h]
        o_inter = {}
        for ci in order:
            rows = slice(ci * c, (ci + 1) * c)
            o_inter[ci] = _dot_nt(q_out[rows, kcols], st.astype(BF16))
            st = st * decay[ci * c:ci * c + 1, kcols] + upd[ci]
        st_ref[h] = st
        o = o_intra + jnp.concatenate([o_inter[ci] for ci in range(nchunk)], axis=0)
        if final:
            o = o + ob_ref[:, vcols]
            ms = jnp.mean(o * o, axis=-1, keepdims=True)
            o = (o * lax.rsqrt(ms + EPS) * ng_ref[...]) * sg_ref[:, vcols].astype(F32)
            o_ref[:, vcols] = o.astype(BF16)
        else:
            o_ref[:, vcols] = o


def _gla_scan(q, k, v, la, s0, *, gt, reverse, o_other=None, sg=None, norm_g=None):
    bsz, n_tok, _ = q.shape
    ng = n_tok // gt
    final = o_other is not None
    gi = (lambda b, g: (b, ng - 1 - g, 0)) if reverse else (lambda b, g: (b, g, 0))
    in_specs = [pl.BlockSpec((None, gt, C_QK), gi), pl.BlockSpec((None, gt, C_QK), gi),
                pl.BlockSpec((None, gt, C_V), gi), pl.BlockSpec((None, gt, C_QK), gi),
                pl.BlockSpec((None, C_HEADS, C_DV, C_DK), lambda b, g: (b, 0, 0, 0))]
    args = [q, k, v, la, s0]
    if final:
        in_specs += [pl.BlockSpec((None, gt, C_V), gi), pl.BlockSpec((None, gt, C_V), gi),
                     pl.BlockSpec((1, C_DV), lambda b, g: (0, 0))]
        args += [o_other, sg, norm_g]
    return pl.pallas_call(
        functools.partial(_gla_scan_kernel, reverse=reverse, final=final),
        grid=(bsz, ng),
        in_specs=in_specs,
        out_specs=pl.BlockSpec((None, gt, C_V), gi),
        out_shape=jax.ShapeDtypeStruct((bsz, n_tok, C_V), BF16 if final else F32),
        scratch_shapes=[pltpu.VMEM((C_HEADS, C_DV, C_DK), F32)],
        compiler_params=_cparams(("parallel", "arbitrary")),
        name="gla_scan_fwd_final" if final else "gla_scan_bwd",
    )(*args)


def _pair_split(n_heads):
    base = np.concatenate([np.arange(0, HEAD_DIM, 2), np.arange(1, HEAD_DIM, 2)])
    return np.concatenate([h * HEAD_DIM + base for h in range(n_heads)])


_A_HEAD_ORDER = np.array([kv * A_GROUP + j for j in range(A_GROUP) for kv in range(A_KV_HEADS)])


def _attn_in_cols():
    aq = (_A_HEAD_ORDER[:, None] * HEAD_DIM + _pair_split(1)[None, :]).reshape(-1)
    o_ak, o_av, o_bq = A_Q, A_Q + A_KV, A_Q + 2 * A_KV
    o_bk, o_bv = o_bq + B_QK, o_bq + 2 * B_QK
    return np.concatenate([aq, o_bq + _pair_split(2 * B_HEADS), o_bk + _pair_split(2 * B_HEADS),
                           o_ak + _pair_split(A_KV_HEADS), o_av + np.arange(A_KV), o_bv + np.arange(B_V)])


def _attn_out_rows():
    oa = (_A_HEAD_ORDER[:, None] * HEAD_DIM + np.arange(HEAD_DIM)[None, :]).reshape(-1)
    return oa


def _rope_tables(n_tok):
    rows = n_tok // GRID_W
    row = jnp.repeat(jnp.arange(rows, dtype=F32), GRID_W)
    col = jnp.tile(jnp.arange(GRID_W, dtype=F32), rows)
    axis_dim = HEAD_DIM // 2
    inv_freq = ROPE_THETA ** (-jnp.arange(0, axis_dim, 2, dtype=F32) / axis_dim)
    ang = jnp.concatenate([row[:, None] * inv_freq, col[:, None] * inv_freq], axis=-1)
    cos, sin = jnp.cos(ang), jnp.sin(ang)
    cos_t = jnp.tile(cos, (1, LANES // (HEAD_DIM // 2)))
    sin_t = jnp.tile(jnp.concatenate([-sin, sin], axis=-1), (1, LANES // HEAD_DIM))
    return cos_t, sin_t


def _pick(n, pref):
    return pref if n % pref == 0 else n


def kernel(x, c, ctx, c_ctx, mod_w, mod_b, norm1_g, norm2_g, attn_w_in, attn_w_out, attn_sink, diff_lambda, diff_subln_g, gla_w_in, gla_gate_w1, gla_gate_w2, gla_gate_b, gla_norm_g, gla_w_out, ffn_w_up, ffn_conv_w, ffn_conv_b, ffn_w_down, final_norm_g):
    bsz, n_tok, d = x.shape
    n_ctx = ctx.shape[1]
    assert d == D_MODEL and bsz + 1 <= MOD_ROWS
    m_lat, m_ctx = bsz * n_tok, bsz * n_ctx

    c_rows = jnp.concatenate([c, c_ctx[None, :], jnp.zeros((MOD_ROWS - bsz - 1, d), F32)], axis=0)
    mod_all = _modulation(c_rows, mod_w, mod_b)
    cos_t, sin_t = _rope_tables(n_tok)

    tm = _pick(n_tok, 512)
    tmc = _pick(n_ctx, 256)
    lat_tiles = n_tok // tm
    lat_row = lambda i: i // lat_tiles
    ctx_row = lambda i: bsz
    tm_ffn = _pick(n_tok, 1024)
    ffn_row = lambda i: i // (n_tok // tm_ffn)

    h = x.reshape(m_lat, d)
    hc = ctx.reshape(m_ctx, d)
    for layer in range(DEPTH):
        need_ctx = layer < DEPTH - 1
        last = layer == DEPTH - 1
        mod = mod_all[layer].reshape(MOD_ROWS, 6, 1, d)
        n1 = norm1_g[layer].reshape(1, d)
        n2 = norm2_g[layer].reshape(1, d)
        i = layer // 2
        if layer % 2 == 0:
            lam_init = 0.8 - 0.6 * math.exp(-B_LAMBDA_DECAY * layer)
            w_in = attn_w_in[i][:, _attn_in_cols()].astype(BF16)
            w_out = attn_w_out[i]
            w_oa = w_out[_attn_out_rows()].astype(BF16)
            w_ob = w_out[A_Q:].astype(BF16)
            sink = attn_sink[i]
            subln = diff_subln_g[i].reshape(1, LANES)
            aq, bq, bk, ak, av, bv = _attn_inproj(h, mod, n1, w_in, cos_t, sin_t, tm=tm, row_of=lat_row,
                                                  rope=True, tiles_per_seq=lat_tiles)
            caq, cbq, cbk, cak, cav, cbv = _attn_inproj(hc, mod, n1, w_in, cos_t, sin_t, tm=tmc, row_of=ctx_row,
                                                        rope=False, tiles_per_seq=1)
            r3 = lambda a, n: a.reshape(bsz, n, a.shape[-1])
            cak3, cav3, cbk3, cbv3 = r3(cak, n_ctx), r3(cav, n_ctx), r3(cbk, n_ctx), r3(cbv, n_ctx)
            oa = _gqa_window(sink, r3(aq, n_tok), r3(ak, n_tok), r3(av, n_tok), cak3, cav3)
            ob = _diff_attn(diff_lambda[i], subln, r3(bq, n_tok), [(r3(bk, n_tok), r3(bv, n_tok)), (cbk3, cbv3)],
                            tq=_pick(n_tok, 512), lam_init=lam_init)
            h = _outproj([oa.reshape(m_lat, A_Q), ob.reshape(m_lat, B_V)], [w_oa, w_ob], h, mod,
                         tm=tm, row_of=lat_row)
            if need_ctx:
                oca = _gqa_context(sink, r3(caq, n_ctx), cak3, cav3)
                ocb = _diff_attn(diff_lambda[i], subln, r3(cbq, n_ctx), [(cbk3, cbv3)],
                                 tq=_pick(n_ctx, 256), lam_init=lam_init)
                hc = _outproj([oca.reshape(m_ctx, A_Q), ocb.reshape(m_ctx, B_V)], [w_oa, w_ob], hc, mod,
                              tm=tmc, row_of=ctx_row)
        else:
            w1 = gla_gate_w1[i]
            pad = jnp.zeros((d, LANES - 2 * C_GATE_RANK), F32)
            w_in = jnp.concatenate([gla_w_in[i], w1[0], w1[1], pad], axis=1).astype(BF16)
            w2 = gla_gate_w2[i]
            w2bd = jnp.zeros((LANES, 2 * C_QK), F32)
            w2bd = w2bd.at[0:C_GATE_RANK, 0:C_QK].set(w2[0]).at[C_GATE_RANK:2 * C_GATE_RANK, C_QK:].set(w2[1])
            w2bd = w2bd.astype(BF16)
            gb = gla_gate_b[i].reshape(1, 2 * C_QK)
            ng = gla_norm_g[i].reshape(1, C_DV)
            q, k, v, sg, la_f, la_b = _gla_inproj(h, mod, n1, w_in, w2bd, gb, tm=tm, row_of=lat_row)
            qc, kc, vc, sgc, lac_f, lac_b = _gla_inproj(hc, mod, n1, w_in, w2bd, gb, tm=tmc, row_of=ctx_row)
            r3 = lambda a, n: a.reshape(bsz, n, a.shape[-1])
            s_f, s_b = _gla_ctx_state(r3(kc, n_ctx), r3(vc, n_ctx), r3(lac_f, n_ctx), r3(lac_b, n_ctx))
            gt = _pick(n_tok, 256)
            q3, k3, v3 = r3(q, n_tok), r3(k, n_tok), r3(v, n_tok)
            o_b = _gla_scan(q3, k3, v3, r3(la_b, n_tok), s_b, gt=gt, reverse=True)
            og = _gla_scan(q3, k3, v3, r3(la_f, n_tok), s_f, gt=gt, reverse=False,
                           o_other=o_b, sg=r3(sg, n_tok), norm_g=ng)
            h = _outproj([og.reshape(m_lat, C_V)], [gla_w_out[i].astype(BF16)], h, mod, tm=tm, row_of=lat_row)
            if need_ctx:
                z = jnp.zeros((bsz, C_HEADS, C_DV, C_DK), F32)
                qc3, kc3, vc3 = r3(qc, n_ctx), r3(kc, n_ctx), r3(vc, n_ctx)
                gtc = _pick(n_ctx, 256)
                oc_b = _gla_scan(qc3, kc3, vc3, r3(lac_b, n_ctx), z, gt=gtc, reverse=True)
                ogc = _gla_scan(qc3, kc3, vc3, r3(lac_f, n_ctx), z, gt=gtc, reverse=False,
                                o_other=oc_b, sg=r3(sgc, n_ctx), norm_g=ng)
                hc = _outproj([ogc.reshape(m_ctx, C_V)], [gla_w_out[i].astype(BF16)], hc, mod,
                              tm=tmc, row_of=ctx_row)
        ffn_w = _ffn_weights(ffn_w_up[layer], ffn_conv_w[layer], ffn_conv_b[layer], ffn_w_down[layer], FFN_TF)
        fg = final_norm_g.reshape(1, d)
        h = _conv_ffn(h, mod, n2, ffn_w, fg, tm=tm_ffn, row_of=ffn_row, tiles_per_seq=n_tok // tm_ffn,
                      final_norm=last)
        if need_ctx:
            hc = _conv_ffn(hc, mod, n2, ffn_w, fg, tm=tmc, row_of=ctx_row, tiles_per_seq=n_ctx // tmc,
                           final_norm=False)
    return h.reshape(bsz, n_tok, d)
```

```python
import functools
import math

import numpy as np
import jax
import jax.numpy as jnp
from jax import lax
from jax.experimental import pallas as pl
from jax.experimental.pallas import tpu as pltpu

F32 = jnp.float32
BF16 = jnp.bfloat16

D_MODEL = 1024
DEPTH = 2
GRID_W = 64
HEAD_DIM = 64
ROPE_THETA = 10000.0
EPS = 1e-6
BLOCK = 128
A_HEADS = 8
A_KV_HEADS = 2
A_GROUP = A_HEADS // A_KV_HEADS
B_HEADS = 4
B_LAMBDA_DECAY = 0.3
A_Q = A_HEADS * HEAD_DIM
A_KV = A_KV_HEADS * HEAD_DIM
B_QK = B_HEADS * 2 * HEAD_DIM
B_V = B_HEADS * 2 * HEAD_DIM
C_HEADS = 4
C_DK = D_MODEL // 2 // C_HEADS
C_DV = D_MODEL // C_HEADS
C_GATE_RANK = 16
C_GATE_NORM = 16.0
C_CHUNK = 64
C_QK = C_HEADS * C_DK
C_V = C_HEADS * C_DV
D_FF = 2816
LANES = 128
MOD_ROWS = 8
VMEM_LIMIT = 56 * 1024 * 1024


def _cparams(sem):
    return pltpu.CompilerParams(dimension_semantics=sem, vmem_limit_bytes=VMEM_LIMIT)


def _dot(a, b):
    return jnp.dot(a, b, preferred_element_type=F32)


def _dot_nt(a, b):
    return lax.dot_general(a, b, (((1,), (1,)), ((), ())), preferred_element_type=F32)


def _dot_tn(a, b):
    return lax.dot_general(a, b, (((0,), (0,)), ((), ())), preferred_element_type=F32)


def _sigmoid(x):
    return 1.0 / (1.0 + jnp.exp(-x))


def _norm_mod(x, g, sc, sh):
    ms = jnp.mean(x * x, axis=-1, keepdims=True)
    return (x * lax.rsqrt(ms + EPS) * g) * (1.0 + sc) + sh


def _mod_kernel(c_ref, w_ref, b_ref, o_ref):
    c = c_ref[...]
    s = (c * _sigmoid(c)).astype(BF16)
    o_ref[...] = _dot(s, w_ref[...].astype(BF16)) + b_ref[...]


def _modulation(c_rows, mod_w, mod_b):
    d = D_MODEL
    return pl.pallas_call(
        _mod_kernel,
        grid=(DEPTH, 6),
        in_specs=[
            pl.BlockSpec((MOD_ROWS, d), lambda l, n: (0, 0)),
            pl.BlockSpec((None, d, d), lambda l, n: (l, 0, n)),
            pl.BlockSpec((None, 1, d), lambda l, n: (l, 0, n)),
        ],
        out_specs=pl.BlockSpec((None, MOD_ROWS, d), lambda l, n: (l, 0, n)),
        out_shape=jax.ShapeDtypeStruct((DEPTH, MOD_ROWS, 6 * d), F32),
        compiler_params=_cparams(("parallel", "parallel")),
        name="modulation",
    )(c_rows, mod_w, mod_b.reshape(DEPTH, 1, 6 * d))


def _mod_spec(k, row_of):
    return pl.BlockSpec((None, None, 1, D_MODEL), lambda i, *_: (row_of(i), k, 0, 0))


LOG2E = math.log2(math.e)
_Q_SCALE = HEAD_DIM ** -0.5 * LOG2E
_ATTN_GROUPS = (("aq", A_Q, True, _Q_SCALE), ("bq", B_QK, True, _Q_SCALE), ("bk", B_QK, True, 1.0),
                ("ak", A_KV, True, 1.0), ("av", A_KV, False, 1.0), ("bv", B_V, False, 1.0))


def _rope_chunk(v, cos, sin, first):
    partner = jnp.where(first, pltpu.roll(v, 96, 1), pltpu.roll(v, 32, 1))
    return v * cos + partner * sin


def _attn_inproj_kernel(x_ref, sh_ref, sc_ref, g_ref, w_ref, cos_ref, sin_ref, *out_refs, rope):
    xn = _norm_mod(x_ref[...], g_ref[...], sc_ref[...], sh_ref[...]).astype(BF16)
    tm = xn.shape[0]
    if rope:
        cos = cos_ref[...]
        sin = sin_ref[...]
        lane = lax.broadcasted_iota(jnp.int32, (tm, LANES), 1)
        first = (lane % HEAD_DIM) < (HEAD_DIM // 2)
    col = 0
    for (name, width, roped, scale), o_ref in zip(_ATTN_GROUPS, out_refs):
        y = _dot(xn, w_ref[:, col:col + width])
        col += width
        for c in range(width // LANES):
            v = y[:, c * LANES:(c + 1) * LANES]
            if rope and roped:
                v = _rope_chunk(v, cos, sin, first)
            if scale != 1.0:
                v = v * scale
            o_ref[:, c * LANES:(c + 1) * LANES] = v.astype(BF16)


def _attn_inproj(x2d, mod, norm_g, w, cos_t, sin_t, *, tm, row_of, rope, tiles_per_seq):
    m, d = x2d.shape
    n_all = w.shape[1]
    out_shape = [jax.ShapeDtypeStruct((m, width), BF16) for (_, width, _, _) in _ATTN_GROUPS]
    out_specs = [pl.BlockSpec((tm, width), lambda i: (i, 0)) for (_, width, _, _) in _ATTN_GROUPS]
    return pl.pallas_call(
        functools.partial(_attn_inproj_kernel, rope=rope),
        grid=(m // tm,),
        in_specs=[
            pl.BlockSpec((tm, d), lambda i: (i, 0)),
            _mod_spec(0, row_of), _mod_spec(1, row_of),
            pl.BlockSpec((1, d), lambda i: (0, 0)),
            pl.BlockSpec((d, n_all), lambda i: (0, 0)),
            pl.BlockSpec((tm, LANES), lambda i: (i % tiles_per_seq, 0)),
            pl.BlockSpec((tm, LANES), lambda i: (i % tiles_per_seq, 0)),
        ],
        out_specs=out_specs,
        out_shape=out_shape,
        compiler_params=_cparams(("parallel",)),
        name="attn_inproj_rope" if rope else "attn_inproj_ctx",
    )(x2d, mod, mod, norm_g, w, cos_t, sin_t)


GQA_QB = 2


def _gqa_kernel(sink_ref, q_ref, *refs, window, nb):
    tq = BLOCK
    qb = q_ref.shape[0] // tq
    if window:
        k_blk = [r[...] for r in refs[0:qb + 2]]
        v_blk = [r[...] for r in refs[qb + 2:2 * qb + 4]]
        kx_ref, vx_ref, o_ref = refs[2 * qb + 4:]
    else:
        kx_ref, vx_ref, o_ref = refs
    lane = lax.broadcasted_iota(jnp.int32, (1, LANES), 1)
    lo = lane < HEAD_DIM
    half = (jnp.where(lo, 1.0, 0.0).astype(F32), jnp.where(lo, 0.0, 1.0).astype(F32))
    row = lax.broadcasted_iota(jnp.int32, (LANES, 1), 0)
    sinks = [jnp.concatenate([jnp.full((1, tq), sink_ref[kv * A_GROUP + j] * LOG2E, F32)
                              for j in range(A_GROUP)], axis=1) for kv in range(A_KV_HEADS)]
    work = []
    for u in range(qb):
        if window:
            keys = jnp.concatenate(k_blk[u:u + 3] + [kx_ref[...]], axis=0)
            vals = jnp.concatenate(v_blk[u:u + 3] + [vx_ref[...]], axis=0)
        else:
            keys, vals = kx_ref[...], vx_ref[...]
        qf = [q_ref[u * tq:(u + 1) * tq, j * LANES:(j + 1) * LANES].astype(F32) for j in range(A_GROUP)]
        scs = [_dot_nt(keys, jnp.concatenate([(q * half[kv]).astype(BF16) for q in qf], axis=0))
               for kv in range(A_KV_HEADS)]
        work.append((vals, scs))
    for u, (vals, scs) in enumerate(work):
        if window:
            n = pl.program_id(1) * qb + u
            s = lax.broadcasted_iota(jnp.int32, (BLOCK, tq), 0)
            t = lax.broadcasted_iota(jnp.int32, (BLOCK, tq), 1)
            has_prev = jnp.where(n > 0, 1, 0)
            has_next = jnp.where(n < nb - 1, 1, 0)
            lower = t * has_prev + BLOCK * (1 - has_prev)
            upper = (t + 1) * has_next - 1
            ninf = jnp.float32(-jnp.inf)
            bias_prev = jnp.concatenate([jnp.where(s < lower, ninf, 0.0)] * A_GROUP, axis=1)
            bias_next = jnp.concatenate([jnp.where(s > upper, ninf, 0.0)] * A_GROUP, axis=1)
        outs = []
        for kv in range(A_KV_HEADS):
            sc = scs[kv]
            if window:
                sc = jnp.concatenate([sc[0:BLOCK] + bias_prev, sc[BLOCK:2 * BLOCK],
                                      sc[2 * BLOCK:3 * BLOCK] + bias_next, sc[3 * BLOCK:]], axis=0)
            sk = sinks[kv]
            mx = jnp.maximum(jnp.max(sc, axis=0, keepdims=True), sk)
            p = jnp.exp2(sc - mx)
            den = jnp.sum(p, axis=0, keepdims=True) + jnp.exp2(sk - mx)
            pn = (p * (1.0 / den)).astype(BF16)
            outs.append(_dot_tn(vals, pn))
        o_t = jnp.where(row < HEAD_DIM, outs[0], outs[1])
        for j in range(A_GROUP):
            o_ref[u * tq:(u + 1) * tq, j * LANES:(j + 1) * LANES] = o_t[:, j * tq:(j + 1) * tq].T.astype(BF16)


def _gqa_window(sink, aq, ak, av, cak, cav):
    bsz, n_tok, _ = aq.shape
    n_ctx = cak.shape[1]
    nb = n_tok // BLOCK
    qb = GQA_QB
    assert nb % qb == 0
    kv_blocks = [pl.BlockSpec((None, BLOCK, A_KV),
                              lambda b, n, off=off: (b, jnp.clip(n * qb + off, 0, nb - 1), 0))
                 for off in range(-1, qb + 1)]
    kv_ctx = pl.BlockSpec((None, n_ctx, A_KV), lambda b, n: (b, 0, 0))
    return pl.pallas_call(
        functools.partial(_gqa_kernel, window=True, nb=nb),
        grid=(bsz, nb // qb),
        in_specs=[pl.BlockSpec(memory_space=pltpu.SMEM),
                  pl.BlockSpec((None, qb * BLOCK, A_Q), lambda b, n: (b, n, 0))]
        + kv_blocks + kv_blocks + [kv_ctx, kv_ctx],
        out_specs=pl.BlockSpec((None, qb * BLOCK, A_Q), lambda b, n: (b, n, 0)),
        out_shape=jax.ShapeDtypeStruct((bsz, n_tok, A_Q), BF16),
        compiler_params=_cparams(("parallel", "parallel")),
        name="gqa_window",
    )(sink, aq, *([ak] * (qb + 2)), *([av] * (qb + 2)), cak, cav)


def _gqa_context(sink, caq, cak, cav):
    bsz, n_ctx, _ = caq.shape
    nb = n_ctx // BLOCK
    kv_ctx = pl.BlockSpec((None, n_ctx, A_KV), lambda b, n: (b, 0, 0))
    return pl.pallas_call(
        functools.partial(_gqa_kernel, window=False, nb=nb),
        grid=(bsz, nb),
        in_specs=[pl.BlockSpec(memory_space=pltpu.SMEM),
                  pl.BlockSpec((None, BLOCK, A_Q), lambda b, n: (b, n, 0)),
                  kv_ctx, kv_ctx],
        out_specs=pl.BlockSpec((None, BLOCK, A_Q), lambda b, n: (b, n, 0)),
        out_shape=jax.ShapeDtypeStruct((bsz, n_ctx, A_Q), BF16),
        compiler_params=_cparams(("parallel", "parallel")),
        name="gqa_context",
    )(sink, caq, cak, cav)


XPOSE_ROWS = 512
ONES_ROWS = 16


def _diff_attn_kernel(lam_ref, g_ref, q_ref, *refs, nseg, lam_init, kc):
    k_refs = refs[0:2 * nseg:2]
    v_refs = refs[1:2 * nseg:2]
    o_ref, vt_ref = refs[2 * nseg:]
    lv = lam_ref[...]
    lam = (jnp.exp(jnp.sum(lv[0:1] * lv[1:2], axis=-1, keepdims=True))
           - jnp.exp(jnp.sum(lv[2:3] * lv[3:4], axis=-1, keepdims=True)) + lam_init)
    q = q_ref[...].astype(F32)
    lane = lax.broadcasted_iota(jnp.int32, (1, LANES), 1)
    lo = lane < HEAD_DIM
    q0 = (q * jnp.where(lo, 1.0, 0.0).astype(F32)).astype(BF16)
    q1 = (q * jnp.where(lo, 0.0, 1.0).astype(F32)).astype(BF16)
    @pl.when(pl.program_id(2) == 0)
    def _():
        off = 0
        for v_ref in v_refs:
            ns = v_ref.shape[0]
            for c0 in range(0, ns, XPOSE_ROWS):
                n = min(XPOSE_ROWS, ns - c0)
                vt_ref[0:LANES, off + c0:off + c0 + n] = v_ref[c0:c0 + n, :].astype(F32).T.astype(BF16)
            off += ns
        vt_ref[LANES:, :] = jnp.ones((vt_ref.shape[0] - LANES, vt_ref.shape[1]), BF16)

    chunks = []
    off = 0
    for k_ref in k_refs:
        ns = k_ref.shape[0]
        step = min(kc, ns)
        chunks += [(k_ref, c0, step, off + c0) for c0 in range(0, ns, step)]
        off += ns

    qms = (q0, q1)

    def scores(mi, ci):
        k_ref, c0, step, _ = chunks[ci]
        return _dot_nt(k_ref[c0:c0 + step, :], qms[mi])

    run_max = [None, None]
    accs = [None, None]

    def probs(mi, sc):
        mc = jnp.max(sc, axis=0, keepdims=True)
        if run_max[mi] is None:
            run_max[mi], alpha = mc, None
        else:
            m_new = jnp.maximum(run_max[mi], mc)
            alpha = jnp.exp2(run_max[mi] - m_new)
            run_max[mi] = m_new
        return jnp.exp2((sc - run_max[mi]).astype(BF16)), alpha

    def accumulate(mi, ci, pb, alpha):
        _, _, step, g0 = chunks[ci]
        pv = _dot(vt_ref[:, g0:g0 + step], pb)
        accs[mi] = pv if alpha is None else accs[mi] * alpha + pv

    n_ch = len(chunks)
    sc_q = {ci: [scores(mi, ci) for mi in range(2)] for ci in range(min(2, n_ch))}
    pb_q = {0: [probs(mi, sc_q[0][mi]) for mi in range(2)]}
    for ci in range(n_ch):
        if ci + 2 < n_ch:
            sc_q[ci + 2] = [scores(mi, ci + 2) for mi in range(2)]
        if ci + 1 < n_ch:
            sc_pair = sc_q.pop(ci + 1)
            pb_q[ci + 1] = [probs(mi, sc_pair[mi]) for mi in range(2)]
        for mi, (pb, alpha) in enumerate(pb_q.pop(ci)):
            accumulate(mi, ci, pb, alpha)
    stats = [(run_max[mi], accs[mi]) for mi in range(2)]
    r0 = 1.0 / stats[0][1][LANES:LANES + 1, :]
    r1 = lam / stats[1][1][LANES:LANES + 1, :]
    o = (stats[0][1][0:LANES, :] * r0 - stats[1][1][0:LANES, :] * r1).T
    ms = jnp.mean(o * o, axis=-1, keepdims=True)
    o_ref[...] = ((o * lax.rsqrt(ms + EPS) * g_ref[...]) * (1.0 - lam_init)).astype(BF16)


def _diff_attn(lam_vec, subln_g, q, kv_list, *, tq, lam_init, kc=256):
    bsz, n_q, _ = q.shape
    in_specs = [pl.BlockSpec((4, HEAD_DIM), lambda b, h, i: (0, 0)),
                pl.BlockSpec((1, LANES), lambda b, h, i: (0, 0)),
                pl.BlockSpec((None, tq, LANES), lambda b, h, i: (b, i, h))]
    args = [lam_vec, subln_g, q]
    for k, v in kv_list:
        ns = k.shape[1]
        in_specs += [pl.BlockSpec((None, ns, LANES), lambda b, h, i: (b, 0, h))] * 2
        args += [k, v]
    return pl.pallas_call(
        functools.partial(_diff_attn_kernel, nseg=len(kv_list), lam_init=lam_init, kc=kc),
        grid=(bsz, B_HEADS, n_q // tq),
        in_specs=in_specs,
        out_specs=pl.BlockSpec((None, tq, LANES), lambda b, h, i: (b, i, h)),
        out_shape=jax.ShapeDtypeStruct((bsz, n_q, B_V), BF16),
        scratch_shapes=[pltpu.VMEM((LANES + ONES_ROWS, sum(k.shape[1] for k, _ in kv_list)), BF16)],
        compiler_params=_cparams(("parallel", "parallel", "arbitrary")),
        name="diff_attn_%dseg" % len(kv_list),
    )(*args)


FFN_HALO = 16
FFN_TF = 256


def _ffn_kernel(*refs, n_mix, tiles_per_seq, final_norm):
    a_refs = refs[:3 * n_mix]
    wo_refs = refs[3 * n_mix:4 * n_mix]
    (h_ref, hp_ref, hn_ref, g1_ref, sh_ref, sc_ref, gt_ref, ng_ref, wu_ref, wg_ref, cwu_ref, cwg_ref,
     wd_ref, fg_ref, o_ref, xn_ref, hu_a, hg_a, hu_b, hg_b, acc_ref) = refs[4 * n_mix:]
    i = pl.program_id(0)
    tm = h_ref.shape[0]
    hl = FFN_HALO
    nj = wu_ref.shape[0]

    y = None
    for m in range(n_mix):
        ap_ref, a_ref, an_ref = a_refs[3 * m + 1], a_refs[3 * m], a_refs[3 * m + 2]
        a_ext = jnp.concatenate([ap_ref[...], a_ref[...], an_ref[...]], axis=0)
        part = _dot(a_ext, wo_refs[m][...])
        y = part if y is None else y + part
    h1 = jnp.concatenate([hp_ref[...], h_ref[...], hn_ref[...]], axis=0) + g1_ref[...] * y
    o_ref[...] = h1[hl:hl + tm, :]

    pos = i % tiles_per_seq
    xn = _norm_mod(h1, ng_ref[...], sc_ref[...], sh_ref[...])
    xn_ref[hl:hl + tm, :] = xn[hl:hl + tm, :].astype(BF16)
    xn_ref[0:hl, :] = (xn[0:hl, :] * jnp.where(pos > 0, 1.0, 0.0)).astype(BF16)
    xn_ref[hl + tm:, :] = (xn[hl + tm:, :] * jnp.where(pos < tiles_per_seq - 1, 1.0, 0.0)).astype(BF16)
    acc_ref[...] = jnp.zeros_like(acc_ref)

    def up(j, hu_ref, hg_ref):
        xn = xn_ref[...]
        hu_ref[...] = _dot(xn, wu_ref[j])
        hg_ref[...] = _dot(xn, wg_ref[j])

    def conv(ref, cw):
        return (ref[hl - 1:hl - 1 + tm, :] * cw[0:1] + ref[hl:hl + tm, :] * cw[1:2]
                + ref[hl + 1:hl + 1 + tm, :] * cw[2:3] + cw[3:4])

    def act(j, hu_ref, hg_ref):
        u = conv(hu_ref, cwu_ref[j])
        gt = conv(hg_ref, cwg_ref[j])
        return (gt * _sigmoid(gt) * u).astype(BF16)

    up(0, hu_a, hg_a)

    def pair(jj, carry):
        j = 2 * jj
        up(j + 1, hu_b, hg_b)
        acc_ref[...] += _dot(act(j, hu_a, hg_a), wd_ref[j])
        up(j + 2, hu_a, hg_a)
        acc_ref[...] += _dot(act(j + 1, hu_b, hg_b), wd_ref[j + 1])
        return carry

    assert nj % 2 == 1
    lax.fori_loop(0, (nj - 1) // 2, pair, 0)
    acc_ref[...] += _dot(act(nj - 1, hu_a, hg_a), wd_ref[nj - 1])

    y = o_ref[...] + gt_ref[...] * acc_ref[...]
    if final_norm:
        ms = jnp.mean(y * y, axis=-1, keepdims=True)
        y = y * lax.rsqrt(ms + EPS) * fg_ref[...]
    o_ref[...] = y


def _ffn_weights(w_up, conv_w, conv_b, w_down, tf):
    d = w_up.shape[0]
    nj = D_FF // tf
    chunked = lambda w: w.reshape(w.shape[0], nj, tf).transpose(1, 0, 2)
    w_uu = chunked(w_up[:, :D_FF]).astype(BF16)
    w_ug = chunked(w_up[:, D_FF:]).astype(BF16)
    cw = jnp.concatenate([conv_w, conv_b[None, :]], axis=0)
    return (w_uu, w_ug, chunked(cw[:, :D_FF]), chunked(cw[:, D_FF:]), w_down.reshape(nj, tf, d).astype(BF16))


def _mixer_out_ffn(acts, w_outs, h2d, mod, norm_g, weights, final_g, *, tm, row_of, tiles_per_seq, final_norm):
    m, d = h2d.shape
    hl = FFN_HALO
    nhb = m // hl
    r = tm // hl
    w_uu, w_ug, cwu, cwg, w_dn = weights
    tf = w_uu.shape[2]
    resident = lambda a: pl.BlockSpec(a.shape, lambda i: (0,) * a.ndim, pipeline_mode=pl.Buffered(1))

    def tile_and_halos(width):
        return [pl.BlockSpec((tm, width), lambda i: (i, 0)),
                pl.BlockSpec((hl, width), lambda i: (jnp.maximum(i * r - 1, 0), 0)),
                pl.BlockSpec((hl, width), lambda i: (jnp.minimum((i + 1) * r, nhb - 1), 0))]

    in_specs, args = [], []
    for a in acts:
        in_specs += tile_and_halos(a.shape[1])
        args += [a, a, a]
    in_specs += [resident(w) for w in w_outs]
    args += list(w_outs)
    in_specs += tile_and_halos(d)
    in_specs += [_mod_spec(2, row_of), _mod_spec(3, row_of), _mod_spec(4, row_of), _mod_spec(5, row_of),
                 pl.BlockSpec((1, d), lambda i: (0, 0)),
                 resident(w_uu), resident(w_ug), resident(cwu), resident(cwg), resident(w_dn),
                 pl.BlockSpec((1, d), lambda i: (0, 0))]
    args += [h2d, h2d, h2d, mod, mod, mod, mod, norm_g, w_uu, w_ug, cwu, cwg, w_dn, final_g]
    return pl.pallas_call(
        functools.partial(_ffn_kernel, n_mix=len(acts), tiles_per_seq=tiles_per_seq, final_norm=final_norm),
        grid=(m // tm,),
        in_specs=in_specs,
        out_specs=pl.BlockSpec((tm, d), lambda i: (i, 0)),
        out_shape=jax.ShapeDtypeStruct((m, d), F32),
        scratch_shapes=[pltpu.VMEM((tm + 2 * hl, d), BF16)]
        + [pltpu.VMEM((tm + 2 * hl, tf), F32)] * 4
        + [pltpu.VMEM((tm, d), F32)],
        compiler_params=_cparams(("parallel",)),
        name="mixer_out_conv_ffn",
    )(*args)


def _gla_inproj_kernel(x_ref, sh_ref, sc_ref, g_ref, w_ref, w2_ref, gb_ref,
                       q_ref, k_ref, v_ref, sg_ref, laf_ref, lab_ref):
    xn = _norm_mod(x_ref[...], g_ref[...], sc_ref[...], sh_ref[...]).astype(BF16)
    q_ref[...] = (_dot(xn, w_ref[:, 0:C_QK]) * (C_DK ** -0.5)).astype(BF16)
    k_ref[...] = _dot(xn, w_ref[:, C_QK:2 * C_QK]).astype(BF16)
    for c in range(C_V // C_QK):
        lo = 2 * C_QK + c * C_QK
        v_ref[:, c * C_QK:(c + 1) * C_QK] = _dot(xn, w_ref[:, lo:lo + C_QK]).astype(BF16)
    for c in range(C_V // C_QK):
        lo = 2 * C_QK + C_V + c * C_QK
        gg = _dot(xn, w_ref[:, lo:lo + C_QK])
        sg_ref[:, c * C_QK:(c + 1) * C_QK] = (gg * _sigmoid(gg)).astype(BF16)
    lo = 2 * C_QK + 2 * C_V
    r = _dot(xn, w_ref[:, lo:lo + LANES]).astype(BF16)
    for dr, la_ref in enumerate((laf_ref, lab_ref)):
        z = _dot(r, w2_ref[:, dr * C_QK:(dr + 1) * C_QK]) + gb_ref[:, dr * C_QK:(dr + 1) * C_QK]
        la_ref[...] = (jnp.minimum(z, 0.0) - jnp.log(1.0 + jnp.exp(-jnp.abs(z)))) * (1.0 / C_GATE_NORM)


def _gla_inproj(x2d, mod, norm_g, w, w2, gb, *, tm, row_of):
    m, d = x2d.shape
    widths = (C_QK, C_QK, C_V, C_V, C_QK, C_QK)
    dts = (BF16, BF16, BF16, BF16, F32, F32)
    return pl.pallas_call(
        _gla_inproj_kernel,
        grid=(m // tm,),
        in_specs=[
            pl.BlockSpec((tm, d), lambda i: (i, 0)),
            _mod_spec(0, row_of), _mod_spec(1, row_of),
            pl.BlockSpec((1, d), lambda i: (0, 0)),
            pl.BlockSpec(w.shape, lambda i: (0, 0)),
            pl.BlockSpec(w2.shape, lambda i: (0, 0)),
            pl.BlockSpec(gb.shape, lambda i: (0, 0)),
        ],
        out_specs=[pl.BlockSpec((tm, wd), lambda i: (i, 0)) for wd in widths],
        out_shape=[jax.ShapeDtypeStruct((m, wd), dt) for wd, dt in zip(widths, dts)],
        compiler_params=_cparams(("parallel",)),
        name="gla_inproj",
    )(x2d, mod, mod, norm_g, w, w2, gb)


def _tri(n, reverse):
    r = lax.broadcasted_iota(jnp.int32, (n, n), 0)
    c = lax.broadcasted_iota(jnp.int32, (n, n), 1)
    return (c >= r) if reverse else (c <= r)


def _cumsum_rows(la, tri_bf):
    hi = la.astype(BF16)
    r1 = la - hi.astype(F32)
    mid = r1.astype(BF16)
    lo = (r1 - mid.astype(F32)).astype(BF16)
    return _dot(tri_bf, hi) + _dot(tri_bf, mid) + _dot(tri_bf, lo)


def _gla_ctx_state_kernel(k_ref, v_ref, laf_ref, lab_ref, sf_ref, sb_ref):
    n = k_ref.shape[0]
    for reverse, la_ref, s_ref in ((False, laf_ref, sf_ref), (True, lab_ref, sb_ref)):
        tri = jnp.where(_tri(n, reverse), 1.0, 0.0).astype(BF16)
        b = _cumsum_rows(la_ref[...], tri)
        b_end = b[0:1, :] if reverse else b[n - 1:n, :]
        kw = (k_ref[...].astype(F32) * jnp.exp(b_end - b)).astype(BF16)
        for h in range(C_HEADS):
            s_ref[h] = _dot_tn(v_ref[:, h * C_DV:(h + 1) * C_DV], kw[:, h * C_DK:(h + 1) * C_DK])


def _gla_ctx_state(kc, vc, lac_f, lac_b):
    bsz, n, _ = kc.shape
    s_shape = jax.ShapeDtypeStruct((bsz, C_HEADS, C_DV, C_DK), F32)
    s_spec = pl.BlockSpec((None, C_HEADS, C_DV, C_DK), lambda b: (b, 0, 0, 0))
    return pl.pallas_call(
        _gla_ctx_state_kernel,
        grid=(bsz,),
        in_specs=[pl.BlockSpec((None, n, C_QK), lambda b: (b, 0, 0)),
                  pl.BlockSpec((None, n, C_V), lambda b: (b, 0, 0)),
                  pl.BlockSpec((None, n, C_QK), lambda b: (b, 0, 0)),
                  pl.BlockSpec((None, n, C_QK), lambda b: (b, 0, 0))],
        out_specs=[s_spec, s_spec],
        out_shape=[s_shape, s_shape],
        compiler_params=_cparams(("parallel",)),
        name="gla_ctx_state",
    )(kc, vc, lac_f, lac_b)


def _gla_scan_kernel(q_ref, k_ref, v_ref, la_ref, s0_ref, *refs, reverse, final):
    if final:
        ob_ref, sg_ref, ng_ref, o_ref, st_ref = refs
    else:
        o_ref, st_ref = refs
    gidx = pl.program_id(1)

    @pl.when(gidx == 0)
    def _():
        st_ref[...] = s0_ref[...]

    gt = q_ref.shape[0]
    nchunk = gt // C_CHUNK
    c = C_CHUNK
    r = lax.broadcasted_iota(jnp.int32, (gt, gt), 0)
    s = lax.broadcasted_iota(jnp.int32, (gt, gt), 1)
    shift = c.bit_length() - 1
    same_chunk = (r >> shift) == (s >> shift)
    tri = same_chunk & ((s >= r) if reverse else (s <= r))
    tri_bf = jnp.where(tri, 1.0, 0.0).astype(BF16)

    def per_chunk_row(x, row):
        return jnp.concatenate([jnp.broadcast_to(x[ci * c + row:ci * c + row + 1, :], (c, x.shape[1]))
                                for ci in range(nchunk)], axis=0)

    b = _cumsum_rows(la_ref[...], tri_bf)
    b_mid = per_chunk_row(b, c // 2)
    b_end = per_chunk_row(b, 0 if reverse else c - 1)
    qf = q_ref[...].astype(F32)
    kf = k_ref[...].astype(F32)
    q_in = (qf * jnp.exp(b - b_mid)).astype(BF16)
    k_in = (kf * jnp.exp(b_mid - b)).astype(BF16)
    q_out = (qf * jnp.exp(b)).astype(BF16)
    k_out = (kf * jnp.exp(b_end - b)).astype(BF16)
    decay = jnp.exp(b_end)

    order = range(nchunk - 1, -1, -1) if reverse else range(nchunk)
    kcols = [slice(h * C_DK, (h + 1) * C_DK) for h in range(C_HEADS)]
    vcols_of = [slice(h * C_DV, (h + 1) * C_DV) for h in range(C_HEADS)]
    o_intra, upd = [], []
    for h in range(C_HEADS):
        vv = v_ref[:, vcols_of[h]]
        sc = jnp.where(tri, _dot_nt(q_in[:, kcols[h]], k_in[:, kcols[h]]), 0.0).astype(BF16)
        o_intra.append(_dot(sc, vv))
        upd.append({ci: _dot_tn(vv[ci * c:(ci + 1) * c, :], k_out[ci * c:(ci + 1) * c, kcols[h]])
                    for ci in order})
    st = [st_ref[h] for h in range(C_HEADS)]
    o_inter = [{} for _ in range(C_HEADS)]
    for ci in order:
        rows = slice(ci * c, (ci + 1) * c)
        for h in range(C_HEADS):
            o_inter[h][ci] = _dot_nt(q_out[rows, kcols[h]], st[h].astype(BF16))
            st[h] = st[h] * decay[ci * c:ci * c + 1, kcols[h]] + upd[h][ci]
    for h in range(C_HEADS):
        vcols = vcols_of[h]
        st_ref[h] = st[h]
        o = o_intra[h] + jnp.concatenate([o_inter[h][ci] for ci in range(nchunk)], axis=0)
        if final:
            o = o + ob_ref[:, vcols]
            ms = jnp.mean(o * o, axis=-1, keepdims=True)
            o = (o * lax.rsqrt(ms + EPS) * ng_ref[...]) * sg_ref[:, vcols].astype(F32)
            o_ref[:, vcols] = o.astype(BF16)
        else:
            o_ref[:, vcols] = o


def _gla_scan(q, k, v, la, s0, *, gt, reverse, o_other=None, sg=None, norm_g=None):
    bsz, n_tok, _ = q.shape
    ng = n_tok // gt
    final = o_other is not None
    gi = (lambda b, g: (b, ng - 1 - g, 0)) if reverse else (lambda b, g: (b, g, 0))
    in_specs = [pl.BlockSpec((None, gt, C_QK), gi), pl.BlockSpec((None, gt, C_QK), gi),
                pl.BlockSpec((None, gt, C_V), gi), pl.BlockSpec((None, gt, C_QK), gi),
                pl.BlockSpec((None, C_HEADS, C_DV, C_DK), lambda b, g: (b, 0, 0, 0))]
    args = [q, k, v, la, s0]
    if final:
        in_specs += [pl.BlockSpec((None, gt, C_V), gi), pl.BlockSpec((None, gt, C_V), gi),
                     pl.BlockSpec((1, C_DV), lambda b, g: (0, 0))]
        args += [o_other, sg, norm_g]
    return pl.pallas_call(
        functools.partial(_gla_scan_kernel, reverse=reverse, final=final),
        grid=(bsz, ng),
        in_specs=in_specs,
        out_specs=pl.BlockSpec((None, gt, C_V), gi),
        out_shape=jax.ShapeDtypeStruct((bsz, n_tok, C_V), BF16 if final else F32),
        scratch_shapes=[pltpu.VMEM((C_HEADS, C_DV, C_DK), F32)],
        compiler_params=_cparams(("parallel", "arbitrary")),
        name="gla_scan_fwd_final" if final else "gla_scan_bwd",
    )(*args)


def _pair_split(n_heads):
    base = np.concatenate([np.arange(0, HEAD_DIM, 2), np.arange(1, HEAD_DIM, 2)])
    return np.concatenate([h * HEAD_DIM + base for h in range(n_heads)])


_A_HEAD_ORDER = np.array([kv * A_GROUP + j for j in range(A_GROUP) for kv in range(A_KV_HEADS)])


def _attn_in_cols():
    aq = (_A_HEAD_ORDER[:, None] * HEAD_DIM + _pair_split(1)[None, :]).reshape(-1)
    o_ak, o_av, o_bq = A_Q, A_Q + A_KV, A_Q + 2 * A_KV
    o_bk, o_bv = o_bq + B_QK, o_bq + 2 * B_QK
    return np.concatenate([aq, o_bq + _pair_split(2 * B_HEADS), o_bk + _pair_split(2 * B_HEADS),
                           o_ak + _pair_split(A_KV_HEADS), o_av + np.arange(A_KV), o_bv + np.arange(B_V)])


def _attn_out_rows():
    oa = (_A_HEAD_ORDER[:, None] * HEAD_DIM + np.arange(HEAD_DIM)[None, :]).reshape(-1)
    return oa


def _rope_tables(n_tok):
    rows = n_tok // GRID_W
    row = jnp.repeat(jnp.arange(rows, dtype=F32), GRID_W)
    col = jnp.tile(jnp.arange(GRID_W, dtype=F32), rows)
    axis_dim = HEAD_DIM // 2
    inv_freq = ROPE_THETA ** (-jnp.arange(0, axis_dim, 2, dtype=F32) / axis_dim)
    ang = jnp.concatenate([row[:, None] * inv_freq, col[:, None] * inv_freq], axis=-1)
    cos, sin = jnp.cos(ang), jnp.sin(ang)
    cos_t = jnp.tile(cos, (1, LANES // (HEAD_DIM // 2)))
    sin_t = jnp.tile(jnp.concatenate([-sin, sin], axis=-1), (1, LANES // HEAD_DIM))
    return cos_t, sin_t


def _pick(n, pref):
    return pref if n % pref == 0 else n


def kernel(x, c, ctx, c_ctx, mod_w, mod_b, norm1_g, norm2_g, attn_w_in, attn_w_out, attn_sink, diff_lambda, diff_subln_g, gla_w_in, gla_gate_w1, gla_gate_w2, gla_gate_b, gla_norm_g, gla_w_out, ffn_w_up, ffn_conv_w, ffn_conv_b, ffn_w_down, final_norm_g):
    bsz, n_tok, d = x.shape
    n_ctx = ctx.shape[1]
    assert d == D_MODEL and bsz + 1 <= MOD_ROWS
    m_lat, m_ctx = bsz * n_tok, bsz * n_ctx

    c_rows = jnp.concatenate([c, c_ctx[None, :], jnp.zeros((MOD_ROWS - bsz - 1, d), F32)], axis=0)
    mod_all = _modulation(c_rows, mod_w, mod_b)
    cos_t, sin_t = _rope_tables(n_tok)

    tm = _pick(n_tok, 512)
    tmc = _pick(n_ctx, 256)
    lat_tiles = n_tok // tm
    lat_row = lambda i: i // lat_tiles
    ctx_row = lambda i: bsz
    tm_ffn = _pick(n_tok, 1024)
    ffn_row = lambda i: i // (n_tok // tm_ffn)

    h = x.reshape(m_lat, d)
    hc = ctx.reshape(m_ctx, d)
    for layer in range(DEPTH):
        need_ctx = layer < DEPTH - 1
        last = layer == DEPTH - 1
        mod = mod_all[layer].reshape(MOD_ROWS, 6, 1, d)
        n1 = norm1_g[layer].reshape(1, d)
        n2 = norm2_g[layer].reshape(1, d)
        i = layer // 2
        if layer % 2 == 0:
            lam_init = 0.8 - 0.6 * math.exp(-B_LAMBDA_DECAY * layer)
            w_in = attn_w_in[i][:, _attn_in_cols()].astype(BF16)
            w_out = attn_w_out[i]
            w_oa = w_out[_attn_out_rows()].astype(BF16)
            w_ob = w_out[A_Q:].astype(BF16)
            sink = attn_sink[i]
            subln = diff_subln_g[i].reshape(1, LANES)
            aq, bq, bk, ak, av, bv = _attn_inproj(h, mod, n1, w_in, cos_t, sin_t, tm=tm, row_of=lat_row,
                                                  rope=True, tiles_per_seq=lat_tiles)
            caq, cbq, cbk, cak, cav, cbv = _attn_inproj(hc, mod, n1, w_in, cos_t, sin_t, tm=tmc, row_of=ctx_row,
                                                        rope=False, tiles_per_seq=1)
            r3 = lambda a, n: a.reshape(bsz, n, a.shape[-1])
            cak3, cav3, cbk3, cbv3 = r3(cak, n_ctx), r3(cav, n_ctx), r3(cbk, n_ctx), r3(cbv, n_ctx)
            oa = _gqa_window(sink, r3(aq, n_tok), r3(ak, n_tok), r3(av, n_tok), cak3, cav3)
            ob = _diff_attn(diff_lambda[i], subln, r3(bq, n_tok), [(r3(bk, n_tok), r3(bv, n_tok)), (cbk3, cbv3)],
                            tq=_pick(n_tok, 512), lam_init=lam_init)
            mix, w_mix = [oa.reshape(m_lat, A_Q), ob.reshape(m_lat, B_V)], [w_oa, w_ob]
            if need_ctx:
                oca = _gqa_context(sink, r3(caq, n_ctx), cak3, cav3)
                ocb = _diff_attn(diff_lambda[i], subln, r3(cbq, n_ctx), [(cbk3, cbv3)],
                                 tq=_pick(n_ctx, 256), lam_init=lam_init)
                mix_c = [oca.reshape(m_ctx, A_Q), ocb.reshape(m_ctx, B_V)]
        else:
            w1 = gla_gate_w1[i]
            pad = jnp.zeros((d, LANES - 2 * C_GATE_RANK), F32)
            w_in = jnp.concatenate([gla_w_in[i], w1[0], w1[1], pad], axis=1).astype(BF16)
            w2 = gla_gate_w2[i]
            w2bd = jnp.zeros((LANES, 2 * C_QK), F32)
            w2bd = w2bd.at[0:C_GATE_RANK, 0:C_QK].set(w2[0]).at[C_GATE_RANK:2 * C_GATE_RANK, C_QK:].set(w2[1])
            w2bd = w2bd.astype(BF16)
            gb = gla_gate_b[i].reshape(1, 2 * C_QK)
            ng = gla_norm_g[i].reshape(1, C_DV)
            q, k, v, sg, la_f, la_b = _gla_inproj(h, mod, n1, w_in, w2bd, gb, tm=tm, row_of=lat_row)
            qc, kc, vc, sgc, lac_f, lac_b = _gla_inproj(hc, mod, n1, w_in, w2bd, gb, tm=tmc, row_of=ctx_row)
            r3 = lambda a, n: a.reshape(bsz, n, a.shape[-1])
            s_f, s_b = _gla_ctx_state(r3(kc, n_ctx), r3(vc, n_ctx), r3(lac_f, n_ctx), r3(lac_b, n_ctx))
            gt = _pick(n_tok, 256)
            q3, k3, v3 = r3(q, n_tok), r3(k, n_tok), r3(v, n_tok)
            o_b = _gla_scan(q3, k3, v3, r3(la_b, n_tok), s_b, gt=gt, reverse=True)
            og = _gla_scan(q3, k3, v3, r3(la_f, n_tok), s_f, gt=gt, reverse=False,
                           o_other=o_b, sg=r3(sg, n_tok), norm_g=ng)
            mix, w_mix = [og.reshape(m_lat, C_V)], [gla_w_out[i].astype(BF16)]
            if need_ctx:
                z = jnp.zeros((bsz, C_HEADS, C_DV, C_DK), F32)
                qc3, kc3, vc3 = r3(qc, n_ctx), r3(kc, n_ctx), r3(vc, n_ctx)
                gtc = _pick(n_ctx, 256)
                oc_b = _gla_scan(qc3, kc3, vc3, r3(lac_b, n_ctx), z, gt=gtc, reverse=True)
                ogc = _gla_scan(qc3, kc3, vc3, r3(lac_f, n_ctx), z, gt=gtc, reverse=False,
                                o_other=oc_b, sg=r3(sgc, n_ctx), norm_g=ng)
                mix_c = [ogc.reshape(m_ctx, C_V)]
        ffn_w = _ffn_weights(ffn_w_up[layer], ffn_conv_w[layer], ffn_conv_b[layer], ffn_w_down[layer], FFN_TF)
        fg = final_norm_g.reshape(1, d)
        h = _mixer_out_ffn(mix, w_mix, h, mod, n2, ffn_w, fg, tm=tm_ffn, row_of=ffn_row,
                           tiles_per_seq=n_tok // tm_ffn, final_norm=last)
        if need_ctx:
            hc = _mixer_out_ffn(mix_c, w_mix, hc, mod, n2, ffn_w, fg, tm=tmc, row_of=ctx_row,
                                tiles_per_seq=n_ctx // tmc, final_norm=False)
    return h.reshape(bsz, n_tok, d)
```

```python
import functools
import math

import numpy as np
import jax
import jax.numpy as jnp
from jax import lax
from jax.experimental import pallas as pl
from jax.experimental.pallas import tpu as pltpu

F32 = jnp.float32
BF16 = jnp.bfloat16

D_MODEL = 1024
DEPTH = 2
GRID_W = 64
HEAD_DIM = 64
ROPE_THETA = 10000.0
EPS = 1e-6
BLOCK = 128
A_HEADS = 8
A_KV_HEADS = 2
A_GROUP = A_HEADS // A_KV_HEADS
B_HEADS = 4
B_LAMBDA_DECAY = 0.3
A_Q = A_HEADS * HEAD_DIM
A_KV = A_KV_HEADS * HEAD_DIM
B_QK = B_HEADS * 2 * HEAD_DIM
B_V = B_HEADS * 2 * HEAD_DIM
C_HEADS = 4
C_DK = D_MODEL // 2 // C_HEADS
C_DV = D_MODEL // C_HEADS
C_GATE_RANK = 16
C_GATE_NORM = 16.0
C_CHUNK = 64
C_QK = C_HEADS * C_DK
C_V = C_HEADS * C_DV
D_FF = 2816
LANES = 128
MOD_ROWS = 8
VMEM_LIMIT = 56 * 1024 * 1024


def _cparams(sem):
    return pltpu.CompilerParams(dimension_semantics=sem, vmem_limit_bytes=VMEM_LIMIT)


def _dot(a, b):
    return jnp.dot(a, b, preferred_element_type=F32)


def _dot_nt(a, b):
    return lax.dot_general(a, b, (((1,), (1,)), ((), ())), preferred_element_type=F32)


def _dot_tn(a, b):
    return lax.dot_general(a, b, (((0,), (0,)), ((), ())), preferred_element_type=F32)


def _sigmoid(x):
    return 1.0 / (1.0 + jnp.exp(-x))


def _norm_mod(x, g, sc, sh):
    ms = jnp.mean(x * x, axis=-1, keepdims=True)
    return (x * lax.rsqrt(ms + EPS) * g) * (1.0 + sc) + sh


def _mod_kernel(c_ref, w_ref, b_ref, o_ref):
    c = c_ref[...]
    s = (c * _sigmoid(c)).astype(BF16)
    o_ref[...] = _dot(s, w_ref[...].astype(BF16)) + b_ref[...]


def _modulation(c_rows, mod_w, mod_b):
    d = D_MODEL
    return pl.pallas_call(
        _mod_kernel,
        grid=(DEPTH, 6),
        in_specs=[
            pl.BlockSpec((MOD_ROWS, d), lambda l, n: (0, 0)),
            pl.BlockSpec((None, d, d), lambda l, n: (l, 0, n)),
            pl.BlockSpec((None, 1, d), lambda l, n: (l, 0, n)),
        ],
        out_specs=pl.BlockSpec((None, MOD_ROWS, d), lambda l, n: (l, 0, n)),
        out_shape=jax.ShapeDtypeStruct((DEPTH, MOD_ROWS, 6 * d), F32),
        compiler_params=_cparams(("parallel", "parallel")),
        name="modulation",
    )(c_rows, mod_w, mod_b.reshape(DEPTH, 1, 6 * d))


def _mod_spec(k, row_of):
    return pl.BlockSpec((None, None, 1, D_MODEL), lambda i, *_: (row_of(i), k, 0, 0))


LOG2E = math.log2(math.e)
_Q_SCALE = HEAD_DIM ** -0.5 * LOG2E
_ATTN_GROUPS = (("aq", A_Q, True, _Q_SCALE), ("bq", B_QK, True, _Q_SCALE), ("bk", B_QK, True, 1.0),
                ("ak", A_KV, True, 1.0), ("av", A_KV, False, 1.0), ("bv", B_V, False, 1.0))


def _rope_chunk(v, cos, sin, first):
    partner = jnp.where(first, pltpu.roll(v, 96, 1), pltpu.roll(v, 32, 1))
    return v * cos + partner * sin


def _attn_inproj_kernel(x_ref, sh_ref, sc_ref, g_ref, w_ref, cos_ref, sin_ref, *out_refs, rope):
    xn = _norm_mod(x_ref[...], g_ref[...], sc_ref[...], sh_ref[...]).astype(BF16)
    tm = xn.shape[0]
    if rope:
        cos = cos_ref[...]
        sin = sin_ref[...]
        lane = lax.broadcasted_iota(jnp.int32, (tm, LANES), 1)
        first = (lane % HEAD_DIM) < (HEAD_DIM // 2)
    col = 0
    for (name, width, roped, scale), o_ref in zip(_ATTN_GROUPS, out_refs):
        y = _dot(xn, w_ref[:, col:col + width])
        col += width
        for c in range(width // LANES):
            v = y[:, c * LANES:(c + 1) * LANES]
            if rope and roped:
                v = _rope_chunk(v, cos, sin, first)
            if scale != 1.0:
                v = v * scale
            o_ref[:, c * LANES:(c + 1) * LANES] = v.astype(BF16)


def _attn_inproj(x2d, mod, norm_g, w, cos_t, sin_t, *, tm, row_of, rope, tiles_per_seq):
    m, d = x2d.shape
    n_all = w.shape[1]
    out_shape = [jax.ShapeDtypeStruct((m, width), BF16) for (_, width, _, _) in _ATTN_GROUPS]
    out_specs = [pl.BlockSpec((tm, width), lambda i: (i, 0)) for (_, width, _, _) in _ATTN_GROUPS]
    return pl.pallas_call(
        functools.partial(_attn_inproj_kernel, rope=rope),
        grid=(m // tm,),
        in_specs=[
            pl.BlockSpec((tm, d), lambda i: (i, 0)),
            _mod_spec(0, row_of), _mod_spec(1, row_of),
            pl.BlockSpec((1, d), lambda i: (0, 0)),
            pl.BlockSpec((d, n_all), lambda i: (0, 0)),
            pl.BlockSpec((tm, LANES), lambda i: (i % tiles_per_seq, 0)),
            pl.BlockSpec((tm, LANES), lambda i: (i % tiles_per_seq, 0)),
        ],
        out_specs=out_specs,
        out_shape=out_shape,
        compiler_params=_cparams(("parallel",)),
        name="attn_inproj_rope" if rope else "attn_inproj_ctx",
    )(x2d, mod, mod, norm_g, w, cos_t, sin_t)


GQA_QB = 2


def _gqa_kernel(sink_ref, q_ref, *refs, window, nb):
    tq = BLOCK
    qb = q_ref.shape[0] // tq
    if window:
        k_blk = [r[...] for r in refs[0:qb + 2]]
        v_blk = [r[...] for r in refs[qb + 2:2 * qb + 4]]
        kx_ref, vx_ref, o_ref = refs[2 * qb + 4:]
    else:
        kx_ref, vx_ref, o_ref = refs
    lane = lax.broadcasted_iota(jnp.int32, (1, LANES), 1)
    lo = lane < HEAD_DIM
    half = (jnp.where(lo, 1.0, 0.0).astype(F32), jnp.where(lo, 0.0, 1.0).astype(F32))
    row = lax.broadcasted_iota(jnp.int32, (LANES, 1), 0)
    sinks = [jnp.concatenate([jnp.full((1, tq), sink_ref[kv * A_GROUP + j] * LOG2E, F32)
                              for j in range(A_GROUP)], axis=1) for kv in range(A_KV_HEADS)]
    work = []
    for u in range(qb):
        if window:
            keys = jnp.concatenate(k_blk[u:u + 3] + [kx_ref[...]], axis=0)
            vals = jnp.concatenate(v_blk[u:u + 3] + [vx_ref[...]], axis=0)
        else:
            keys, vals = kx_ref[...], vx_ref[...]
        qf = [q_ref[u * tq:(u + 1) * tq, j * LANES:(j + 1) * LANES].astype(F32) for j in range(A_GROUP)]
        scs = [_dot_nt(keys, jnp.concatenate([(q * half[kv]).astype(BF16) for q in qf], axis=0))
               for kv in range(A_KV_HEADS)]
        work.append((vals, scs))
    for u, (vals, scs) in enumerate(work):
        if window:
            n = pl.program_id(1) * qb + u
            s = lax.broadcasted_iota(jnp.int32, (BLOCK, tq), 0)
            t = lax.broadcasted_iota(jnp.int32, (BLOCK, tq), 1)
            has_prev = jnp.where(n > 0, 1, 0)
            has_next = jnp.where(n < nb - 1, 1, 0)
            lower = t * has_prev + BLOCK * (1 - has_prev)
            upper = (t + 1) * has_next - 1
            ninf = jnp.float32(-jnp.inf)
            bias_prev = jnp.concatenate([jnp.where(s < lower, ninf, 0.0)] * A_GROUP, axis=1)
            bias_next = jnp.concatenate([jnp.where(s > upper, ninf, 0.0)] * A_GROUP, axis=1)
        outs = []
        for kv in range(A_KV_HEADS):
            sc = scs[kv]
            if window:
                sc = jnp.concatenate([sc[0:BLOCK] + bias_prev, sc[BLOCK:2 * BLOCK],
                                      sc[2 * BLOCK:3 * BLOCK] + bias_next, sc[3 * BLOCK:]], axis=0)
            sk = sinks[kv]
            mx = jnp.maximum(jnp.max(sc, axis=0, keepdims=True), sk)
            p = jnp.exp2(sc - mx)
            den = jnp.sum(p, axis=0, keepdims=True) + jnp.exp2(sk - mx)
            pn = (p * (1.0 / den)).astype(BF16)
            outs.append(_dot_tn(vals, pn))
        o_t = jnp.where(row < HEAD_DIM, outs[0], outs[1])
        for j in range(A_GROUP):
            o_ref[u * tq:(u + 1) * tq, j * LANES:(j + 1) * LANES] = o_t[:, j * tq:(j + 1) * tq].T.astype(BF16)


def _gqa_window(sink, aq, ak, av, cak, cav):
    bsz, n_tok, _ = aq.shape
    n_ctx = cak.shape[1]
    nb = n_tok // BLOCK
    qb = GQA_QB
    assert nb % qb == 0
    kv_blocks = [pl.BlockSpec((None, BLOCK, A_KV),
                              lambda b, n, off=off: (b, jnp.clip(n * qb + off, 0, nb - 1), 0))
                 for off in range(-1, qb + 1)]
    kv_ctx = pl.BlockSpec((None, n_ctx, A_KV), lambda b, n: (b, 0, 0))
    return pl.pallas_call(
        functools.partial(_gqa_kernel, window=True, nb=nb),
        grid=(bsz, nb // qb),
        in_specs=[pl.BlockSpec(memory_space=pltpu.SMEM),
                  pl.BlockSpec((None, qb * BLOCK, A_Q), lambda b, n: (b, n, 0))]
        + kv_blocks + kv_blocks + [kv_ctx, kv_ctx],
        out_specs=pl.BlockSpec((None, qb * BLOCK, A_Q), lambda b, n: (b, n, 0)),
        out_shape=jax.ShapeDtypeStruct((bsz, n_tok, A_Q), BF16),
        compiler_params=_cparams(("parallel", "parallel")),
        name="gqa_window",
    )(sink, aq, *([ak] * (qb + 2)), *([av] * (qb + 2)), cak, cav)


def _gqa_context(sink, caq, cak, cav):
    bsz, n_ctx, _ = caq.shape
    nb = n_ctx // BLOCK
    kv_ctx = pl.BlockSpec((None, n_ctx, A_KV), lambda b, n: (b, 0, 0))
    return pl.pallas_call(
        functools.partial(_gqa_kernel, window=False, nb=nb),
        grid=(bsz, nb),
        in_specs=[pl.BlockSpec(memory_space=pltpu.SMEM),
                  pl.BlockSpec((None, BLOCK, A_Q), lambda b, n: (b, n, 0)),
                  kv_ctx, kv_ctx],
        out_specs=pl.BlockSpec((None, BLOCK, A_Q), lambda b, n: (b, n, 0)),
        out_shape=jax.ShapeDtypeStruct((bsz, n_ctx, A_Q), BF16),
        compiler_params=_cparams(("parallel", "parallel")),
        name="gqa_context",
    )(sink, caq, cak, cav)


LAG_LIMIT = 12.0
XPOSE_ROWS = 512
ONES_ROWS = 16


def _diff_attn_kernel(lam_ref, g_ref, q_ref, *refs, nseg, lam_init, kc, lagged):
    k_refs = refs[0:2 * nseg:2]
    v_refs = refs[1:2 * nseg:2]
    if lagged:
        o_ref, ex_ref, vt_ref = refs[2 * nseg:]
    else:
        o_ref, vt_ref = refs[2 * nseg:]
    lv = lam_ref[...]
    lam = (jnp.exp(jnp.sum(lv[0:1] * lv[1:2], axis=-1, keepdims=True))
           - jnp.exp(jnp.sum(lv[2:3] * lv[3:4], axis=-1, keepdims=True)) + lam_init)
    q = q_ref[...].astype(F32)
    lane = lax.broadcasted_iota(jnp.int32, (1, LANES), 1)
    lo = lane < HEAD_DIM
    q0 = (q * jnp.where(lo, 1.0, 0.0).astype(F32)).astype(BF16)
    q1 = (q * jnp.where(lo, 0.0, 1.0).astype(F32)).astype(BF16)
    @pl.when(pl.program_id(2) == 0)
    def _():
        off = 0
        for v_ref in v_refs:
            ns = v_ref.shape[0]
            for c0 in range(0, ns, XPOSE_ROWS):
                n = min(XPOSE_ROWS, ns - c0)
                vt_ref[0:LANES, off + c0:off + c0 + n] = v_ref[c0:c0 + n, :].astype(F32).T.astype(BF16)
            off += ns
        vt_ref[LANES:, :] = jnp.ones((vt_ref.shape[0] - LANES, vt_ref.shape[1]), BF16)

    chunks = []
    off = 0
    for k_ref in k_refs:
        ns = k_ref.shape[0]
        step = min(kc, ns)
        chunks += [(k_ref, c0, step, off + c0) for c0 in range(0, ns, step)]
        off += ns

    qms = (q0, q1)

    def scores(mi, ci):
        k_ref, c0, step, _ = chunks[ci]
        return _dot_nt(k_ref[c0:c0 + step, :], qms[mi])

    run_max = [None, None]
    accs = [None, None]
    excess = [None, None]

    def probs(mi, sc):
        mc = jnp.max(sc, axis=0, keepdims=True)
        if run_max[mi] is None:
            run_max[mi], alpha = (mc, mc), None
        else:
            ref_old, best = run_max[mi]
            new_best = jnp.maximum(best, mc)
            ref = best if lagged else new_best
            alpha = jnp.exp2(ref_old - ref)
            if lagged:
                over = mc - ref
                excess[mi] = over if excess[mi] is None else jnp.maximum(excess[mi], over)
            run_max[mi] = (ref, new_best)
        return jnp.exp2((sc - run_max[mi][0]).astype(BF16)), alpha

    def accumulate(mi, ci, pb, alpha):
        _, _, step, g0 = chunks[ci]
        pv = _dot(vt_ref[:, g0:g0 + step], pb)
        accs[mi] = pv if alpha is None else accs[mi] * alpha + pv

    n_ch = len(chunks)
    sc_q = {ci: [scores(mi, ci) for mi in range(2)] for ci in range(min(2, n_ch))}
    pb_q = {0: [probs(mi, sc_q[0][mi]) for mi in range(2)]}
    for ci in range(n_ch):
        if ci + 2 < n_ch:
            sc_q[ci + 2] = [scores(mi, ci + 2) for mi in range(2)]
        if ci + 1 < n_ch:
            sc_pair = sc_q.pop(ci + 1)
            pb_q[ci + 1] = [probs(mi, sc_pair[mi]) for mi in range(2)]
        for mi, (pb, alpha) in enumerate(pb_q.pop(ci)):
            accumulate(mi, ci, pb, alpha)
    r0 = 1.0 / accs[0][LANES:LANES + 1, :]
    r1 = lam / accs[1][LANES:LANES + 1, :]
    o = (accs[0][0:LANES, :] * r0 - accs[1][0:LANES, :] * r1).T
    ms = jnp.mean(o * o, axis=-1, keepdims=True)
    o_ref[...] = ((o * lax.rsqrt(ms + EPS) * g_ref[...]) * (1.0 - lam_init)).astype(BF16)
    if lagged:
        ex_ref[...] = jnp.maximum(excess[0], excess[1])


def _diff_attn(lam_vec, subln_g, q, kv_list, *, tq, lam_init, kc=256, lagged=False):
    bsz, n_q, _ = q.shape
    out_specs = [pl.BlockSpec((None, tq, LANES), lambda b, h, i: (b, i, h))]
    out_shape = [jax.ShapeDtypeStruct((bsz, n_q, B_V), BF16)]
    if lagged:
        out_specs.append(pl.BlockSpec((None, None, 1, tq), lambda b, h, i: (b, h, 0, i)))
        out_shape.append(jax.ShapeDtypeStruct((bsz, B_HEADS, 1, n_q), F32))
    in_specs = [pl.BlockSpec((4, HEAD_DIM), lambda b, h, i: (0, 0)),
                pl.BlockSpec((1, LANES), lambda b, h, i: (0, 0)),
                pl.BlockSpec((None, tq, LANES), lambda b, h, i: (b, i, h))]
    args = [lam_vec, subln_g, q]
    for k, v in kv_list:
        ns = k.shape[1]
        in_specs += [pl.BlockSpec((None, ns, LANES), lambda b, h, i: (b, 0, h))] * 2
        args += [k, v]
    return pl.pallas_call(
        functools.partial(_diff_attn_kernel, nseg=len(kv_list), lam_init=lam_init, kc=kc, lagged=lagged),
        grid=(bsz, B_HEADS, n_q // tq),
        in_specs=in_specs,
        out_specs=out_specs,
        out_shape=out_shape,
        scratch_shapes=[pltpu.VMEM((LANES + ONES_ROWS, sum(k.shape[1] for k, _ in kv_list)), BF16)],
        compiler_params=_cparams(("parallel", "parallel", "arbitrary")),
        name="diff_attn_%dseg%s" % (len(kv_list), "_lagged" if lagged else ""),
    )(*args)


FFN_HALO = 16
FFN_TF = 256


def _ffn_kernel(*refs, n_mix, tiles_per_seq, final_norm):
    a_refs = refs[:3 * n_mix]
    wo_refs = refs[3 * n_mix:4 * n_mix]
    (h_ref, hp_ref, hn_ref, g1_ref, sh_ref, sc_ref, gt_ref, ng_ref, wu_ref, wg_ref, cwu_ref, cwg_ref,
     wd_ref, fg_ref, o_ref, xn_ref, hu_a, hg_a, hu_b, hg_b, acc_ref) = refs[4 * n_mix:]
    i = pl.program_id(0)
    tm = h_ref.shape[0]
    hl = FFN_HALO
    nj = wu_ref.shape[0]

    y = None
    for m in range(n_mix):
        ap_ref, a_ref, an_ref = a_refs[3 * m + 1], a_refs[3 * m], a_refs[3 * m + 2]
        a_ext = jnp.concatenate([ap_ref[...], a_ref[...], an_ref[...]], axis=0)
        part = _dot(a_ext, wo_refs[m][...])
        y = part if y is None else y + part
    h1 = jnp.concatenate([hp_ref[...], h_ref[...], hn_ref[...]], axis=0) + g1_ref[...] * y
    o_ref[...] = h1[hl:hl + tm, :]

    pos = i % tiles_per_seq
    xn = _norm_mod(h1, ng_ref[...], sc_ref[...], sh_ref[...])
    xn_ref[hl:hl + tm, :] = xn[hl:hl + tm, :].astype(BF16)
    xn_ref[0:hl, :] = (xn[0:hl, :] * jnp.where(pos > 0, 1.0, 0.0)).astype(BF16)
    xn_ref[hl + tm:, :] = (xn[hl + tm:, :] * jnp.where(pos < tiles_per_seq - 1, 1.0, 0.0)).astype(BF16)
    acc_ref[...] = jnp.zeros_like(acc_ref)

    def up(j, hu_ref, hg_ref):
        xn = xn_ref[...]
        hu_ref[...] = _dot(xn, wu_ref[j])
        hg_ref[...] = _dot(xn, wg_ref[j])

    def conv(ref, cw):
        return (ref[hl - 1:hl - 1 + tm, :] * cw[0:1] + ref[hl:hl + tm, :] * cw[1:2]
                + ref[hl + 1:hl + 1 + tm, :] * cw[2:3] + cw[3:4])

    def act(j, hu_ref, hg_ref):
        u = conv(hu_ref, cwu_ref[j])
        gt = conv(hg_ref, cwg_ref[j])
        return (gt * _sigmoid(gt) * u).astype(BF16)

    up(0, hu_a, hg_a)

    def pair(jj, carry):
        j = 2 * jj
        up(j + 1, hu_b, hg_b)
        acc_ref[...] += _dot(act(j, hu_a, hg_a), wd_ref[j])
        up(j + 2, hu_a, hg_a)
        acc_ref[...] += _dot(act(j + 1, hu_b, hg_b), wd_ref[j + 1])
        return carry

    assert nj % 2 == 1
    lax.fori_loop(0, (nj - 1) // 2, pair, 0)
    acc_ref[...] += _dot(act(nj - 1, hu_a, hg_a), wd_ref[nj - 1])

    y = o_ref[...] + gt_ref[...] * acc_ref[...]
    if final_norm:
        ms = jnp.mean(y * y, axis=-1, keepdims=True)
        y = y * lax.rsqrt(ms + EPS) * fg_ref[...]
    o_ref[...] = y


def _ffn_weights(w_up, conv_w, conv_b, w_down, tf):
    d = w_up.shape[0]
    nj = D_FF // tf
    chunked = lambda w: w.reshape(w.shape[0], nj, tf).transpose(1, 0, 2)
    w_uu = chunked(w_up[:, :D_FF]).astype(BF16)
    w_ug = chunked(w_up[:, D_FF:]).astype(BF16)
    cw = jnp.concatenate([conv_w, conv_b[None, :]], axis=0)
    return (w_uu, w_ug, chunked(cw[:, :D_FF]), chunked(cw[:, D_FF:]), w_down.reshape(nj, tf, d).astype(BF16))


def _mixer_out_ffn(acts, w_outs, h2d, mod, norm_g, weights, final_g, *, tm, row_of, tiles_per_seq, final_norm):
    m, d = h2d.shape
    hl = FFN_HALO
    nhb = m // hl
    r = tm // hl
    w_uu, w_ug, cwu, cwg, w_dn = weights
    tf = w_uu.shape[2]
    resident = lambda a: pl.BlockSpec(a.shape, lambda i: (0,) * a.ndim, pipeline_mode=pl.Buffered(1))

    def tile_and_halos(width):
        return [pl.BlockSpec((tm, width), lambda i: (i, 0)),
                pl.BlockSpec((hl, width), lambda i: (jnp.maximum(i * r - 1, 0), 0)),
                pl.BlockSpec((hl, width), lambda i: (jnp.minimum((i + 1) * r, nhb - 1), 0))]

    in_specs, args = [], []
    for a in acts:
        in_specs += tile_and_halos(a.shape[1])
        args += [a, a, a]
    in_specs += [resident(w) for w in w_outs]
    args += list(w_outs)
    in_specs += tile_and_halos(d)
    in_specs += [_mod_spec(2, row_of), _mod_spec(3, row_of), _mod_spec(4, row_of), _mod_spec(5, row_of),
                 pl.BlockSpec((1, d), lambda i: (0, 0)),
                 resident(w_uu), resident(w_ug), resident(cwu), resident(cwg), resident(w_dn),
                 pl.BlockSpec((1, d), lambda i: (0, 0))]
    args += [h2d, h2d, h2d, mod, mod, mod, mod, norm_g, w_uu, w_ug, cwu, cwg, w_dn, final_g]
    return pl.pallas_call(
        functools.partial(_ffn_kernel, n_mix=len(acts), tiles_per_seq=tiles_per_seq, final_norm=final_norm),
        grid=(m // tm,),
        in_specs=in_specs,
        out_specs=pl.BlockSpec((tm, d), lambda i: (i, 0)),
        out_shape=jax.ShapeDtypeStruct((m, d), F32),
        scratch_shapes=[pltpu.VMEM((tm + 2 * hl, d), BF16)]
        + [pltpu.VMEM((tm + 2 * hl, tf), F32)] * 4
        + [pltpu.VMEM((tm, d), F32)],
        compiler_params=_cparams(("parallel",)),
        name="mixer_out_conv_ffn",
    )(*args)


def _gla_inproj_kernel(x_ref, sh_ref, sc_ref, g_ref, w_ref, w2_ref, gb_ref,
                       q_ref, k_ref, v_ref, sg_ref, laf_ref, lab_ref):
    xn = _norm_mod(x_ref[...], g_ref[...], sc_ref[...], sh_ref[...]).astype(BF16)
    q_ref[...] = (_dot(xn, w_ref[:, 0:C_QK]) * (C_DK ** -0.5)).astype(BF16)
    k_ref[...] = _dot(xn, w_ref[:, C_QK:2 * C_QK]).astype(BF16)
    for c in range(C_V // C_QK):
        lo = 2 * C_QK + c * C_QK
        v_ref[:, c * C_QK:(c + 1) * C_QK] = _dot(xn, w_ref[:, lo:lo + C_QK]).astype(BF16)
    for c in range(C_V // C_QK):
        lo = 2 * C_QK + C_V + c * C_QK
        gg = _dot(xn, w_ref[:, lo:lo + C_QK])
        sg_ref[:, c * C_QK:(c + 1) * C_QK] = (gg * _sigmoid(gg)).astype(BF16)
    lo = 2 * C_QK + 2 * C_V
    r = _dot(xn, w_ref[:, lo:lo + LANES]).astype(BF16)
    for dr, la_ref in enumerate((laf_ref, lab_ref)):
        z = _dot(r, w2_ref[:, dr * C_QK:(dr + 1) * C_QK]) + gb_ref[:, dr * C_QK:(dr + 1) * C_QK]
        la_ref[...] = (jnp.minimum(z, 0.0) - jnp.log(1.0 + jnp.exp(-jnp.abs(z)))) * (1.0 / C_GATE_NORM)


def _gla_inproj(x2d, mod, norm_g, w, w2, gb, *, tm, row_of):
    m, d = x2d.shape
    widths = (C_QK, C_QK, C_V, C_V, C_QK, C_QK)
    dts = (BF16, BF16, BF16, BF16, F32, F32)
    return pl.pallas_call(
        _gla_inproj_kernel,
        grid=(m // tm,),
        in_specs=[
            pl.BlockSpec((tm, d), lambda i: (i, 0)),
            _mod_spec(0, row_of), _mod_spec(1, row_of),
            pl.BlockSpec((1, d), lambda i: (0, 0)),
            pl.BlockSpec(w.shape, lambda i: (0, 0)),
            pl.BlockSpec(w2.shape, lambda i: (0, 0)),
            pl.BlockSpec(gb.shape, lambda i: (0, 0)),
        ],
        out_specs=[pl.BlockSpec((tm, wd), lambda i: (i, 0)) for wd in widths],
        out_shape=[jax.ShapeDtypeStruct((m, wd), dt) for wd, dt in zip(widths, dts)],
        compiler_params=_cparams(("parallel",)),
        name="gla_inproj",
    )(x2d, mod, mod, norm_g, w, w2, gb)


def _tri(n, reverse):
    r = lax.broadcasted_iota(jnp.int32, (n, n), 0)
    c = lax.broadcasted_iota(jnp.int32, (n, n), 1)
    return (c >= r) if reverse else (c <= r)


def _cumsum_rows(la, tri_bf):
    hi = la.astype(BF16)
    r1 = la - hi.astype(F32)
    mid = r1.astype(BF16)
    lo = (r1 - mid.astype(F32)).astype(BF16)
    return _dot(tri_bf, hi) + _dot(tri_bf, mid) + _dot(tri_bf, lo)


def _gla_ctx_state_kernel(k_ref, v_ref, laf_ref, lab_ref, sf_ref, sb_ref):
    n = k_ref.shape[0]
    for reverse, la_ref, s_ref in ((False, laf_ref, sf_ref), (True, lab_ref, sb_ref)):
        tri = jnp.where(_tri(n, reverse), 1.0, 0.0).astype(BF16)
        b = _cumsum_rows(la_ref[...], tri)
        b_end = b[0:1, :] if reverse else b[n - 1:n, :]
        kw = (k_ref[...].astype(F32) * jnp.exp(b_end - b)).astype(BF16)
        for h in range(C_HEADS):
            s_ref[h] = _dot_tn(v_ref[:, h * C_DV:(h + 1) * C_DV], kw[:, h * C_DK:(h + 1) * C_DK])


def _gla_ctx_state(kc, vc, lac_f, lac_b):
    bsz, n, _ = kc.shape
    s_shape = jax.ShapeDtypeStruct((bsz, C_HEADS, C_DV, C_DK), F32)
    s_spec = pl.BlockSpec((None, C_HEADS, C_DV, C_DK), lambda b: (b, 0, 0, 0))
    return pl.pallas_call(
        _gla_ctx_state_kernel,
        grid=(bsz,),
        in_specs=[pl.BlockSpec((None, n, C_QK), lambda b: (b, 0, 0)),
                  pl.BlockSpec((None, n, C_V), lambda b: (b, 0, 0)),
                  pl.BlockSpec((None, n, C_QK), lambda b: (b, 0, 0)),
                  pl.BlockSpec((None, n, C_QK), lambda b: (b, 0, 0))],
        out_specs=[s_spec, s_spec],
        out_shape=[s_shape, s_shape],
        compiler_params=_cparams(("parallel",)),
        name="gla_ctx_state",
    )(kc, vc, lac_f, lac_b)


def _gla_scan_kernel(q_ref, k_ref, v_ref, la_ref, s0_ref, *refs, reverse, final):
    if final:
        ob_ref, sg_ref, ng_ref, o_ref, st_ref = refs
    else:
        o_ref, st_ref = refs
    gidx = pl.program_id(1)

    @pl.when(gidx == 0)
    def _():
        st_ref[...] = s0_ref[...]

    gt = q_ref.shape[0]
    nchunk = gt // C_CHUNK
    c = C_CHUNK
    r = lax.broadcasted_iota(jnp.int32, (gt, gt), 0)
    s = lax.broadcasted_iota(jnp.int32, (gt, gt), 1)
    shift = c.bit_length() - 1
    same_chunk = (r >> shift) == (s >> shift)
    tri = same_chunk & ((s >= r) if reverse else (s <= r))
    tri_bf = jnp.where(tri, 1.0, 0.0).astype(BF16)

    def per_chunk_row(x, row):
        return jnp.concatenate([jnp.broadcast_to(x[ci * c + row:ci * c + row + 1, :], (c, x.shape[1]))
                                for ci in range(nchunk)], axis=0)

    b = _cumsum_rows(la_ref[...], tri_bf)
    b_mid = per_chunk_row(b, c // 2)
    b_end = per_chunk_row(b, 0 if reverse else c - 1)
    qf = q_ref[...].astype(F32)
    kf = k_ref[...].astype(F32)
    q_in = (qf * jnp.exp(b - b_mid)).astype(BF16)
    k_in = (kf * jnp.exp(b_mid - b)).astype(BF16)
    q_out = (qf * jnp.exp(b)).astype(BF16)
    k_out = (kf * jnp.exp(b_end - b)).astype(BF16)
    decay = jnp.exp(b_end)

    order = range(nchunk - 1, -1, -1) if reverse else range(nchunk)
    kcols = [slice(h * C_DK, (h + 1) * C_DK) for h in range(C_HEADS)]
    vcols_of = [slice(h * C_DV, (h + 1) * C_DV) for h in range(C_HEADS)]
    o_intra, upd = [], []
    for h in range(C_HEADS):
        vv = v_ref[:, vcols_of[h]]
        sc = jnp.where(tri, _dot_nt(q_in[:, kcols[h]], k_in[:, kcols[h]]), 0.0).astype(BF16)
        o_intra.append(_dot(sc, vv))
        upd.append({ci: _dot_tn(vv[ci * c:(ci + 1) * c, :], k_out[ci * c:(ci + 1) * c, kcols[h]])
                    for ci in order})
    st = [st_ref[h] for h in range(C_HEADS)]
    o_inter = [{} for _ in range(C_HEADS)]
    for ci in order:
        rows = slice(ci * c, (ci + 1) * c)
        for h in range(C_HEADS):
            o_inter[h][ci] = _dot_nt(q_out[rows, kcols[h]], st[h].astype(BF16))
            st[h] = st[h] * decay[ci * c:ci * c + 1, kcols[h]] + upd[h][ci]
    for h in range(C_HEADS):
        vcols = vcols_of[h]
        st_ref[h] = st[h]
        o = o_intra[h] + jnp.concatenate([o_inter[h][ci] for ci in range(nchunk)], axis=0)
        if final:
            o = o + ob_ref[:, vcols]
            ms = jnp.mean(o * o, axis=-1, keepdims=True)
            o = (o * lax.rsqrt(ms + EPS) * ng_ref[...]) * sg_ref[:, vcols].astype(F32)
            o_ref[:, vcols] = o.astype(BF16)
        else:
            o_ref[:, vcols] = o


def _gla_scan(q, k, v, la, s0, *, gt, reverse, o_other=None, sg=None, norm_g=None):
    bsz, n_tok, _ = q.shape
    ng = n_tok // gt
    final = o_other is not None
    gi = (lambda b, g: (b, ng - 1 - g, 0)) if reverse else (lambda b, g: (b, g, 0))
    in_specs = [pl.BlockSpec((None, gt, C_QK), gi), pl.BlockSpec((None, gt, C_QK), gi),
                pl.BlockSpec((None, gt, C_V), gi), pl.BlockSpec((None, gt, C_QK), gi),
                pl.BlockSpec((None, C_HEADS, C_DV, C_DK), lambda b, g: (b, 0, 0, 0))]
    args = [q, k, v, la, s0]
    if final:
        in_specs += [pl.BlockSpec((None, gt, C_V), gi), pl.BlockSpec((None, gt, C_V), gi),
                     pl.BlockSpec((1, C_DV), lambda b, g: (0, 0))]
        args += [o_other, sg, norm_g]
    return pl.pallas_call(
        functools.partial(_gla_scan_kernel, reverse=reverse, final=final),
        grid=(bsz, ng),
        in_specs=in_specs,
        out_specs=pl.BlockSpec((None, gt, C_V), gi),
        out_shape=jax.ShapeDtypeStruct((bsz, n_tok, C_V), BF16 if final else F32),
        scratch_shapes=[pltpu.VMEM((C_HEADS, C_DV, C_DK), F32)],
        compiler_params=_cparams(("parallel", "arbitrary")),
        name="gla_scan_fwd_final" if final else "gla_scan_bwd",
    )(*args)


def _pair_split(n_heads):
    base = np.concatenate([np.arange(0, HEAD_DIM, 2), np.arange(1, HEAD_DIM, 2)])
    return np.concatenate([h * HEAD_DIM + base for h in range(n_heads)])


_A_HEAD_ORDER = np.array([kv * A_GROUP + j for j in range(A_GROUP) for kv in range(A_KV_HEADS)])


def _attn_in_cols():
    aq = (_A_HEAD_ORDER[:, None] * HEAD_DIM + _pair_split(1)[None, :]).reshape(-1)
    o_ak, o_av, o_bq = A_Q, A_Q + A_KV, A_Q + 2 * A_KV
    o_bk, o_bv = o_bq + B_QK, o_bq + 2 * B_QK
    return np.concatenate([aq, o_bq + _pair_split(2 * B_HEADS), o_bk + _pair_split(2 * B_HEADS),
                           o_ak + _pair_split(A_KV_HEADS), o_av + np.arange(A_KV), o_bv + np.arange(B_V)])


def _attn_out_rows():
    oa = (_A_HEAD_ORDER[:, None] * HEAD_DIM + np.arange(HEAD_DIM)[None, :]).reshape(-1)
    return oa


def _rope_tables(n_tok):
    rows = n_tok // GRID_W
    row = jnp.repeat(jnp.arange(rows, dtype=F32), GRID_W)
    col = jnp.tile(jnp.arange(GRID_W, dtype=F32), rows)
    axis_dim = HEAD_DIM // 2
    inv_freq = ROPE_THETA ** (-jnp.arange(0, axis_dim, 2, dtype=F32) / axis_dim)
    ang = jnp.concatenate([row[:, None] * inv_freq, col[:, None] * inv_freq], axis=-1)
    cos, sin = jnp.cos(ang), jnp.sin(ang)
    cos_t = jnp.tile(cos, (1, LANES // (HEAD_DIM // 2)))
    sin_t = jnp.tile(jnp.concatenate([-sin, sin], axis=-1), (1, LANES // HEAD_DIM))
    return cos_t, sin_t


def _pick(n, pref):
    return pref if n % pref == 0 else n


def kernel(x, c, ctx, c_ctx, mod_w, mod_b, norm1_g, norm2_g, attn_w_in, attn_w_out, attn_sink, diff_lambda, diff_subln_g, gla_w_in, gla_gate_w1, gla_gate_w2, gla_gate_b, gla_norm_g, gla_w_out, ffn_w_up, ffn_conv_w, ffn_conv_b, ffn_w_down, final_norm_g):
    bsz, n_tok, d = x.shape
    n_ctx = ctx.shape[1]
    assert d == D_MODEL and bsz + 1 <= MOD_ROWS
    m_lat, m_ctx = bsz * n_tok, bsz * n_ctx

    c_rows = jnp.concatenate([c, c_ctx[None, :], jnp.zeros((MOD_ROWS - bsz - 1, d), F32)], axis=0)
    mod_all = _modulation(c_rows, mod_w, mod_b)
    cos_t, sin_t = _rope_tables(n_tok)

    tm = _pick(n_tok, 1024)
    tmc = _pick(n_ctx, 256)
    lat_tiles = n_tok // tm
    lat_row = lambda i: i // lat_tiles
    ctx_row = lambda i: bsz
    tm_ffn = _pick(n_tok, 1024)
    ffn_row = lambda i: i // (n_tok // tm_ffn)

    h = x.reshape(m_lat, d)
    hc = ctx.reshape(m_ctx, d)
    for layer in range(DEPTH):
        need_ctx = layer < DEPTH - 1
        last = layer == DEPTH - 1
        mod = mod_all[layer].reshape(MOD_ROWS, 6, 1, d)
        n1 = norm1_g[layer].reshape(1, d)
        n2 = norm2_g[layer].reshape(1, d)
        i = layer // 2
        if layer % 2 == 0:
            lam_init = 0.8 - 0.6 * math.exp(-B_LAMBDA_DECAY * layer)
            w_in = attn_w_in[i][:, _attn_in_cols()].astype(BF16)
            w_out = attn_w_out[i]
            w_oa = w_out[_attn_out_rows()].astype(BF16)
            w_ob = w_out[A_Q:].astype(BF16)
            sink = attn_sink[i]
            subln = diff_subln_g[i].reshape(1, LANES)
            aq, bq, bk, ak, av, bv = _attn_inproj(h, mod, n1, w_in, cos_t, sin_t, tm=tm, row_of=lat_row,
                                                  rope=True, tiles_per_seq=lat_tiles)
            caq, cbq, cbk, cak, cav, cbv = _attn_inproj(hc, mod, n1, w_in, cos_t, sin_t, tm=tmc, row_of=ctx_row,
                                                        rope=False, tiles_per_seq=1)
            r3 = lambda a, n: a.reshape(bsz, n, a.shape[-1])
            cak3, cav3, cbk3, cbv3 = r3(cak, n_ctx), r3(cav, n_ctx), r3(cbk, n_ctx), r3(cbv, n_ctx)
            oa = _gqa_window(sink, r3(aq, n_tok), r3(ak, n_tok), r3(av, n_tok), cak3, cav3)
            b_args = (diff_lambda[i], subln, r3(bq, n_tok), [(r3(bk, n_tok), r3(bv, n_tok)), (cbk3, cbv3)])
            b_kw = dict(tq=_pick(n_tok, 512), lam_init=lam_init)
            ob_lagged, over = _diff_attn(*b_args, lagged=True, **b_kw)
            ob = lax.cond(jnp.max(over) <= LAG_LIMIT, lambda: ob_lagged,
                          lambda: _diff_attn(*b_args, **b_kw)[0])
            mix, w_mix = [oa.reshape(m_lat, A_Q), ob.reshape(m_lat, B_V)], [w_oa, w_ob]
            if need_ctx:
                oca = _gqa_context(sink, r3(caq, n_ctx), cak3, cav3)
                ocb = _diff_attn(diff_lambda[i], subln, r3(cbq, n_ctx), [(cbk3, cbv3)],
                                 tq=_pick(n_ctx, 256), lam_init=lam_init)[0]
                mix_c = [oca.reshape(m_ctx, A_Q), ocb.reshape(m_ctx, B_V)]
        else:
            w1 = gla_gate_w1[i]
            pad = jnp.zeros((d, LANES - 2 * C_GATE_RANK), F32)
            w_in = jnp.concatenate([gla_w_in[i], w1[0], w1[1], pad], axis=1).astype(BF16)
            w2 = gla_gate_w2[i]
            w2bd = jnp.zeros((LANES, 2 * C_QK), F32)
            w2bd = w2bd.at[0:C_GATE_RANK, 0:C_QK].set(w2[0]).at[C_GATE_RANK:2 * C_GATE_RANK, C_QK:].set(w2[1])
            w2bd = w2bd.astype(BF16)
            gb = gla_gate_b[i].reshape(1, 2 * C_QK)
            ng = gla_norm_g[i].reshape(1, C_DV)
            q, k, v, sg, la_f, la_b = _gla_inproj(h, mod, n1, w_in, w2bd, gb, tm=tm, row_of=lat_row)
            qc, kc, vc, sgc, lac_f, lac_b = _gla_inproj(hc, mod, n1, w_in, w2bd, gb, tm=tmc, row_of=ctx_row)
            r3 = lambda a, n: a.reshape(bsz, n, a.shape[-1])
            s_f, s_b = _gla_ctx_state(r3(kc, n_ctx), r3(vc, n_ctx), r3(lac_f, n_ctx), r3(lac_b, n_ctx))
            gt = _pick(n_tok, 256)
            q3, k3, v3 = r3(q, n_tok), r3(k, n_tok), r3(v, n_tok)
            o_b = _gla_scan(q3, k3, v3, r3(la_b, n_tok), s_b, gt=gt, reverse=True)
            og = _gla_scan(q3, k3, v3, r3(la_f, n_tok), s_f, gt=gt, reverse=False,
                           o_other=o_b, sg=r3(sg, n_tok), norm_g=ng)
            mix, w_mix = [og.reshape(m_lat, C_V)], [gla_w_out[i].astype(BF16)]
            if need_ctx:
                z = jnp.zeros((bsz, C_HEADS, C_DV, C_DK), F32)
                qc3, kc3, vc3 = r3(qc, n_ctx), r3(kc, n_ctx), r3(vc, n_ctx)
                gtc = _pick(n_ctx, 256)
                oc_b = _gla_scan(qc3, kc3, vc3, r3(lac_b, n_ctx), z, gt=gtc, reverse=True)
                ogc = _gla_scan(qc3, kc3, vc3, r3(lac_f, n_ctx), z, gt=gtc, reverse=False,
                                o_other=oc_b, sg=r3(sgc, n_ctx), norm_g=ng)
                mix_c = [ogc.reshape(m_ctx, C_V)]
        ffn_w = _ffn_weights(ffn_w_up[layer], ffn_conv_w[layer], ffn_conv_b[layer], ffn_w_down[layer], FFN_TF)
        fg = final_norm_g.reshape(1, d)
        h = _mixer_out_ffn(mix, w_mix, h, mod, n2, ffn_w, fg, tm=tm_ffn, row_of=ffn_row,
                           tiles_per_seq=n_tok // tm_ffn, final_norm=last)
        if need_ctx:
            hc = _mixer_out_ffn(mix_c, w_mix, hc, mod, n2, ffn_w, fg, tm=tmc, row_of=ctx_row,
                                tiles_per_seq=n_ctx // tmc, final_norm=False)
    return h.reshape(bsz, n_tok, d)
```

```python
import functools
import math

import numpy as np
import jax
import jax.numpy as jnp
from jax import lax
from jax.experimental import pallas as pl
from jax.experimental.pallas import tpu as pltpu

F32 = jnp.float32
BF16 = jnp.bfloat16

D_MODEL = 1024
DEPTH = 2
GRID_W = 64
HEAD_DIM = 64
ROPE_THETA = 10000.0
EPS = 1e-6
BLOCK = 128
A_HEADS = 8
A_KV_HEADS = 2
A_GROUP = A_HEADS // A_KV_HEADS
B_HEADS = 4
B_LAMBDA_DECAY = 0.3
A_Q = A_HEADS * HEAD_DIM
A_KV = A_KV_HEADS * HEAD_DIM
B_QK = B_HEADS * 2 * HEAD_DIM
B_V = B_HEADS * 2 * HEAD_DIM
C_HEADS = 4
C_DK = D_MODEL // 2 // C_HEADS
C_DV = D_MODEL // C_HEADS
C_GATE_RANK = 16
C_GATE_NORM = 16.0
C_CHUNK = 64
C_QK = C_HEADS * C_DK
C_V = C_HEADS * C_DV
D_FF = 2816
LANES = 128
MOD_ROWS = 8
VMEM_LIMIT = 56 * 1024 * 1024


def _cparams(sem):
    return pltpu.CompilerParams(dimension_semantics=sem, vmem_limit_bytes=VMEM_LIMIT)


def _dot(a, b):
    return jnp.dot(a, b, preferred_element_type=F32)


def _dot_nt(a, b):
    return lax.dot_general(a, b, (((1,), (1,)), ((), ())), preferred_element_type=F32)


def _dot_tn(a, b):
    return lax.dot_general(a, b, (((0,), (0,)), ((), ())), preferred_element_type=F32)


def _sigmoid(x):
    return 1.0 / (1.0 + jnp.exp(-x))


def _norm_mod(x, g, sc, sh):
    ms = jnp.mean(x * x, axis=-1, keepdims=True)
    return (x * lax.rsqrt(ms + EPS) * g) * (1.0 + sc) + sh


def _mod_kernel(c_ref, w_ref, b_ref, o_ref):
    c = c_ref[...]
    s = (c * _sigmoid(c)).astype(BF16)
    o_ref[...] = _dot(s, w_ref[...].astype(BF16)) + b_ref[...]


def _modulation(c_rows, mod_w, mod_b):
    d = D_MODEL
    return pl.pallas_call(
        _mod_kernel,
        grid=(DEPTH, 6),
        in_specs=[
            pl.BlockSpec((MOD_ROWS, d), lambda l, n: (0, 0)),
            pl.BlockSpec((None, d, d), lambda l, n: (l, 0, n)),
            pl.BlockSpec((None, 1, d), lambda l, n: (l, 0, n)),
        ],
        out_specs=pl.BlockSpec((None, MOD_ROWS, d), lambda l, n: (l, 0, n)),
        out_shape=jax.ShapeDtypeStruct((DEPTH, MOD_ROWS, 6 * d), F32),
        compiler_params=_cparams(("parallel", "parallel")),
        name="modulation",
    )(c_rows, mod_w, mod_b.reshape(DEPTH, 1, 6 * d))


def _mod_spec(k, row_of):
    return pl.BlockSpec((None, None, 1, D_MODEL), lambda i, *_: (row_of(i), k, 0, 0))


LOG2E = math.log2(math.e)
_Q_SCALE = HEAD_DIM ** -0.5 * LOG2E
_ATTN_GROUPS = (("aq", A_Q, True, _Q_SCALE), ("bq", B_QK, True, _Q_SCALE), ("bk", B_QK, True, 1.0),
                ("ak", A_KV, True, 1.0), ("av", A_KV, False, 1.0), ("bv", B_V, False, 1.0))


def _rope_chunk(v, cos, sin, first):
    partner = jnp.where(first, pltpu.roll(v, 96, 1), pltpu.roll(v, 32, 1))
    return v * cos + partner * sin


def _attn_inproj_kernel(x_ref, sh_ref, sc_ref, g_ref, w_ref, cos_ref, sin_ref, *out_refs, rope):
    xn = _norm_mod(x_ref[...], g_ref[...], sc_ref[...], sh_ref[...]).astype(BF16)
    tm = xn.shape[0]
    if rope:
        cos = cos_ref[...]
        sin = sin_ref[...]
        lane = lax.broadcasted_iota(jnp.int32, (tm, LANES), 1)
        first = (lane % HEAD_DIM) < (HEAD_DIM // 2)
    lo = lax.broadcasted_iota(jnp.int32, (1, LANES), 1) < HEAD_DIM
    col = 0
    outs = iter(out_refs)
    for name, width, roped, scale in _ATTN_GROUPS:
        y = _dot(xn, w_ref[:, col:col + width])
        col += width
        o_refs = [next(outs), next(outs)] if name == "bq" else [next(outs)]
        for c in range(width // LANES):
            v = y[:, c * LANES:(c + 1) * LANES]
            if rope and roped:
                v = _rope_chunk(v, cos, sin, first)
            if scale != 1.0:
                v = v * scale
            if name == "bq":
                o_refs[0][:, c * LANES:(c + 1) * LANES] = jnp.where(lo, v, 0.0).astype(BF16)
                o_refs[1][:, c * LANES:(c + 1) * LANES] = jnp.where(lo, 0.0, v).astype(BF16)
            else:
                o_refs[0][:, c * LANES:(c + 1) * LANES] = v.astype(BF16)


def _attn_inproj(x2d, mod, norm_g, w, cos_t, sin_t, *, tm, row_of, rope, tiles_per_seq):
    m, d = x2d.shape
    n_all = w.shape[1]
    widths = [width for (name, width, _, _) in _ATTN_GROUPS for _ in range(2 if name == "bq" else 1)]
    out_shape = [jax.ShapeDtypeStruct((m, width), BF16) for width in widths]
    out_specs = [pl.BlockSpec((tm, width), lambda i: (i, 0)) for width in widths]
    return pl.pallas_call(
        functools.partial(_attn_inproj_kernel, rope=rope),
        grid=(m // tm,),
        in_specs=[
            pl.BlockSpec((tm, d), lambda i: (i, 0)),
            _mod_spec(0, row_of), _mod_spec(1, row_of),
            pl.BlockSpec((1, d), lambda i: (0, 0)),
            pl.BlockSpec((d, n_all), lambda i: (0, 0)),
            pl.BlockSpec((tm, LANES), lambda i: (i % tiles_per_seq, 0)),
            pl.BlockSpec((tm, LANES), lambda i: (i % tiles_per_seq, 0)),
        ],
        out_specs=out_specs,
        out_shape=out_shape,
        compiler_params=_cparams(("parallel",)),
        name="attn_inproj_rope" if rope else "attn_inproj_ctx",
    )(x2d, mod, mod, norm_g, w, cos_t, sin_t)


GQA_QB = 4


def _gqa_kernel(sink_ref, q_ref, *refs, window, nb):
    tq = BLOCK
    qb = q_ref.shape[0] // tq
    if window:
        k_blk = [r[...] for r in refs[0:qb + 2]]
        v_blk = [r[...] for r in refs[qb + 2:2 * qb + 4]]
        kx_ref, vx_ref, o_ref = refs[2 * qb + 4:]
    else:
        kx_ref, vx_ref, o_ref = refs
    lane = lax.broadcasted_iota(jnp.int32, (1, LANES), 1)
    lo = lane < HEAD_DIM
    half = (jnp.where(lo, 1.0, 0.0).astype(F32), jnp.where(lo, 0.0, 1.0).astype(F32))
    row = lax.broadcasted_iota(jnp.int32, (LANES, 1), 0)
    sinks = [jnp.concatenate([jnp.full((1, tq), sink_ref[kv * A_GROUP + j] * LOG2E, F32)
                              for j in range(A_GROUP)], axis=1) for kv in range(A_KV_HEADS)]
    work = []
    for u in range(qb):
        if window:
            keys = jnp.concatenate(k_blk[u:u + 3] + [kx_ref[...]], axis=0)
            vals = jnp.concatenate(v_blk[u:u + 3] + [vx_ref[...]], axis=0)
        else:
            keys, vals = kx_ref[...], vx_ref[...]
        qf = [q_ref[u * tq:(u + 1) * tq, j * LANES:(j + 1) * LANES].astype(F32) for j in range(A_GROUP)]
        scs = [_dot_nt(keys, jnp.concatenate([(q * half[kv]).astype(BF16) for q in qf], axis=0))
               for kv in range(A_KV_HEADS)]
        work.append((vals, scs))
    for u, (vals, scs) in enumerate(work):
        if window:
            n = pl.program_id(1) * qb + u
            s = lax.broadcasted_iota(jnp.int32, (BLOCK, tq), 0)
            t = lax.broadcasted_iota(jnp.int32, (BLOCK, tq), 1)
            has_prev = jnp.where(n > 0, 1, 0)
            has_next = jnp.where(n < nb - 1, 1, 0)
            lower = t * has_prev + BLOCK * (1 - has_prev)
            upper = (t + 1) * has_next - 1
            ninf = jnp.float32(-jnp.inf)
            bias_prev = jnp.concatenate([jnp.where(s < lower, ninf, 0.0)] * A_GROUP, axis=1)
            bias_next = jnp.concatenate([jnp.where(s > upper, ninf, 0.0)] * A_GROUP, axis=1)
        outs = []
        for kv in range(A_KV_HEADS):
            sc = scs[kv]
            if window:
                sc = jnp.concatenate([sc[0:BLOCK] + bias_prev, sc[BLOCK:2 * BLOCK],
                                      sc[2 * BLOCK:3 * BLOCK] + bias_next, sc[3 * BLOCK:]], axis=0)
            sk = sinks[kv]
            mx = jnp.maximum(jnp.max(sc, axis=0, keepdims=True), sk)
            p = jnp.exp2(sc - mx)
            den = jnp.sum(p, axis=0, keepdims=True) + jnp.exp2(sk - mx)
            pn = (p * (1.0 / den)).astype(BF16)
            outs.append(_dot_tn(vals, pn))
        o_t = jnp.where(row < HEAD_DIM, outs[0], outs[1])
        for j in range(A_GROUP):
            o_ref[u * tq:(u + 1) * tq, j * LANES:(j + 1) * LANES] = o_t[:, j * tq:(j + 1) * tq].T.astype(BF16)


def _gqa_window(sink, aq, ak, av, cak, cav):
    bsz, n_tok, _ = aq.shape
    n_ctx = cak.shape[1]
    nb = n_tok // BLOCK
    qb = GQA_QB
    assert nb % qb == 0
    kv_blocks = [pl.BlockSpec((None, BLOCK, A_KV),
                              lambda b, n, off=off: (b, jnp.clip(n * qb + off, 0, nb - 1), 0))
                 for off in range(-1, qb + 1)]
    kv_ctx = pl.BlockSpec((None, n_ctx, A_KV), lambda b, n: (b, 0, 0))
    return pl.pallas_call(
        functools.partial(_gqa_kernel, window=True, nb=nb),
        grid=(bsz, nb // qb),
        in_specs=[pl.BlockSpec(memory_space=pltpu.SMEM),
                  pl.BlockSpec((None, qb * BLOCK, A_Q), lambda b, n: (b, n, 0))]
        + kv_blocks + kv_blocks + [kv_ctx, kv_ctx],
        out_specs=pl.BlockSpec((None, qb * BLOCK, A_Q), lambda b, n: (b, n, 0)),
        out_shape=jax.ShapeDtypeStruct((bsz, n_tok, A_Q), BF16),
        compiler_params=_cparams(("parallel", "parallel")),
        name="gqa_window",
    )(sink, aq, *([ak] * (qb + 2)), *([av] * (qb + 2)), cak, cav)


def _gqa_context(sink, caq, cak, cav):
    bsz, n_ctx, _ = caq.shape
    nb = n_ctx // BLOCK
    kv_ctx = pl.BlockSpec((None, n_ctx, A_KV), lambda b, n: (b, 0, 0))
    return pl.pallas_call(
        functools.partial(_gqa_kernel, window=False, nb=nb),
        grid=(bsz, nb),
        in_specs=[pl.BlockSpec(memory_space=pltpu.SMEM),
                  pl.BlockSpec((None, BLOCK, A_Q), lambda b, n: (b, n, 0)),
                  kv_ctx, kv_ctx],
        out_specs=pl.BlockSpec((None, BLOCK, A_Q), lambda b, n: (b, n, 0)),
        out_shape=jax.ShapeDtypeStruct((bsz, n_ctx, A_Q), BF16),
        compiler_params=_cparams(("parallel", "parallel")),
        name="gqa_context",
    )(sink, caq, cak, cav)


LAG_LIMIT = 12.0
XPOSE_ROWS = 512
ONES_ROWS = 16


def _diff_attn_kernel(lam_ref, g_ref, q0_ref, q1_ref, *refs, nseg, lam_init, kc, lagged):
    k_refs = refs[0:2 * nseg:2]
    v_refs = refs[1:2 * nseg:2]
    if lagged:
        o_ref, ex_ref, vt_ref = refs[2 * nseg:]
    else:
        o_ref, vt_ref = refs[2 * nseg:]
    lv = lam_ref[...]
    lam = (jnp.exp(jnp.sum(lv[0:1] * lv[1:2], axis=-1, keepdims=True))
           - jnp.exp(jnp.sum(lv[2:3] * lv[3:4], axis=-1, keepdims=True)) + lam_init)
    q0 = q0_ref[...]
    q1 = q1_ref[...]
    @pl.when(pl.program_id(2) == 0)
    def _():
        off = 0
        for v_ref in v_refs:
            ns = v_ref.shape[0]
            for c0 in range(0, ns, XPOSE_ROWS):
                n = min(XPOSE_ROWS, ns - c0)
                vt_ref[0:LANES, off + c0:off + c0 + n] = v_ref[c0:c0 + n, :].astype(F32).T.astype(BF16)
            off += ns
        vt_ref[LANES:, :] = jnp.ones((vt_ref.shape[0] - LANES, vt_ref.shape[1]), BF16)

    chunks = []
    off = 0
    for k_ref in k_refs:
        ns = k_ref.shape[0]
        step = min(kc, ns)
        chunks += [(k_ref, c0, step, off + c0) for c0 in range(0, ns, step)]
        off += ns

    qms = (q0, q1)

    def scores(mi, ci):
        k_ref, c0, step, _ = chunks[ci]
        return _dot_nt(k_ref[c0:c0 + step, :], qms[mi])

    run_max = [None, None]
    accs = [None, None]
    excess = [None, None]

    def probs(mi, sc):
        mc = jnp.max(sc, axis=0, keepdims=True)
        if run_max[mi] is None:
            run_max[mi], alpha = (mc, mc), None
        else:
            ref_old, best = run_max[mi]
            new_best = jnp.maximum(best, mc)
            ref = best if lagged else new_best
            alpha = jnp.exp2(ref_old - ref)
            if lagged:
                over = mc - ref
                excess[mi] = over if excess[mi] is None else jnp.maximum(excess[mi], over)
            run_max[mi] = (ref, new_best)
        return jnp.exp2((sc - run_max[mi][0]).astype(BF16)), alpha

    def accumulate(mi, ci, pb, alpha):
        _, _, step, g0 = chunks[ci]
        pv = _dot(vt_ref[:, g0:g0 + step], pb)
        accs[mi] = pv if alpha is None else accs[mi] * alpha + pv

    n_ch = len(chunks)
    sc_q = {ci: [scores(mi, ci) for mi in range(2)] for ci in range(min(2, n_ch))}
    pb_q = {0: [probs(mi, sc_q[0][mi]) for mi in range(2)]}
    for ci in range(n_ch):
        if ci + 2 < n_ch:
            sc_q[ci + 2] = [scores(mi, ci + 2) for mi in range(2)]
        if ci + 1 < n_ch:
            sc_pair = sc_q.pop(ci + 1)
            pb_q[ci + 1] = [probs(mi, sc_pair[mi]) for mi in range(2)]
        for mi, (pb, alpha) in enumerate(pb_q.pop(ci)):
            accumulate(mi, ci, pb, alpha)
    r0 = 1.0 / accs[0][LANES:LANES + 1, :]
    r1 = lam / accs[1][LANES:LANES + 1, :]
    o = (accs[0][0:LANES, :] * r0 - accs[1][0:LANES, :] * r1).T
    ms = jnp.mean(o * o, axis=-1, keepdims=True)
    o_ref[...] = ((o * lax.rsqrt(ms + EPS) * g_ref[...]) * (1.0 - lam_init)).astype(BF16)
    if lagged:
        ex_ref[...] = jnp.maximum(excess[0], excess[1])


def _diff_attn(lam_vec, subln_g, q0, q1, kv_list, *, tq, lam_init, kc=256, lagged=False):
    bsz, n_q, _ = q0.shape
    out_specs = [pl.BlockSpec((None, tq, LANES), lambda b, h, i: (b, i, h))]
    out_shape = [jax.ShapeDtypeStruct((bsz, n_q, B_V), BF16)]
    if lagged:
        out_specs.append(pl.BlockSpec((None, None, 1, tq), lambda b, h, i: (b, h, 0, i)))
        out_shape.append(jax.ShapeDtypeStruct((bsz, B_HEADS, 1, n_q), F32))
    in_specs = [pl.BlockSpec((4, HEAD_DIM), lambda b, h, i: (0, 0)),
                pl.BlockSpec((1, LANES), lambda b, h, i: (0, 0)),
                pl.BlockSpec((None, tq, LANES), lambda b, h, i: (b, i, h)),
                pl.BlockSpec((None, tq, LANES), lambda b, h, i: (b, i, h))]
    args = [lam_vec, subln_g, q0, q1]
    for k, v in kv_list:
        ns = k.shape[1]
        in_specs += [pl.BlockSpec((None, ns, LANES), lambda b, h, i: (b, 0, h))] * 2
        args += [k, v]
    return pl.pallas_call(
        functools.partial(_diff_attn_kernel, nseg=len(kv_list), lam_init=lam_init, kc=kc, lagged=lagged),
        grid=(bsz, B_HEADS, n_q // tq),
        in_specs=in_specs,
        out_specs=out_specs,
        out_shape=out_shape,
        scratch_shapes=[pltpu.VMEM((LANES + ONES_ROWS, sum(k.shape[1] for k, _ in kv_list)), BF16)],
        compiler_params=_cparams(("parallel", "parallel", "arbitrary")),
        name="diff_attn_%dseg%s" % (len(kv_list), "_lagged" if lagged else ""),
    )(*args)


FFN_HALO = 16
FFN_TF = 256


def _ffn_kernel(*refs, n_mix, tiles_per_seq, final_norm):
    a_refs = refs[:3 * n_mix]
    wo_refs = refs[3 * n_mix:4 * n_mix]
    (h_ref, hp_ref, hn_ref, g1_ref, sh_ref, sc_ref, gt_ref, ng_ref, wu_ref, wg_ref, cwu_ref, cwg_ref,
     wd_ref, fg_ref, o_ref, xn_ref, hu_a, hg_a, hu_b, hg_b, acc_ref) = refs[4 * n_mix:]
    i = pl.program_id(0)
    tm = h_ref.shape[0]
    hl = FFN_HALO
    nj, tf = wd_ref.shape[0], wd_ref.shape[1]

    y = None
    for m in range(n_mix):
        ap_ref, a_ref, an_ref = a_refs[3 * m + 1], a_refs[3 * m], a_refs[3 * m + 2]
        a_ext = jnp.concatenate([ap_ref[...], a_ref[...], an_ref[...]], axis=0)
        part = _dot(a_ext, wo_refs[m][...])
        y = part if y is None else y + part
    h1 = jnp.concatenate([hp_ref[...], h_ref[...], hn_ref[...]], axis=0) + g1_ref[...] * y
    o_ref[...] = h1[hl:hl + tm, :]

    pos = i % tiles_per_seq
    xn = _norm_mod(h1, ng_ref[...], sc_ref[...], sh_ref[...])
    xn_ref[hl:hl + tm, :] = xn[hl:hl + tm, :].astype(BF16)
    xn_ref[0:hl, :] = (xn[0:hl, :] * jnp.where(pos > 0, 1.0, 0.0)).astype(BF16)
    xn_ref[hl + tm:, :] = (xn[hl + tm:, :] * jnp.where(pos < tiles_per_seq - 1, 1.0, 0.0)).astype(BF16)
    acc_ref[...] = jnp.zeros_like(acc_ref)

    def up(j, hu_ref, hg_ref):
        xn = xn_ref[...]
        cols = pl.ds(pl.multiple_of(j * tf, tf), tf)
        hu_ref[...] = _dot(xn, wu_ref[:, cols])
        hg_ref[...] = _dot(xn, wg_ref[:, cols])

    def conv(ref, cw):
        return (ref[hl - 1:hl - 1 + tm, :] * cw[0:1] + ref[hl:hl + tm, :] * cw[1:2]
                + ref[hl + 1:hl + 1 + tm, :] * cw[2:3] + cw[3:4])

    def act(j, hu_ref, hg_ref):
        u = conv(hu_ref, cwu_ref[j])
        gt = conv(hg_ref, cwg_ref[j])
        return (gt * _sigmoid(gt) * u).astype(BF16)

    up(0, hu_a, hg_a)

    def pair(jj, carry):
        j = 2 * jj
        up(j + 1, hu_b, hg_b)
        acc_ref[...] += _dot(act(j, hu_a, hg_a), wd_ref[j])
        up(j + 2, hu_a, hg_a)
        acc_ref[...] += _dot(act(j + 1, hu_b, hg_b), wd_ref[j + 1])
        return carry

    assert nj % 2 == 1
    lax.fori_loop(0, (nj - 1) // 2, pair, 0)
    acc_ref[...] += _dot(act(nj - 1, hu_a, hg_a), wd_ref[nj - 1])

    y = o_ref[...] + gt_ref[...] * acc_ref[...]
    if final_norm:
        ms = jnp.mean(y * y, axis=-1, keepdims=True)
        y = y * lax.rsqrt(ms + EPS) * fg_ref[...]
    o_ref[...] = y


def _ffn_weights(w_up, conv_w, conv_b, w_down, tf):
    d = w_up.shape[0]
    nj = D_FF // tf
    chunked = lambda w: w.reshape(w.shape[0], nj, tf).transpose(1, 0, 2)
    w_uu = w_up[:, :D_FF].astype(BF16)
    w_ug = w_up[:, D_FF:].astype(BF16)
    cw = jnp.concatenate([conv_w, conv_b[None, :]], axis=0)
    return (w_uu, w_ug, chunked(cw[:, :D_FF]), chunked(cw[:, D_FF:]), w_down.reshape(nj, tf, d).astype(BF16))


def _mixer_out_ffn(acts, w_outs, h2d, mod, norm_g, weights, final_g, *, tm, row_of, tiles_per_seq, final_norm):
    m, d = h2d.shape
    hl = FFN_HALO
    nhb = m // hl
    r = tm // hl
    w_uu, w_ug, cwu, cwg, w_dn = weights
    tf = w_dn.shape[1]
    resident = lambda a: pl.BlockSpec(a.shape, lambda i: (0,) * a.ndim, pipeline_mode=pl.Buffered(1))

    def tile_and_halos(width):
        return [pl.BlockSpec((tm, width), lambda i: (i, 0)),
                pl.BlockSpec((hl, width), lambda i: (jnp.maximum(i * r - 1, 0), 0)),
                pl.BlockSpec((hl, width), lambda i: (jnp.minimum((i + 1) * r, nhb - 1), 0))]

    in_specs, args = [], []
    for a in acts:
        in_specs += tile_and_halos(a.shape[1])
        args += [a, a, a]
    in_specs += [resident(w) for w in w_outs]
    args += list(w_outs)
    in_specs += tile_and_halos(d)
    in_specs += [_mod_spec(2, row_of), _mod_spec(3, row_of), _mod_spec(4, row_of), _mod_spec(5, row_of),
                 pl.BlockSpec((1, d), lambda i: (0, 0)),
                 resident(w_uu), resident(w_ug), resident(cwu), resident(cwg), resident(w_dn),
                 pl.BlockSpec((1, d), lambda i: (0, 0))]
    args += [h2d, h2d, h2d, mod, mod, mod, mod, norm_g, w_uu, w_ug, cwu, cwg, w_dn, final_g]
    return pl.pallas_call(
        functools.partial(_ffn_kernel, n_mix=len(acts), tiles_per_seq=tiles_per_seq, final_norm=final_norm),
        grid=(m // tm,),
        in_specs=in_specs,
        out_specs=pl.BlockSpec((tm, d), lambda i: (i, 0)),
        out_shape=jax.ShapeDtypeStruct((m, d), F32),
        scratch_shapes=[pltpu.VMEM((tm + 2 * hl, d), BF16)]
        + [pltpu.VMEM((tm + 2 * hl, tf), F32)] * 4
        + [pltpu.VMEM((tm, d), F32)],
        compiler_params=_cparams(("parallel",)),
        name="mixer_out_conv_ffn",
    )(*args)


def _gla_inproj_kernel(x_ref, sh_ref, sc_ref, g_ref, w_ref, w2_ref, gb_ref,
                       q_ref, k_ref, v_ref, sg_ref, laf_ref, lab_ref):
    xn = _norm_mod(x_ref[...], g_ref[...], sc_ref[...], sh_ref[...]).astype(BF16)
    q_ref[...] = (_dot(xn, w_ref[:, 0:C_QK]) * (C_DK ** -0.5)).astype(BF16)
    k_ref[...] = _dot(xn, w_ref[:, C_QK:2 * C_QK]).astype(BF16)
    for c in range(C_V // C_QK):
        lo = 2 * C_QK + c * C_QK
        v_ref[:, c * C_QK:(c + 1) * C_QK] = _dot(xn, w_ref[:, lo:lo + C_QK]).astype(BF16)
    for c in range(C_V // C_QK):
        lo = 2 * C_QK + C_V + c * C_QK
        gg = _dot(xn, w_ref[:, lo:lo + C_QK])
        sg_ref[:, c * C_QK:(c + 1) * C_QK] = (gg * _sigmoid(gg)).astype(BF16)
    lo = 2 * C_QK + 2 * C_V
    r = _dot(xn, w_ref[:, lo:lo + LANES]).astype(BF16)
    for dr, la_ref in enumerate((laf_ref, lab_ref)):
        z = _dot(r, w2_ref[:, dr * C_QK:(dr + 1) * C_QK]) + gb_ref[:, dr * C_QK:(dr + 1) * C_QK]
        la_ref[...] = (jnp.minimum(z, 0.0) - jnp.log(1.0 + jnp.exp(-jnp.abs(z)))) * (1.0 / C_GATE_NORM)


def _gla_inproj(x2d, mod, norm_g, w, w2, gb, *, tm, row_of):
    m, d = x2d.shape
    widths = (C_QK, C_QK, C_V, C_V, C_QK, C_QK)
    dts = (BF16, BF16, BF16, BF16, F32, F32)
    return pl.pallas_call(
        _gla_inproj_kernel,
        grid=(m // tm,),
        in_specs=[
            pl.BlockSpec((tm, d), lambda i: (i, 0)),
            _mod_spec(0, row_of), _mod_spec(1, row_of),
            pl.BlockSpec((1, d), lambda i: (0, 0)),
            pl.BlockSpec(w.shape, lambda i: (0, 0)),
            pl.BlockSpec(w2.shape, lambda i: (0, 0)),
            pl.BlockSpec(gb.shape, lambda i: (0, 0)),
        ],
        out_specs=[pl.BlockSpec((tm, wd), lambda i: (i, 0)) for wd in widths],
        out_shape=[jax.ShapeDtypeStruct((m, wd), dt) for wd, dt in zip(widths, dts)],
        compiler_params=_cparams(("parallel",)),
        name="gla_inproj",
    )(x2d, mod, mod, norm_g, w, w2, gb)


def _tri(n, reverse):
    r = lax.broadcasted_iota(jnp.int32, (n, n), 0)
    c = lax.broadcasted_iota(jnp.int32, (n, n), 1)
    return (c >= r) if reverse else (c <= r)


def _cumsum_rows(la, tri_bf):
    hi = la.astype(BF16)
    r1 = la - hi.astype(F32)
    mid = r1.astype(BF16)
    lo = (r1 - mid.astype(F32)).astype(BF16)
    return _dot(tri_bf, hi) + _dot(tri_bf, mid) + _dot(tri_bf, lo)


def _gla_ctx_state_kernel(k_ref, v_ref, laf_ref, lab_ref, sf_ref, sb_ref):
    n = k_ref.shape[0]
    for reverse, la_ref, s_ref in ((False, laf_ref, sf_ref), (True, lab_ref, sb_ref)):
        tri = jnp.where(_tri(n, reverse), 1.0, 0.0).astype(BF16)
        b = _cumsum_rows(la_ref[...], tri)
        b_end = b[0:1, :] if reverse else b[n - 1:n, :]
        kw = (k_ref[...].astype(F32) * jnp.exp(b_end - b)).astype(BF16)
        for h in range(C_HEADS):
            s_ref[h] = _dot_tn(v_ref[:, h * C_DV:(h + 1) * C_DV], kw[:, h * C_DK:(h + 1) * C_DK])


def _gla_ctx_state(kc, vc, lac_f, lac_b):
    bsz, n, _ = kc.shape
    s_shape = jax.ShapeDtypeStruct((bsz, C_HEADS, C_DV, C_DK), F32)
    s_spec = pl.BlockSpec((None, C_HEADS, C_DV, C_DK), lambda b: (b, 0, 0, 0))
    return pl.pallas_call(
        _gla_ctx_state_kernel,
        grid=(bsz,),
        in_specs=[pl.BlockSpec((None, n, C_QK), lambda b: (b, 0, 0)),
                  pl.BlockSpec((None, n, C_V), lambda b: (b, 0, 0)),
                  pl.BlockSpec((None, n, C_QK), lambda b: (b, 0, 0)),
                  pl.BlockSpec((None, n, C_QK), lambda b: (b, 0, 0))],
        out_specs=[s_spec, s_spec],
        out_shape=[s_shape, s_shape],
        compiler_params=_cparams(("parallel",)),
        name="gla_ctx_state",
    )(kc, vc, lac_f, lac_b)


GLA_BATCH_BLOCK = 2


def _gla_scan_kernel(q_ref, k_ref, v_ref, la_ref, s0_ref, *refs, reverse, final):
    if final:
        ob_ref, sg_ref, ng_ref, o_ref, st_ref = refs
    else:
        o_ref, st_ref = refs
    gidx = pl.program_id(1)

    @pl.when(gidx == 0)
    def _():
        st_ref[...] = s0_ref[...]

    nbb, gt = q_ref.shape[0], q_ref.shape[1]
    nchunk = gt // C_CHUNK
    c = C_CHUNK
    r = lax.broadcasted_iota(jnp.int32, (gt, gt), 0)
    s = lax.broadcasted_iota(jnp.int32, (gt, gt), 1)
    shift = c.bit_length() - 1
    same_chunk = (r >> shift) == (s >> shift)
    tri = same_chunk & ((s >= r) if reverse else (s <= r))
    tri_bf = jnp.where(tri, 1.0, 0.0).astype(BF16)

    def per_chunk_row(x, row):
        return jnp.concatenate([jnp.broadcast_to(x[ci * c + row:ci * c + row + 1, :], (c, x.shape[1]))
                                for ci in range(nchunk)], axis=0)

    order = range(nchunk - 1, -1, -1) if reverse else range(nchunk)
    kcols = [slice(h * C_DK, (h + 1) * C_DK) for h in range(C_HEADS)]
    vcols_of = [slice(h * C_DV, (h + 1) * C_DV) for h in range(C_HEADS)]
    streams = [(bi, h) for bi in range(nbb) for h in range(C_HEADS)]
    q_out, k_out, decay, o_intra, upd = {}, {}, {}, {}, {}
    for bi in range(nbb):
        b = _cumsum_rows(la_ref[bi], tri_bf)
        b_mid = per_chunk_row(b, c // 2)
        b_end = per_chunk_row(b, 0 if reverse else c - 1)
        qf = q_ref[bi].astype(F32)
        kf = k_ref[bi].astype(F32)
        q_in = (qf * jnp.exp(b - b_mid)).astype(BF16)
        k_in = (kf * jnp.exp(b_mid - b)).astype(BF16)
        q_out[bi] = (qf * jnp.exp(b)).astype(BF16)
        k_out[bi] = (kf * jnp.exp(b_end - b)).astype(BF16)
        decay[bi] = jnp.exp(b_end)
        for h in range(C_HEADS):
            vv = v_ref[bi, :, vcols_of[h]]
            sc = jnp.where(tri, _dot_nt(q_in[:, kcols[h]], k_in[:, kcols[h]]), 0.0).astype(BF16)
            o_intra[bi, h] = _dot(sc, vv)
            upd[bi, h] = {ci: _dot_tn(vv[ci * c:(ci + 1) * c, :], k_out[bi][ci * c:(ci + 1) * c, kcols[h]])
                          for ci in order}
    st = {s_: st_ref[s_[0], s_[1]] for s_ in streams}
    o_inter = {s_: {} for s_ in streams}
    for ci in order:
        rows = slice(ci * c, (ci + 1) * c)
        for bi, h in streams:
            o_inter[bi, h][ci] = _dot_nt(q_out[bi][rows, kcols[h]], st[bi, h].astype(BF16))
            st[bi, h] = st[bi, h] * decay[bi][ci * c:ci * c + 1, kcols[h]] + upd[bi, h][ci]
    for bi, h in streams:
        vcols = vcols_of[h]
        st_ref[bi, h] = st[bi, h]
        o = o_intra[bi, h] + jnp.concatenate([o_inter[bi, h][ci] for ci in range(nchunk)], axis=0)
        if final:
            o = o + ob_ref[bi, :, vcols]
            ms = jnp.mean(o * o, axis=-1, keepdims=True)
            o = (o * lax.rsqrt(ms + EPS) * ng_ref[...]) * sg_ref[bi, :, vcols].astype(F32)
            o_ref[bi, :, vcols] = o.astype(BF16)
        else:
            o_ref[bi, :, vcols] = o


def _gla_scan(q, k, v, la, s0, *, gt, reverse, o_other=None, sg=None, norm_g=None):
    bsz, n_tok, _ = q.shape
    ng = n_tok // gt
    final = o_other is not None
    nbb = GLA_BATCH_BLOCK if bsz % GLA_BATCH_BLOCK == 0 else 1
    gi = (lambda b, g: (b, ng - 1 - g, 0)) if reverse else (lambda b, g: (b, g, 0))
    in_specs = [pl.BlockSpec((nbb, gt, C_QK), gi), pl.BlockSpec((nbb, gt, C_QK), gi),
                pl.BlockSpec((nbb, gt, C_V), gi), pl.BlockSpec((nbb, gt, C_QK), gi),
                pl.BlockSpec((nbb, C_HEADS, C_DV, C_DK), lambda b, g: (b, 0, 0, 0))]
    args = [q, k, v, la, s0]
    if final:
        in_specs += [pl.BlockSpec((nbb, gt, C_V), gi), pl.BlockSpec((nbb, gt, C_V), gi),
                     pl.BlockSpec((1, C_DV), lambda b, g: (0, 0))]
        args += [o_other, sg, norm_g]
    return pl.pallas_call(
        functools.partial(_gla_scan_kernel, reverse=reverse, final=final),
        grid=(bsz // nbb, ng),
        in_specs=in_specs,
        out_specs=pl.BlockSpec((nbb, gt, C_V), gi),
        out_shape=jax.ShapeDtypeStruct((bsz, n_tok, C_V), BF16 if final else F32),
        scratch_shapes=[pltpu.VMEM((nbb, C_HEADS, C_DV, C_DK), F32)],
        compiler_params=_cparams(("parallel", "arbitrary")),
        name="gla_scan_fwd_final" if final else "gla_scan_bwd",
    )(*args)


def _pair_split(n_heads):
    base = np.concatenate([np.arange(0, HEAD_DIM, 2), np.arange(1, HEAD_DIM, 2)])
    return np.concatenate([h * HEAD_DIM + base for h in range(n_heads)])


_A_HEAD_ORDER = np.array([kv * A_GROUP + j for j in range(A_GROUP) for kv in range(A_KV_HEADS)])


def _attn_in_cols():
    aq = (_A_HEAD_ORDER[:, None] * HEAD_DIM + _pair_split(1)[None, :]).reshape(-1)
    o_ak, o_av, o_bq = A_Q, A_Q + A_KV, A_Q + 2 * A_KV
    o_bk, o_bv = o_bq + B_QK, o_bq + 2 * B_QK
    return np.concatenate([aq, o_bq + _pair_split(2 * B_HEADS), o_bk + _pair_split(2 * B_HEADS),
                           o_ak + _pair_split(A_KV_HEADS), o_av + np.arange(A_KV), o_bv + np.arange(B_V)])


def _attn_out_rows():
    oa = (_A_HEAD_ORDER[:, None] * HEAD_DIM + np.arange(HEAD_DIM)[None, :]).reshape(-1)
    return oa


def _rope_tables(n_tok):
    rows = n_tok // GRID_W
    row = jnp.repeat(jnp.arange(rows, dtype=F32), GRID_W)
    col = jnp.tile(jnp.arange(GRID_W, dtype=F32), rows)
    axis_dim = HEAD_DIM // 2
    inv_freq = ROPE_THETA ** (-jnp.arange(0, axis_dim, 2, dtype=F32) / axis_dim)
    ang = jnp.concatenate([row[:, None] * inv_freq, col[:, None] * inv_freq], axis=-1)
    cos, sin = jnp.cos(ang), jnp.sin(ang)
    cos_t = jnp.tile(cos, (1, LANES // (HEAD_DIM // 2)))
    sin_t = jnp.tile(jnp.concatenate([-sin, sin], axis=-1), (1, LANES // HEAD_DIM))
    return cos_t, sin_t


def _pick(n, pref):
    return pref if n % pref == 0 else n


def kernel(x, c, ctx, c_ctx, mod_w, mod_b, norm1_g, norm2_g, attn_w_in, attn_w_out, attn_sink, diff_lambda, diff_subln_g, gla_w_in, gla_gate_w1, gla_gate_w2, gla_gate_b, gla_norm_g, gla_w_out, ffn_w_up, ffn_conv_w, ffn_conv_b, ffn_w_down, final_norm_g):
    bsz, n_tok, d = x.shape
    n_ctx = ctx.shape[1]
    assert d == D_MODEL and bsz + 1 <= MOD_ROWS
    m_lat, m_ctx = bsz * n_tok, bsz * n_ctx

    c_rows = jnp.concatenate([c, c_ctx[None, :], jnp.zeros((MOD_ROWS - bsz - 1, d), F32)], axis=0)
    mod_all = _modulation(c_rows, mod_w, mod_b)
    cos_t, sin_t = _rope_tables(n_tok)

    tm = _pick(n_tok, 1024)
    tmc = _pick(n_ctx, 256)
    lat_tiles = n_tok // tm
    lat_row = lambda i: i // lat_tiles
    ctx_row = lambda i: bsz
    tm_ffn = _pick(n_tok, 1024)
    ffn_row = lambda i: i // (n_tok // tm_ffn)

    h = x.reshape(m_lat, d)
    hc = ctx.reshape(m_ctx, d)
    for layer in range(DEPTH):
        need_ctx = layer < DEPTH - 1
        last = layer == DEPTH - 1
        mod = mod_all[layer].reshape(MOD_ROWS, 6, 1, d)
        n1 = norm1_g[layer].reshape(1, d)
        n2 = norm2_g[layer].reshape(1, d)
        i = layer // 2
        if layer % 2 == 0:
            lam_init = 0.8 - 0.6 * math.exp(-B_LAMBDA_DECAY * layer)
            w_in = attn_w_in[i][:, _attn_in_cols()].astype(BF16)
            w_out = attn_w_out[i]
            w_oa = w_out[_attn_out_rows()].astype(BF16)
            w_ob = w_out[A_Q:].astype(BF16)
            sink = attn_sink[i]
            subln = diff_subln_g[i].reshape(1, LANES)
            aq, bq0, bq1, bk, ak, av, bv = _attn_inproj(h, mod, n1, w_in, cos_t, sin_t, tm=tm, row_of=lat_row,
                                                        rope=True, tiles_per_seq=lat_tiles)
            caq, cbq0, cbq1, cbk, cak, cav, cbv = _attn_inproj(hc, mod, n1, w_in, cos_t, sin_t, tm=tmc,
                                                               row_of=ctx_row, rope=False, tiles_per_seq=1)
            r3 = lambda a, n: a.reshape(bsz, n, a.shape[-1])
            cak3, cav3, cbk3, cbv3 = r3(cak, n_ctx), r3(cav, n_ctx), r3(cbk, n_ctx), r3(cbv, n_ctx)
            oa = _gqa_window(sink, r3(aq, n_tok), r3(ak, n_tok), r3(av, n_tok), cak3, cav3)
            b_args = (diff_lambda[i], subln, r3(bq0, n_tok), r3(bq1, n_tok),
                      [(r3(bk, n_tok), r3(bv, n_tok)), (cbk3, cbv3)])
            b_kw = dict(tq=_pick(n_tok, 512), lam_init=lam_init)
            ob_lagged, over = _diff_attn(*b_args, lagged=True, **b_kw)
            ob = lax.cond(jnp.max(over) <= LAG_LIMIT, lambda: ob_lagged,
                          lambda: _diff_attn(*b_args, **b_kw)[0])
            mix, w_mix = [oa.reshape(m_lat, A_Q), ob.reshape(m_lat, B_V)], [w_oa, w_ob]
            if need_ctx:
                oca = _gqa_context(sink, r3(caq, n_ctx), cak3, cav3)
                ocb = _diff_attn(diff_lambda[i], subln, r3(cbq0, n_ctx), r3(cbq1, n_ctx), [(cbk3, cbv3)],
                                 tq=_pick(n_ctx, 256), lam_init=lam_init)[0]
                mix_c = [oca.reshape(m_ctx, A_Q), ocb.reshape(m_ctx, B_V)]
        else:
            w1 = gla_gate_w1[i]
            pad = jnp.zeros((d, LANES - 2 * C_GATE_RANK), F32)
            w_in = jnp.concatenate([gla_w_in[i], w1[0], w1[1], pad], axis=1).astype(BF16)
            w2 = gla_gate_w2[i]
            w2bd = jnp.zeros((LANES, 2 * C_QK), F32)
            w2bd = w2bd.at[0:C_GATE_RANK, 0:C_QK].set(w2[0]).at[C_GATE_RANK:2 * C_GATE_RANK, C_QK:].set(w2[1])
            w2bd = w2bd.astype(BF16)
            gb = gla_gate_b[i].reshape(1, 2 * C_QK)
            ng = gla_norm_g[i].reshape(1, C_DV)
            q, k, v, sg, la_f, la_b = _gla_inproj(h, mod, n1, w_in, w2bd, gb, tm=tm, row_of=lat_row)
            qc, kc, vc, sgc, lac_f, lac_b = _gla_inproj(hc, mod, n1, w_in, w2bd, gb, tm=tmc, row_of=ctx_row)
            r3 = lambda a, n: a.reshape(bsz, n, a.shape[-1])
            s_f, s_b = _gla_ctx_state(r3(kc, n_ctx), r3(vc, n_ctx), r3(lac_f, n_ctx), r3(lac_b, n_ctx))
            gt = _pick(n_tok, 256)
            q3, k3, v3 = r3(q, n_tok), r3(k, n_tok), r3(v, n_tok)
            o_b = _gla_scan(q3, k3, v3, r3(la_b, n_tok), s_b, gt=gt, reverse=True)
            og = _gla_scan(q3, k3, v3, r3(la_f, n_tok), s_f, gt=gt, reverse=False,
                           o_other=o_b, sg=r3(sg, n_tok), norm_g=ng)
            mix, w_mix = [og.reshape(m_lat, C_V)], [gla_w_out[i].astype(BF16)]
            if need_ctx:
                z = jnp.zeros((bsz, C_HEADS, C_DV, C_DK), F32)
                qc3, kc3, vc3 = r3(qc, n_ctx), r3(kc, n_ctx), r3(vc, n_ctx)
                gtc = _pick(n_ctx, 256)
                oc_b = _gla_scan(qc3, kc3, vc3, r3(lac_b, n_ctx), z, gt=gtc, reverse=True)
                ogc = _gla_scan(qc3, kc3, vc3, r3(lac_f, n_ctx), z, gt=gtc, reverse=False,
                                o_other=oc_b, sg=r3(sgc, n_ctx), norm_g=ng)
                mix_c = [ogc.reshape(m_ctx, C_V)]
        ffn_w = _ffn_weights(ffn_w_up[layer], ffn_conv_w[layer], ffn_conv_b[layer], ffn_w_down[layer], FFN_TF)
        fg = final_norm_g.reshape(1, d)
        h = _mixer_out_ffn(mix, w_mix, h, mod, n2, ffn_w, fg, tm=tm_ffn, row_of=ffn_row,
                           tiles_per_seq=n_tok // tm_ffn, final_norm=last)
        if need_ctx:
            hc = _mixer_out_ffn(mix_c, w_mix, hc, mod, n2, ffn_w, fg, tm=tmc, row_of=ctx_row,
                                tiles_per_seq=n_ctx // tmc, final_norm=False)
    return h.reshape(bsz, n_tok, d)
```

```python
import functools
import math

import numpy as np
import jax
import jax.numpy as jnp
from jax import lax
from jax.experimental import pallas as pl
from jax.experimental.pallas import tpu as pltpu

F32 = jnp.float32
BF16 = jnp.bfloat16

D_MODEL = 1024
DEPTH = 2
GRID_W = 64
HEAD_DIM = 64
ROPE_THETA = 10000.0
EPS = 1e-6
BLOCK = 128
A_HEADS = 8
A_KV_HEADS = 2
A_GROUP = A_HEADS // A_KV_HEADS
B_HEADS = 4
B_LAMBDA_DECAY = 0.3
A_Q = A_HEADS * HEAD_DIM
A_KV = A_KV_HEADS * HEAD_DIM
B_QK = B_HEADS * 2 * HEAD_DIM
B_V = B_HEADS * 2 * HEAD_DIM
C_HEADS = 4
C_DK = D_MODEL // 2 // C_HEADS
C_DV = D_MODEL // C_HEADS
C_GATE_RANK = 16
C_GATE_NORM = 16.0
C_CHUNK = 64
C_QK = C_HEADS * C_DK
C_V = C_HEADS * C_DV
D_FF = 2816
LANES = 128
MOD_ROWS = 8
VMEM_LIMIT = 56 * 1024 * 1024


def _cparams(sem):
    return pltpu.CompilerParams(dimension_semantics=sem, vmem_limit_bytes=VMEM_LIMIT)


def _dot(a, b):
    return jnp.dot(a, b, preferred_element_type=F32)


def _dot_nt(a, b):
    return lax.dot_general(a, b, (((1,), (1,)), ((), ())), preferred_element_type=F32)


def _dot_tn(a, b):
    return lax.dot_general(a, b, (((0,), (0,)), ((), ())), preferred_element_type=F32)


def _sigmoid(x):
    return 1.0 / (1.0 + jnp.exp(-x))


def _norm_mod(x, g, sc, sh):
    ms = jnp.mean(x * x, axis=-1, keepdims=True)
    return (x * lax.rsqrt(ms + EPS) * g) * (1.0 + sc) + sh


def _mod_kernel(c_ref, w_ref, b_ref, o_ref):
    c = c_ref[...]
    s = (c * _sigmoid(c)).astype(BF16)
    o_ref[...] = _dot(s, w_ref[...].astype(BF16)) + b_ref[...]


def _modulation(c_rows, mod_w, mod_b):
    d = D_MODEL
    return pl.pallas_call(
        _mod_kernel,
        grid=(DEPTH, 6),
        in_specs=[
            pl.BlockSpec((MOD_ROWS, d), lambda l, n: (0, 0)),
            pl.BlockSpec((None, d, d), lambda l, n: (l, 0, n)),
            pl.BlockSpec((None, 1, d), lambda l, n: (l, 0, n)),
        ],
        out_specs=pl.BlockSpec((None, MOD_ROWS, d), lambda l, n: (l, 0, n)),
        out_shape=jax.ShapeDtypeStruct((DEPTH, MOD_ROWS, 6 * d), F32),
        compiler_params=_cparams(("parallel", "parallel")),
        name="modulation",
    )(c_rows, mod_w, mod_b.reshape(DEPTH, 1, 6 * d))


def _mod_spec(k, row_of):
    return pl.BlockSpec((None, None, 1, D_MODEL), lambda i, *_: (row_of(i), k, 0, 0))


LOG2E = math.log2(math.e)
_Q_SCALE = HEAD_DIM ** -0.5 * LOG2E
_ATTN_GROUPS = (("aq", A_Q, True, _Q_SCALE), ("bq", B_QK, True, _Q_SCALE), ("bk", B_QK, True, 1.0),
                ("ak", A_KV, True, 1.0), ("av", A_KV, False, 1.0), ("bv", B_V, False, 1.0))


def _rope_chunk(v, cos, sin, first):
    partner = jnp.where(first, pltpu.roll(v, 96, 1), pltpu.roll(v, 32, 1))
    return v * cos + partner * sin


def _attn_inproj_kernel(x_ref, sh_ref, sc_ref, g_ref, w_ref, cos_ref, sin_ref, *out_refs, rope):
    xn = _norm_mod(x_ref[...], g_ref[...], sc_ref[...], sh_ref[...]).astype(BF16)
    tm = xn.shape[0]
    if rope:
        cos = cos_ref[...]
        sin = sin_ref[...]
        lane = lax.broadcasted_iota(jnp.int32, (tm, LANES), 1)
        first = (lane % HEAD_DIM) < (HEAD_DIM // 2)
    lo = lax.broadcasted_iota(jnp.int32, (1, LANES), 1) < HEAD_DIM
    col = 0
    outs = iter(out_refs)
    for name, width, roped, scale in _ATTN_GROUPS:
        y = _dot(xn, w_ref[:, col:col + width])
        col += width
        o_refs = [next(outs), next(outs)] if name == "bq" else [next(outs)]
        for c in range(width // LANES):
            v = y[:, c * LANES:(c + 1) * LANES]
            if rope and roped:
                v = _rope_chunk(v, cos, sin, first)
            if scale != 1.0:
                v = v * scale
            if name == "bq":
                o_refs[0][:, c * LANES:(c + 1) * LANES] = jnp.where(lo, v, 0.0).astype(BF16)
                o_refs[1][:, c * LANES:(c + 1) * LANES] = jnp.where(lo, 0.0, v).astype(BF16)
            else:
                o_refs[0][:, c * LANES:(c + 1) * LANES] = v.astype(BF16)


def _attn_inproj(x2d, mod, norm_g, w, cos_t, sin_t, *, tm, row_of, rope, tiles_per_seq):
    m, d = x2d.shape
    n_all = w.shape[1]
    widths = [width for (name, width, _, _) in _ATTN_GROUPS for _ in range(2 if name == "bq" else 1)]
    out_shape = [jax.ShapeDtypeStruct((m, width), BF16) for width in widths]
    out_specs = [pl.BlockSpec((tm, width), lambda i: (i, 0)) for width in widths]
    return pl.pallas_call(
        functools.partial(_attn_inproj_kernel, rope=rope),
        grid=(m // tm,),
        in_specs=[
            pl.BlockSpec((tm, d), lambda i: (i, 0)),
            _mod_spec(0, row_of), _mod_spec(1, row_of),
            pl.BlockSpec((1, d), lambda i: (0, 0)),
            pl.BlockSpec((d, n_all), lambda i: (0, 0)),
            pl.BlockSpec((tm, LANES), lambda i: (i % tiles_per_seq, 0)),
            pl.BlockSpec((tm, LANES), lambda i: (i % tiles_per_seq, 0)),
        ],
        out_specs=out_specs,
        out_shape=out_shape,
        compiler_params=_cparams(("parallel",)),
        name="attn_inproj_rope" if rope else "attn_inproj_ctx",
    )(x2d, mod, mod, norm_g, w, cos_t, sin_t)


GQA_QB = 8


def _gqa_kernel(sink_ref, q_ref, *refs, window, nb):
    tq = BLOCK
    qb = q_ref.shape[0] // tq
    if window:
        k_blk = [r[...] for r in refs[0:qb + 2]]
        v_blk = [r[...] for r in refs[qb + 2:2 * qb + 4]]
        kx_ref, vx_ref, o_ref = refs[2 * qb + 4:]
    else:
        kx_ref, vx_ref, o_ref = refs
    lane = lax.broadcasted_iota(jnp.int32, (1, LANES), 1)
    lo = lane < HEAD_DIM
    half = (jnp.where(lo, 1.0, 0.0).astype(F32), jnp.where(lo, 0.0, 1.0).astype(F32))
    row = lax.broadcasted_iota(jnp.int32, (LANES, 1), 0)
    sinks = [jnp.concatenate([jnp.full((1, tq), sink_ref[kv * A_GROUP + j] * LOG2E, F32)
                              for j in range(A_GROUP)], axis=1) for kv in range(A_KV_HEADS)]
    work = []
    for u in range(qb):
        if window:
            keys = jnp.concatenate(k_blk[u:u + 3] + [kx_ref[...]], axis=0)
            vals = jnp.concatenate(v_blk[u:u + 3] + [vx_ref[...]], axis=0)
        else:
            keys, vals = kx_ref[...], vx_ref[...]
        qf = [q_ref[u * tq:(u + 1) * tq, j * LANES:(j + 1) * LANES].astype(F32) for j in range(A_GROUP)]
        scs = [_dot_nt(keys, jnp.concatenate([(q * half[kv]).astype(BF16) for q in qf], axis=0))
               for kv in range(A_KV_HEADS)]
        work.append((vals, scs))
    for u, (vals, scs) in enumerate(work):
        if window:
            n = pl.program_id(1) * qb + u
            s = lax.broadcasted_iota(jnp.int32, (BLOCK, tq), 0)
            t = lax.broadcasted_iota(jnp.int32, (BLOCK, tq), 1)
            has_prev = jnp.where(n > 0, 1, 0)
            has_next = jnp.where(n < nb - 1, 1, 0)
            lower = t * has_prev + BLOCK * (1 - has_prev)
            upper = (t + 1) * has_next - 1
            ninf = jnp.float32(-jnp.inf)
            bias_prev = jnp.concatenate([jnp.where(s < lower, ninf, 0.0)] * A_GROUP, axis=1)
            bias_next = jnp.concatenate([jnp.where(s > upper, ninf, 0.0)] * A_GROUP, axis=1)
        outs = []
        for kv in range(A_KV_HEADS):
            sc = scs[kv]
            if window:
                sc = jnp.concatenate([sc[0:BLOCK] + bias_prev, sc[BLOCK:2 * BLOCK],
                                      sc[2 * BLOCK:3 * BLOCK] + bias_next, sc[3 * BLOCK:]], axis=0)
            sk = sinks[kv]
            mx = jnp.maximum(jnp.max(sc, axis=0, keepdims=True), sk)
            p = jnp.exp2(sc - mx)
            den = jnp.sum(p, axis=0, keepdims=True) + jnp.exp2(sk - mx)
            pn = (p * (1.0 / den)).astype(BF16)
            outs.append(_dot_tn(vals, pn))
        o_t = jnp.where(row < HEAD_DIM, outs[0], outs[1])
        for j in range(A_GROUP):
            o_ref[u * tq:(u + 1) * tq, j * LANES:(j + 1) * LANES] = o_t[:, j * tq:(j + 1) * tq].T.astype(BF16)


def _gqa_window(sink, aq, ak, av, cak, cav):
    bsz, n_tok, _ = aq.shape
    n_ctx = cak.shape[1]
    nb = n_tok // BLOCK
    qb = math.gcd(nb, GQA_QB)
    kv_blocks = [pl.BlockSpec((None, BLOCK, A_KV),
                              lambda b, n, off=off: (b, jnp.clip(n * qb + off, 0, nb - 1), 0))
                 for off in range(-1, qb + 1)]
    kv_ctx = pl.BlockSpec((None, n_ctx, A_KV), lambda b, n: (b, 0, 0))
    return pl.pallas_call(
        functools.partial(_gqa_kernel, window=True, nb=nb),
        grid=(bsz, nb // qb),
        in_specs=[pl.BlockSpec(memory_space=pltpu.SMEM),
                  pl.BlockSpec((None, qb * BLOCK, A_Q), lambda b, n: (b, n, 0))]
        + kv_blocks + kv_blocks + [kv_ctx, kv_ctx],
        out_specs=pl.BlockSpec((None, qb * BLOCK, A_Q), lambda b, n: (b, n, 0)),
        out_shape=jax.ShapeDtypeStruct((bsz, n_tok, A_Q), BF16),
        compiler_params=_cparams(("parallel", "parallel")),
        name="gqa_window",
    )(sink, aq, *([ak] * (qb + 2)), *([av] * (qb + 2)), cak, cav)


def _gqa_context(sink, caq, cak, cav):
    bsz, n_ctx, _ = caq.shape
    nb = n_ctx // BLOCK
    kv_ctx = pl.BlockSpec((None, n_ctx, A_KV), lambda b, n: (b, 0, 0))
    return pl.pallas_call(
        functools.partial(_gqa_kernel, window=False, nb=nb),
        grid=(bsz, nb),
        in_specs=[pl.BlockSpec(memory_space=pltpu.SMEM),
                  pl.BlockSpec((None, BLOCK, A_Q), lambda b, n: (b, n, 0)),
                  kv_ctx, kv_ctx],
        out_specs=pl.BlockSpec((None, BLOCK, A_Q), lambda b, n: (b, n, 0)),
        out_shape=jax.ShapeDtypeStruct((bsz, n_ctx, A_Q), BF16),
        compiler_params=_cparams(("parallel", "parallel")),
        name="gqa_context",
    )(sink, caq, cak, cav)


LAG_LIMIT = 12.0
XPOSE_ROWS = 512
ONES_ROWS = 16


def _diff_attn_kernel(lam_ref, g_ref, q0_ref, q1_ref, *refs, nseg, lam_init, kc, lagged):
    k_refs = refs[0:2 * nseg:2]
    v_refs = refs[1:2 * nseg:2]
    if lagged:
        o_ref, ex_ref, vt_ref = refs[2 * nseg:]
    else:
        o_ref, vt_ref = refs[2 * nseg:]
    lv = lam_ref[...]
    lam = (jnp.exp(jnp.sum(lv[0:1] * lv[1:2], axis=-1, keepdims=True))
           - jnp.exp(jnp.sum(lv[2:3] * lv[3:4], axis=-1, keepdims=True)) + lam_init)
    q0 = q0_ref[...]
    q1 = q1_ref[...]
    @pl.when(pl.program_id(2) == 0)
    def _():
        off = 0
        for v_ref in v_refs:
            ns = v_ref.shape[0]
            for c0 in range(0, ns, XPOSE_ROWS):
                n = min(XPOSE_ROWS, ns - c0)
                vt_ref[0:LANES, off + c0:off + c0 + n] = v_ref[c0:c0 + n, :].astype(F32).T.astype(BF16)
            off += ns
        vt_ref[LANES:, :] = jnp.ones((vt_ref.shape[0] - LANES, vt_ref.shape[1]), BF16)

    chunks = []
    off = 0
    for k_ref in k_refs:
        ns = k_ref.shape[0]
        step = min(kc, ns)
        chunks += [(k_ref, c0, step, off + c0) for c0 in range(0, ns, step)]
        off += ns

    qms = (q0, q1)

    def scores(mi, ci):
        k_ref, c0, step, _ = chunks[ci]
        return _dot_nt(k_ref[c0:c0 + step, :], qms[mi])

    run_max = [None, None]
    accs = [None, None]
    excess = [None, None]

    def probs(mi, sc):
        mc = jnp.max(sc, axis=0, keepdims=True)
        if run_max[mi] is None:
            run_max[mi], alpha = (mc, mc), None
        else:
            ref_old, best = run_max[mi]
            new_best = jnp.maximum(best, mc)
            ref = best if lagged else new_best
            alpha = jnp.exp2(ref_old - ref)
            if lagged:
                over = mc - ref
                excess[mi] = over if excess[mi] is None else jnp.maximum(excess[mi], over)
            run_max[mi] = (ref, new_best)
        return jnp.exp2((sc - run_max[mi][0]).astype(BF16)), alpha

    def accumulate(mi, ci, pb, alpha):
        _, _, step, g0 = chunks[ci]
        pv = _dot(vt_ref[:, g0:g0 + step], pb)
        accs[mi] = pv if alpha is None else accs[mi] * alpha + pv

    n_ch = len(chunks)
    sc_q = {ci: [scores(mi, ci) for mi in range(2)] for ci in range(min(2, n_ch))}
    pb_q = {0: [probs(mi, sc_q[0][mi]) for mi in range(2)]}
    for ci in range(n_ch):
        if ci + 2 < n_ch:
            sc_q[ci + 2] = [scores(mi, ci + 2) for mi in range(2)]
        if ci + 1 < n_ch:
            sc_pair = sc_q.pop(ci + 1)
            pb_q[ci + 1] = [probs(mi, sc_pair[mi]) for mi in range(2)]
        for mi, (pb, alpha) in enumerate(pb_q.pop(ci)):
            accumulate(mi, ci, pb, alpha)
    r0 = 1.0 / accs[0][LANES:LANES + 1, :]
    r1 = lam / accs[1][LANES:LANES + 1, :]
    o = (accs[0][0:LANES, :] * r0 - accs[1][0:LANES, :] * r1).T
    ms = jnp.mean(o * o, axis=-1, keepdims=True)
    o_ref[...] = ((o * lax.rsqrt(ms + EPS) * g_ref[...]) * (1.0 - lam_init)).astype(BF16)
    if lagged:
        ex_ref[...] = jnp.maximum(excess[0], excess[1])


def _diff_attn(lam_vec, subln_g, q0, q1, kv_list, *, tq, lam_init, kc=256, lagged=False):
    bsz, n_q, _ = q0.shape
    out_specs = [pl.BlockSpec((None, tq, LANES), lambda b, h, i: (b, i, h))]
    out_shape = [jax.ShapeDtypeStruct((bsz, n_q, B_V), BF16)]
    if lagged:
        out_specs.append(pl.BlockSpec((None, None, 1, tq), lambda b, h, i: (b, h, 0, i)))
        out_shape.append(jax.ShapeDtypeStruct((bsz, B_HEADS, 1, n_q), F32))
    in_specs = [pl.BlockSpec((4, HEAD_DIM), lambda b, h, i: (0, 0)),
                pl.BlockSpec((1, LANES), lambda b, h, i: (0, 0)),
                pl.BlockSpec((None, tq, LANES), lambda b, h, i: (b, i, h)),
                pl.BlockSpec((None, tq, LANES), lambda b, h, i: (b, i, h))]
    args = [lam_vec, subln_g, q0, q1]
    for k, v in kv_list:
        ns = k.shape[1]
        in_specs += [pl.BlockSpec((None, ns, LANES), lambda b, h, i: (b, 0, h))] * 2
        args += [k, v]
    return pl.pallas_call(
        functools.partial(_diff_attn_kernel, nseg=len(kv_list), lam_init=lam_init, kc=kc, lagged=lagged),
        grid=(bsz, B_HEADS, n_q // tq),
        in_specs=in_specs,
        out_specs=out_specs,
        out_shape=out_shape,
        scratch_shapes=[pltpu.VMEM((LANES + ONES_ROWS, sum(k.shape[1] for k, _ in kv_list)), BF16)],
        compiler_params=_cparams(("parallel", "parallel", "arbitrary")),
        name="diff_attn_%dseg%s" % (len(kv_list), "_lagged" if lagged else ""),
    )(*args)


FFN_HALO = 16
FFN_TF = 256


def _ffn_kernel(*refs, n_mix, tiles_per_seq, final_norm):
    a_refs = refs[:3 * n_mix]
    wo_refs = refs[3 * n_mix:4 * n_mix]
    (h_ref, hp_ref, hn_ref, g1_ref, sh_ref, sc_ref, gt_ref, ng_ref, wu_ref, wg_ref, cwu_ref, cwg_ref,
     wd_ref, fg_ref, o_ref, xn_ref, hu_a, hg_a, hu_b, hg_b, acc_ref) = refs[4 * n_mix:]
    i = pl.program_id(0)
    tm = h_ref.shape[0]
    hl = FFN_HALO
    nj, tf = wd_ref.shape[0], wd_ref.shape[1]

    ext = tm + 2 * hl
    half = ext // 2
    pos = i % tiles_per_seq
    a_ext = [jnp.concatenate([a_refs[3 * m + 1][...], a_refs[3 * m][...], a_refs[3 * m + 2][...]], axis=0)
             for m in range(n_mix)]
    h_ext = jnp.concatenate([hp_ref[...], h_ref[...], hn_ref[...]], axis=0)
    ys = []
    for blk in range(2):
        rows = slice(blk * half, (blk + 1) * half)
        y = None
        for m in range(n_mix):
            part = _dot(a_ext[m][rows, :], wo_refs[m][...])
            y = part if y is None else y + part
        ys.append(y)
    for blk in range(2):
        h1 = h_ext[blk * half:(blk + 1) * half, :] + g1_ref[...] * ys[blk]
        xn = _norm_mod(h1, ng_ref[...], sc_ref[...], sh_ref[...])
        if blk == 0:
            o_ref[0:half - hl, :] = h1[hl:, :]
            xn_ref[0:hl, :] = (xn[0:hl, :] * jnp.where(pos > 0, 1.0, 0.0)).astype(BF16)
            xn_ref[hl:half, :] = xn[hl:, :].astype(BF16)
        else:
            o_ref[half - hl:, :] = h1[0:half - hl, :]
            xn_ref[half:ext - hl, :] = xn[0:half - hl, :].astype(BF16)
            xn_ref[ext - hl:, :] = (xn[half - hl:, :] * jnp.where(pos < tiles_per_seq - 1, 1.0, 0.0)).astype(BF16)
    acc_ref[...] = jnp.zeros_like(acc_ref)

    def up(j, hu_ref, hg_ref, row_blocks=1):
        cols = pl.ds(pl.multiple_of(j * tf, tf), tf)
        for blk in range(row_blocks):
            rows = slice(blk * ext // row_blocks, (blk + 1) * ext // row_blocks)
            xn = xn_ref[rows, :]
            hu_ref[rows, :] = _dot(xn, wu_ref[:, cols])
            hg_ref[rows, :] = _dot(xn, wg_ref[:, cols])

    def conv(ref, cw):
        return (ref[hl - 1:hl - 1 + tm, :] * cw[0:1] + ref[hl:hl + tm, :] * cw[1:2]
                + ref[hl + 1:hl + 1 + tm, :] * cw[2:3] + cw[3:4])

    def act(j, hu_ref, hg_ref):
        u = conv(hu_ref, cwu_ref[j])
        gt = conv(hg_ref, cwg_ref[j])
        return (gt * _sigmoid(gt) * u).astype(BF16)

    up(0, hu_a, hg_a, row_blocks=2)

    def pair(jj, carry):
        j = 2 * jj
        up(j + 1, hu_b, hg_b)
        acc_ref[...] += _dot(act(j, hu_a, hg_a), wd_ref[j])
        up(j + 2, hu_a, hg_a)
        acc_ref[...] += _dot(act(j + 1, hu_b, hg_b), wd_ref[j + 1])
        return carry

    assert nj % 2 == 1
    lax.fori_loop(0, (nj - 1) // 2, pair, 0)
    acc_ref[...] += _dot(act(nj - 1, hu_a, hg_a), wd_ref[nj - 1])

    y = o_ref[...] + gt_ref[...] * acc_ref[...]
    if final_norm:
        ms = jnp.mean(y * y, axis=-1, keepdims=True)
        y = y * lax.rsqrt(ms + EPS) * fg_ref[...]
    o_ref[...] = y


def _ffn_weights(w_up, conv_w, conv_b, w_down, tf):
    d = w_up.shape[0]
    nj = D_FF // tf
    chunked = lambda w: w.reshape(w.shape[0], nj, tf).transpose(1, 0, 2)
    w_uu = w_up[:, :D_FF].astype(BF16)
    w_ug = w_up[:, D_FF:].astype(BF16)
    cw = jnp.concatenate([conv_w, conv_b[None, :]], axis=0)
    return (w_uu, w_ug, chunked(cw[:, :D_FF]), chunked(cw[:, D_FF:]), w_down.reshape(nj, tf, d).astype(BF16))


def _mixer_out_ffn(acts, w_outs, h2d, mod, norm_g, weights, final_g, *, tm, row_of, tiles_per_seq, final_norm):
    m, d = h2d.shape
    hl = FFN_HALO
    nhb = m // hl
    r = tm // hl
    w_uu, w_ug, cwu, cwg, w_dn = weights
    tf = w_dn.shape[1]
    resident = lambda a: pl.BlockSpec(a.shape, lambda i: (0,) * a.ndim, pipeline_mode=pl.Buffered(1))

    def tile_and_halos(width):
        return [pl.BlockSpec((tm, width), lambda i: (i, 0)),
                pl.BlockSpec((hl, width), lambda i: (jnp.maximum(i * r - 1, 0), 0)),
                pl.BlockSpec((hl, width), lambda i: (jnp.minimum((i + 1) * r, nhb - 1), 0))]

    in_specs, args = [], []
    for a in acts:
        in_specs += tile_and_halos(a.shape[1])
        args += [a, a, a]
    in_specs += [resident(w) for w in w_outs]
    args += list(w_outs)
    in_specs += tile_and_halos(d)
    in_specs += [_mod_spec(2, row_of), _mod_spec(3, row_of), _mod_spec(4, row_of), _mod_spec(5, row_of),
                 pl.BlockSpec((1, d), lambda i: (0, 0)),
                 resident(w_uu), resident(w_ug), resident(cwu), resident(cwg), resident(w_dn),
                 pl.BlockSpec((1, d), lambda i: (0, 0))]
    args += [h2d, h2d, h2d, mod, mod, mod, mod, norm_g, w_uu, w_ug, cwu, cwg, w_dn, final_g]
    return pl.pallas_call(
        functools.partial(_ffn_kernel, n_mix=len(acts), tiles_per_seq=tiles_per_seq, final_norm=final_norm),
        grid=(m // tm,),
        in_specs=in_specs,
        out_specs=pl.BlockSpec((tm, d), lambda i: (i, 0)),
        out_shape=jax.ShapeDtypeStruct((m, d), F32),
        scratch_shapes=[pltpu.VMEM((tm + 2 * hl, d), BF16)]
        + [pltpu.VMEM((tm + 2 * hl, tf), F32)] * 4
        + [pltpu.VMEM((tm, d), F32)],
        compiler_params=_cparams(("parallel",)),
        name="mixer_out_conv_ffn",
    )(*args)


def _gla_inproj_kernel(x_ref, sh_ref, sc_ref, g_ref, w_ref, w2_ref, gb_ref,
                       q_ref, k_ref, v_ref, sg_ref, laf_ref, lab_ref):
    tm = x_ref.shape[0]
    n_split = 2 if tm % 32 == 0 else 1
    blocks = [slice(s * tm // n_split, (s + 1) * tm // n_split) for s in range(n_split)]
    xns = [_norm_mod(x_ref[rows, :], g_ref[...], sc_ref[...], sh_ref[...]).astype(BF16) for rows in blocks]
    for rows, xn in zip(blocks, xns):
        q_ref[rows, :] = (_dot(xn, w_ref[:, 0:C_QK]) * (C_DK ** -0.5)).astype(BF16)
        k_ref[rows, :] = _dot(xn, w_ref[:, C_QK:2 * C_QK]).astype(BF16)
        for c in range(C_V // C_QK):
            lo = 2 * C_QK + c * C_QK
            v_ref[rows, c * C_QK:(c + 1) * C_QK] = _dot(xn, w_ref[:, lo:lo + C_QK]).astype(BF16)
        for c in range(C_V // C_QK):
            lo = 2 * C_QK + C_V + c * C_QK
            gg = _dot(xn, w_ref[:, lo:lo + C_QK])
            sg_ref[rows, c * C_QK:(c + 1) * C_QK] = (gg * _sigmoid(gg)).astype(BF16)
        lo = 2 * C_QK + 2 * C_V
        r = _dot(xn, w_ref[:, lo:lo + LANES]).astype(BF16)
        for dr, la_ref in enumerate((laf_ref, lab_ref)):
            z = _dot(r, w2_ref[:, dr * C_QK:(dr + 1) * C_QK]) + gb_ref[:, dr * C_QK:(dr + 1) * C_QK]
            la_ref[rows, :] = (jnp.minimum(z, 0.0) - jnp.log(1.0 + jnp.exp(-jnp.abs(z)))) * (1.0 / C_GATE_NORM)


def _gla_inproj(x2d, mod, norm_g, w, w2, gb, *, tm, row_of):
    m, d = x2d.shape
    widths = (C_QK, C_QK, C_V, C_V, C_QK, C_QK)
    dts = (BF16, BF16, BF16, BF16, F32, F32)
    return pl.pallas_call(
        _gla_inproj_kernel,
        grid=(m // tm,),
        in_specs=[
            pl.BlockSpec((tm, d), lambda i: (i, 0)),
            _mod_spec(0, row_of), _mod_spec(1, row_of),
            pl.BlockSpec((1, d), lambda i: (0, 0)),
            pl.BlockSpec(w.shape, lambda i: (0, 0)),
            pl.BlockSpec(w2.shape, lambda i: (0, 0)),
            pl.BlockSpec(gb.shape, lambda i: (0, 0)),
        ],
        out_specs=[pl.BlockSpec((tm, wd), lambda i: (i, 0)) for wd in widths],
        out_shape=[jax.ShapeDtypeStruct((m, wd), dt) for wd, dt in zip(widths, dts)],
        compiler_params=_cparams(("parallel",)),
        name="gla_inproj",
    )(x2d, mod, mod, norm_g, w, w2, gb)


def _tri(n, reverse):
    r = lax.broadcasted_iota(jnp.int32, (n, n), 0)
    c = lax.broadcasted_iota(jnp.int32, (n, n), 1)
    return (c >= r) if reverse else (c <= r)


def _cumsum_rows(la, tri_bf):
    hi = la.astype(BF16)
    r1 = la - hi.astype(F32)
    mid = r1.astype(BF16)
    lo = (r1 - mid.astype(F32)).astype(BF16)
    return _dot(tri_bf, hi) + _dot(tri_bf, mid) + _dot(tri_bf, lo)


def _gla_ctx_state_kernel(k_ref, v_ref, laf_ref, lab_ref, sf_ref, sb_ref):
    n = k_ref.shape[0]
    for reverse, la_ref, s_ref in ((False, laf_ref, sf_ref), (True, lab_ref, sb_ref)):
        tri = jnp.where(_tri(n, reverse), 1.0, 0.0).astype(BF16)
        b = _cumsum_rows(la_ref[...], tri)
        b_end = b[0:1, :] if reverse else b[n - 1:n, :]
        kw = (k_ref[...].astype(F32) * jnp.exp(b_end - b)).astype(BF16)
        for h in range(C_HEADS):
            s_ref[h] = _dot_tn(v_ref[:, h * C_DV:(h + 1) * C_DV], kw[:, h * C_DK:(h + 1) * C_DK])


def _gla_ctx_state(kc, vc, lac_f, lac_b):
    bsz, n, _ = kc.shape
    s_shape = jax.ShapeDtypeStruct((bsz, C_HEADS, C_DV, C_DK), F32)
    s_spec = pl.BlockSpec((None, C_HEADS, C_DV, C_DK), lambda b: (b, 0, 0, 0))
    return pl.pallas_call(
        _gla_ctx_state_kernel,
        grid=(bsz,),
        in_specs=[pl.BlockSpec((None, n, C_QK), lambda b: (b, 0, 0)),
                  pl.BlockSpec((None, n, C_V), lambda b: (b, 0, 0)),
                  pl.BlockSpec((None, n, C_QK), lambda b: (b, 0, 0)),
                  pl.BlockSpec((None, n, C_QK), lambda b: (b, 0, 0))],
        out_specs=[s_spec, s_spec],
        out_shape=[s_shape, s_shape],
        compiler_params=_cparams(("parallel",)),
        name="gla_ctx_state",
    )(kc, vc, lac_f, lac_b)


GLA_BATCH_BLOCK = 4


def _gla_scan_kernel(q_ref, k_ref, v_ref, la_ref, s0_ref, *refs, reverse, final):
    if final:
        ob_ref, sg_ref, ng_ref, o_ref, st_ref = refs
    else:
        o_ref, st_ref = refs
    gidx = pl.program_id(1)

    @pl.when(gidx == 0)
    def _():
        st_ref[...] = s0_ref[...]

    nbb, gt = q_ref.shape[0], q_ref.shape[1]
    nchunk = gt // C_CHUNK
    c = C_CHUNK
    r = lax.broadcasted_iota(jnp.int32, (gt, gt), 0)
    s = lax.broadcasted_iota(jnp.int32, (gt, gt), 1)
    shift = c.bit_length() - 1
    same_chunk = (r >> shift) == (s >> shift)
    tri = same_chunk & ((s >= r) if reverse else (s <= r))
    tri_bf = jnp.where(tri, 1.0, 0.0).astype(BF16)

    def per_chunk_row(x, row):
        return jnp.concatenate([jnp.broadcast_to(x[ci * c + row:ci * c + row + 1, :], (c, x.shape[1]))
                                for ci in range(nchunk)], axis=0)

    order = range(nchunk - 1, -1, -1) if reverse else range(nchunk)
    kcols = [slice(h * C_DK, (h + 1) * C_DK) for h in range(C_HEADS)]
    vcols_of = [slice(h * C_DV, (h + 1) * C_DV) for h in range(C_HEADS)]
    streams = [(bi, h) for bi in range(nbb) for h in range(C_HEADS)]
    q_out, k_out, decay, o_intra, upd = {}, {}, {}, {}, {}
    for bi in range(nbb):
        b = _cumsum_rows(la_ref[bi], tri_bf)
        b_mid = per_chunk_row(b, c // 2)
        b_end = per_chunk_row(b, 0 if reverse else c - 1)
        qf = q_ref[bi].astype(F32)
        kf = k_ref[bi].astype(F32)
        q_in = (qf * jnp.exp(b - b_mid)).astype(BF16)
        k_in = (kf * jnp.exp(b_mid - b)).astype(BF16)
        q_out[bi] = (qf * jnp.exp(b)).astype(BF16)
        k_out[bi] = (kf * jnp.exp(b_end - b)).astype(BF16)
        decay[bi] = jnp.exp(b_end)
        for h in range(C_HEADS):
            vv = v_ref[bi, :, vcols_of[h]]
            sc = jnp.where(tri, _dot_nt(q_in[:, kcols[h]], k_in[:, kcols[h]]), 0.0).astype(BF16)
            o_intra[bi, h] = _dot(sc, vv)
            upd[bi, h] = {ci: _dot_tn(vv[ci * c:(ci + 1) * c, :], k_out[bi][ci * c:(ci + 1) * c, kcols[h]])
                          for ci in order}
    st = {s_: st_ref[s_[0], s_[1]] for s_ in streams}
    o_inter = {s_: {} for s_ in streams}
    for ci in order:
        rows = slice(ci * c, (ci + 1) * c)
        for bi, h in streams:
            o_inter[bi, h][ci] = _dot_nt(q_out[bi][rows, kcols[h]], st[bi, h].astype(BF16))
            st[bi, h] = st[bi, h] * decay[bi][ci * c:ci * c + 1, kcols[h]] + upd[bi, h][ci]
    for bi, h in streams:
        vcols = vcols_of[h]
        st_ref[bi, h] = st[bi, h]
        o = o_intra[bi, h] + jnp.concatenate([o_inter[bi, h][ci] for ci in range(nchunk)], axis=0)
        if final:
            o = o + ob_ref[bi, :, vcols]
            ms = jnp.mean(o * o, axis=-1, keepdims=True)
            o = (o * lax.rsqrt(ms + EPS) * ng_ref[...]) * sg_ref[bi, :, vcols].astype(F32)
            o_ref[bi, :, vcols] = o.astype(BF16)
        else:
            o_ref[bi, :, vcols] = o


def _gla_scan(q, k, v, la, s0, *, gt, reverse, o_other=None, sg=None, norm_g=None):
    bsz, n_tok, _ = q.shape
    ng = n_tok // gt
    final = o_other is not None
    nbb = GLA_BATCH_BLOCK if bsz % GLA_BATCH_BLOCK == 0 else 1
    gi = (lambda b, g: (b, ng - 1 - g, 0)) if reverse else (lambda b, g: (b, g, 0))
    in_specs = [pl.BlockSpec((nbb, gt, C_QK), gi), pl.BlockSpec((nbb, gt, C_QK), gi),
                pl.BlockSpec((nbb, gt, C_V), gi), pl.BlockSpec((nbb, gt, C_QK), gi),
                pl.BlockSpec((nbb, C_HEADS, C_DV, C_DK), lambda b, g: (b, 0, 0, 0))]
    args = [q, k, v, la, s0]
    if final:
        in_specs += [pl.BlockSpec((nbb, gt, C_V), gi), pl.BlockSpec((nbb, gt, C_V), gi),
                     pl.BlockSpec((1, C_DV), lambda b, g: (0, 0))]
        args += [o_other, sg, norm_g]
    return pl.pallas_call(
        functools.partial(_gla_scan_kernel, reverse=reverse, final=final),
        grid=(bsz // nbb, ng),
        in_specs=in_specs,
        out_specs=pl.BlockSpec((nbb, gt, C_V), gi),
        out_shape=jax.ShapeDtypeStruct((bsz, n_tok, C_V), BF16 if final else F32),
        scratch_shapes=[pltpu.VMEM((nbb, C_HEADS, C_DV, C_DK), F32)],
        compiler_params=_cparams(("parallel", "arbitrary")),
        name="gla_scan_fwd_final" if final else "gla_scan_bwd",
    )(*args)


def _pair_split(n_heads):
    base = np.concatenate([np.arange(0, HEAD_DIM, 2), np.arange(1, HEAD_DIM, 2)])
    return np.concatenate([h * HEAD_DIM + base for h in range(n_heads)])


_A_HEAD_ORDER = np.array([kv * A_GROUP + j for j in range(A_GROUP) for kv in range(A_KV_HEADS)])


def _attn_in_cols():
    aq = (_A_HEAD_ORDER[:, None] * HEAD_DIM + _pair_split(1)[None, :]).reshape(-1)
    o_ak, o_av, o_bq = A_Q, A_Q + A_KV, A_Q + 2 * A_KV
    o_bk, o_bv = o_bq + B_QK, o_bq + 2 * B_QK
    return np.concatenate([aq, o_bq + _pair_split(2 * B_HEADS), o_bk + _pair_split(2 * B_HEADS),
                           o_ak + _pair_split(A_KV_HEADS), o_av + np.arange(A_KV), o_bv + np.arange(B_V)])


def _attn_out_rows():
    oa = (_A_HEAD_ORDER[:, None] * HEAD_DIM + np.arange(HEAD_DIM)[None, :]).reshape(-1)
    return oa


def _rope_tables(n_tok):
    rows = n_tok // GRID_W
    row = jnp.repeat(jnp.arange(rows, dtype=F32), GRID_W)
    col = jnp.tile(jnp.arange(GRID_W, dtype=F32), rows)
    axis_dim = HEAD_DIM // 2
    inv_freq = ROPE_THETA ** (-jnp.arange(0, axis_dim, 2, dtype=F32) / axis_dim)
    ang = jnp.concatenate([row[:, None] * inv_freq, col[:, None] * inv_freq], axis=-1)
    cos, sin = jnp.cos(ang), jnp.sin(ang)
    cos_t = jnp.tile(cos, (1, LANES // (HEAD_DIM // 2)))
    sin_t = jnp.tile(jnp.concatenate([-sin, sin], axis=-1), (1, LANES // HEAD_DIM))
    return cos_t, sin_t


def _pick(n, pref):
    return pref if n % pref == 0 else n


def kernel(x, c, ctx, c_ctx, mod_w, mod_b, norm1_g, norm2_g, attn_w_in, attn_w_out, attn_sink, diff_lambda, diff_subln_g, gla_w_in, gla_gate_w1, gla_gate_w2, gla_gate_b, gla_norm_g, gla_w_out, ffn_w_up, ffn_conv_w, ffn_conv_b, ffn_w_down, final_norm_g):
    bsz, n_tok, d = x.shape
    n_ctx = ctx.shape[1]
    assert d == D_MODEL and bsz + 1 <= MOD_ROWS
    m_lat, m_ctx = bsz * n_tok, bsz * n_ctx

    c_rows = jnp.concatenate([c, c_ctx[None, :], jnp.zeros((MOD_ROWS - bsz - 1, d), F32)], axis=0)
    mod_all = _modulation(c_rows, mod_w, mod_b)
    cos_t, sin_t = _rope_tables(n_tok)

    tm = _pick(n_tok, 1024)
    tmc = _pick(n_ctx, 256)
    lat_tiles = n_tok // tm
    lat_row = lambda i: i // lat_tiles
    ctx_row = lambda i: bsz
    tm_ffn = _pick(n_tok, 1024)
    ffn_row = lambda i: i // (n_tok // tm_ffn)

    h = x.reshape(m_lat, d)
    hc = ctx.reshape(m_ctx, d)
    for layer in range(DEPTH):
        need_ctx = layer < DEPTH - 1
        last = layer == DEPTH - 1
        mod = mod_all[layer].reshape(MOD_ROWS, 6, 1, d)
        n1 = norm1_g[layer].reshape(1, d)
        n2 = norm2_g[layer].reshape(1, d)
        i = layer // 2
        if layer % 2 == 0:
            lam_init = 0.8 - 0.6 * math.exp(-B_LAMBDA_DECAY * layer)
            w_in = attn_w_in[i][:, _attn_in_cols()].astype(BF16)
            w_out = attn_w_out[i]
            w_oa = w_out[_attn_out_rows()].astype(BF16)
            w_ob = w_out[A_Q:].astype(BF16)
            sink = attn_sink[i]
            subln = diff_subln_g[i].reshape(1, LANES)
            aq, bq0, bq1, bk, ak, av, bv = _attn_inproj(h, mod, n1, w_in, cos_t, sin_t, tm=tm, row_of=lat_row,
                                                        rope=True, tiles_per_seq=lat_tiles)
            caq, cbq0, cbq1, cbk, cak, cav, cbv = _attn_inproj(hc, mod, n1, w_in, cos_t, sin_t, tm=tmc,
                                                               row_of=ctx_row, rope=False, tiles_per_seq=1)
            r3 = lambda a, n: a.reshape(bsz, n, a.shape[-1])
            cak3, cav3, cbk3, cbv3 = r3(cak, n_ctx), r3(cav, n_ctx), r3(cbk, n_ctx), r3(cbv, n_ctx)
            oa = _gqa_window(sink, r3(aq, n_tok), r3(ak, n_tok), r3(av, n_tok), cak3, cav3)
            b_args = (diff_lambda[i], subln, r3(bq0, n_tok), r3(bq1, n_tok),
                      [(r3(bk, n_tok), r3(bv, n_tok)), (cbk3, cbv3)])
            b_kw = dict(tq=_pick(n_tok, 1024), lam_init=lam_init)
            ob_lagged, over = _diff_attn(*b_args, lagged=True, **b_kw)
            ob = lax.cond(jnp.max(over) <= LAG_LIMIT, lambda: ob_lagged,
                          lambda: _diff_attn(*b_args, **b_kw)[0])
            mix, w_mix = [oa.reshape(m_lat, A_Q), ob.reshape(m_lat, B_V)], [w_oa, w_ob]
            if need_ctx:
                oca = _gqa_context(sink, r3(caq, n_ctx), cak3, cav3)
                ocb = _diff_attn(diff_lambda[i], subln, r3(cbq0, n_ctx), r3(cbq1, n_ctx), [(cbk3, cbv3)],
                                 tq=_pick(n_ctx, 256), lam_init=lam_init)[0]
                mix_c = [oca.reshape(m_ctx, A_Q), ocb.reshape(m_ctx, B_V)]
        else:
            w1 = gla_gate_w1[i]
            pad = jnp.zeros((d, LANES - 2 * C_GATE_RANK), F32)
            w_in = jnp.concatenate([gla_w_in[i], w1[0], w1[1], pad], axis=1).astype(BF16)
            w2 = gla_gate_w2[i]
            w2bd = jnp.zeros((LANES, 2 * C_QK), F32)
            w2bd = w2bd.at[0:C_GATE_RANK, 0:C_QK].set(w2[0]).at[C_GATE_RANK:2 * C_GATE_RANK, C_QK:].set(w2[1])
            w2bd = w2bd.astype(BF16)
            gb = gla_gate_b[i].reshape(1, 2 * C_QK)
            ng = gla_norm_g[i].reshape(1, C_DV)
            q, k, v, sg, la_f, la_b = _gla_inproj(h, mod, n1, w_in, w2bd, gb, tm=tm, row_of=lat_row)
            qc, kc, vc, sgc, lac_f, lac_b = _gla_inproj(hc, mod, n1, w_in, w2bd, gb, tm=tmc, row_of=ctx_row)
            r3 = lambda a, n: a.reshape(bsz, n, a.shape[-1])
            s_f, s_b = _gla_ctx_state(r3(kc, n_ctx), r3(vc, n_ctx), r3(lac_f, n_ctx), r3(lac_b, n_ctx))
            gt = _pick(n_tok, 256)
            q3, k3, v3 = r3(q, n_tok), r3(k, n_tok), r3(v, n_tok)
            o_b = _gla_scan(q3, k3, v3, r3(la_b, n_tok), s_b, gt=gt, reverse=True)
            og = _gla_scan(q3, k3, v3, r3(la_f, n_tok), s_f, gt=gt, reverse=False,
                           o_other=o_b, sg=r3(sg, n_tok), norm_g=ng)
            mix, w_mix = [og.reshape(m_lat, C_V)], [gla_w_out[i].astype(BF16)]
            if need_ctx:
                z = jnp.zeros((bsz, C_HEADS, C_DV, C_DK), F32)
                qc3, kc3, vc3 = r3(qc, n_ctx), r3(kc, n_ctx), r3(vc, n_ctx)
                gtc = _pick(n_ctx, 256)
                oc_b = _gla_scan(qc3, kc3, vc3, r3(lac_b, n_ctx), z, gt=gtc, reverse=True)
                ogc = _gla_scan(qc3, kc3, vc3, r3(lac_f, n_ctx), z, gt=gtc, reverse=False,
                                o_other=oc_b, sg=r3(sgc, n_ctx), norm_g=ng)
                mix_c = [ogc.reshape(m_ctx, C_V)]
        ffn_w = _ffn_weights(ffn_w_up[layer], ffn_conv_w[layer], ffn_conv_b[layer], ffn_w_down[layer], FFN_TF)
        fg = final_norm_g.reshape(1, d)
        h = _mixer_out_ffn(mix, w_mix, h, mod, n2, ffn_w, fg, tm=tm_ffn, row_of=ffn_row,
                           tiles_per_seq=n_tok // tm_ffn, final_norm=last)
        if need_ctx:
            hc = _mixer_out_ffn(mix_c, w_mix, hc, mod, n2, ffn_w, fg, tm=tmc, row_of=ctx_row,
                                tiles_per_seq=n_ctx // tmc, final_norm=False)
    return h.reshape(bsz, n_tok, d)
```

```python
import functools
import math

import numpy as np
import jax
import jax.numpy as jnp
from jax import lax
from jax.experimental import pallas as pl
from jax.experimental.pallas import tpu as pltpu

F32 = jnp.float32
BF16 = jnp.bfloat16

D_MODEL = 1024
DEPTH = 2
GRID_W = 64
HEAD_DIM = 64
ROPE_THETA = 10000.0
EPS = 1e-6
BLOCK = 128
A_HEADS = 8
A_KV_HEADS = 2
A_GROUP = A_HEADS // A_KV_HEADS
B_HEADS = 4
B_LAMBDA_DECAY = 0.3
A_Q = A_HEADS * HEAD_DIM
A_KV = A_KV_HEADS * HEAD_DIM
B_QK = B_HEADS * 2 * HEAD_DIM
B_V = B_HEADS * 2 * HEAD_DIM
C_HEADS = 4
C_DK = D_MODEL // 2 // C_HEADS
C_DV = D_MODEL // C_HEADS
C_GATE_RANK = 16
C_GATE_NORM = 16.0
C_CHUNK = 64
C_QK = C_HEADS * C_DK
C_V = C_HEADS * C_DV
D_FF = 2816
LANES = 128
MOD_ROWS = 8
VMEM_LIMIT = 56 * 1024 * 1024


def _cparams(sem):
    return pltpu.CompilerParams(dimension_semantics=sem, vmem_limit_bytes=VMEM_LIMIT)


def _dot(a, b):
    return jnp.dot(a, b, preferred_element_type=F32)


def _dot_nt(a, b):
    return lax.dot_general(a, b, (((1,), (1,)), ((), ())), preferred_element_type=F32)


def _dot_tn(a, b):
    return lax.dot_general(a, b, (((0,), (0,)), ((), ())), preferred_element_type=F32)


def _sigmoid(x):
    return 1.0 / (1.0 + jnp.exp(-x))


def _norm_mod(x, g, sc, sh):
    ms = jnp.mean(x * x, axis=-1, keepdims=True)
    return (x * lax.rsqrt(ms + EPS) * g) * (1.0 + sc) + sh


def _mod_kernel(c_ref, w_ref, b_ref, o_ref):
    c = c_ref[...]
    s = (c * _sigmoid(c)).astype(BF16)
    o_ref[...] = _dot(s, w_ref[...].astype(BF16)) + b_ref[...]


def _modulation(c_rows, mod_w, mod_b):
    d = D_MODEL
    return pl.pallas_call(
        _mod_kernel,
        grid=(DEPTH, 6),
        in_specs=[
            pl.BlockSpec((MOD_ROWS, d), lambda l, n: (0, 0)),
            pl.BlockSpec((None, d, d), lambda l, n: (l, 0, n)),
            pl.BlockSpec((None, 1, d), lambda l, n: (l, 0, n)),
        ],
        out_specs=pl.BlockSpec((None, MOD_ROWS, d), lambda l, n: (l, 0, n)),
        out_shape=jax.ShapeDtypeStruct((DEPTH, MOD_ROWS, 6 * d), F32),
        compiler_params=_cparams(("parallel", "parallel")),
        name="modulation",
    )(c_rows, mod_w, mod_b.reshape(DEPTH, 1, 6 * d))


def _mod_spec(k, row_of):
    return pl.BlockSpec((None, None, 1, D_MODEL), lambda i, *_: (row_of(i), k, 0, 0))


LOG2E = math.log2(math.e)
_Q_SCALE = HEAD_DIM ** -0.5 * LOG2E
_ATTN_GROUPS = (("aq", A_Q, True, _Q_SCALE), ("bq", B_QK, True, _Q_SCALE), ("bk", B_QK, True, 1.0),
                ("ak", A_KV, True, 1.0), ("av", A_KV, False, 1.0), ("bv", B_V, False, 1.0))


def _rope_chunk(v, cos, sin, first):
    partner = jnp.where(first, pltpu.roll(v, 96, 1), pltpu.roll(v, 32, 1))
    return v * cos + partner * sin


def _attn_inproj_kernel(x_ref, sh_ref, sc_ref, g_ref, w_ref, cos_ref, sin_ref, *out_refs, rope):
    xn = _norm_mod(x_ref[...], g_ref[...], sc_ref[...], sh_ref[...]).astype(BF16)
    tm = xn.shape[0]
    if rope:
        cos = cos_ref[...]
        sin = sin_ref[...]
        lane = lax.broadcasted_iota(jnp.int32, (tm, LANES), 1)
        first = (lane % HEAD_DIM) < (HEAD_DIM // 2)
    lo = lax.broadcasted_iota(jnp.int32, (1, LANES), 1) < HEAD_DIM
    col = 0
    outs = iter(out_refs)
    for name, width, roped, scale in _ATTN_GROUPS:
        y = _dot(xn, w_ref[:, col:col + width])
        col += width
        o_refs = [next(outs), next(outs)] if name == "bq" else [next(outs)]
        for c in range(width // LANES):
            v = y[:, c * LANES:(c + 1) * LANES]
            if rope and roped:
                v = _rope_chunk(v, cos, sin, first)
            if scale != 1.0:
                v = v * scale
            if name == "bq":
                o_refs[0][:, c * LANES:(c + 1) * LANES] = jnp.where(lo, v, 0.0).astype(BF16)
                o_refs[1][:, c * LANES:(c + 1) * LANES] = jnp.where(lo, 0.0, v).astype(BF16)
            else:
                o_refs[0][:, c * LANES:(c + 1) * LANES] = v.astype(BF16)


def _attn_inproj(x2d, mod, norm_g, w, cos_t, sin_t, *, tm, row_of, rope, tiles_per_seq):
    m, d = x2d.shape
    n_all = w.shape[1]
    widths = [width for (name, width, _, _) in _ATTN_GROUPS for _ in range(2 if name == "bq" else 1)]
    out_shape = [jax.ShapeDtypeStruct((m, width), BF16) for width in widths]
    out_specs = [pl.BlockSpec((tm, width), lambda i: (i, 0)) for width in widths]
    return pl.pallas_call(
        functools.partial(_attn_inproj_kernel, rope=rope),
        grid=(m // tm,),
        in_specs=[
            pl.BlockSpec((tm, d), lambda i: (i, 0)),
            _mod_spec(0, row_of), _mod_spec(1, row_of),
            pl.BlockSpec((1, d), lambda i: (0, 0)),
            pl.BlockSpec((d, n_all), lambda i: (0, 0)),
            pl.BlockSpec((tm, LANES), lambda i: (i % tiles_per_seq, 0)),
            pl.BlockSpec((tm, LANES), lambda i: (i % tiles_per_seq, 0)),
        ],
        out_specs=out_specs,
        out_shape=out_shape,
        compiler_params=_cparams(("parallel",)),
        name="attn_inproj_rope" if rope else "attn_inproj_ctx",
    )(x2d, mod, mod, norm_g, w, cos_t, sin_t)


GQA_QB = 8


def _gqa_kernel(sink_ref, q_ref, *refs, window, nb):
    tq = BLOCK
    qb = q_ref.shape[0] // tq
    if window:
        k_blk = [r[...] for r in refs[0:qb + 2]]
        v_blk = [r[...] for r in refs[qb + 2:2 * qb + 4]]
        kx_ref, vx_ref, o_ref = refs[2 * qb + 4:]
    else:
        kx_ref, vx_ref, o_ref = refs
    lane = lax.broadcasted_iota(jnp.int32, (1, LANES), 1)
    lo = lane < HEAD_DIM
    half = (jnp.where(lo, 1.0, 0.0).astype(F32), jnp.where(lo, 0.0, 1.0).astype(F32))
    row = lax.broadcasted_iota(jnp.int32, (LANES, 1), 0)
    sinks = [jnp.concatenate([jnp.full((1, tq), sink_ref[kv * A_GROUP + j] * LOG2E, F32)
                              for j in range(A_GROUP)], axis=1) for kv in range(A_KV_HEADS)]
    work = []
    for u in range(qb):
        if window:
            keys = jnp.concatenate(k_blk[u:u + 3] + [kx_ref[...]], axis=0)
            vals = jnp.concatenate(v_blk[u:u + 3] + [vx_ref[...]], axis=0)
        else:
            keys, vals = kx_ref[...], vx_ref[...]
        qf = [q_ref[u * tq:(u + 1) * tq, j * LANES:(j + 1) * LANES].astype(F32) for j in range(A_GROUP)]
        scs = [_dot_nt(keys, jnp.concatenate([(q * half[kv]).astype(BF16) for q in qf], axis=0))
               for kv in range(A_KV_HEADS)]
        work.append((vals, scs))
    for u, (vals, scs) in enumerate(work):
        if window:
            n = pl.program_id(1) * qb + u
            s = lax.broadcasted_iota(jnp.int32, (BLOCK, tq), 0)
            t = lax.broadcasted_iota(jnp.int32, (BLOCK, tq), 1)
            has_prev = jnp.where(n > 0, 1, 0)
            has_next = jnp.where(n < nb - 1, 1, 0)
            lower = t * has_prev + BLOCK * (1 - has_prev)
            upper = (t + 1) * has_next - 1
            ninf = jnp.float32(-jnp.inf)
            bias_prev = jnp.concatenate([jnp.where(s < lower, ninf, 0.0)] * A_GROUP, axis=1)
            bias_next = jnp.concatenate([jnp.where(s > upper, ninf, 0.0)] * A_GROUP, axis=1)
        outs = []
        for kv in range(A_KV_HEADS):
            sc = scs[kv]
            if window:
                sc = jnp.concatenate([sc[0:BLOCK] + bias_prev, sc[BLOCK:2 * BLOCK],
                                      sc[2 * BLOCK:3 * BLOCK] + bias_next, sc[3 * BLOCK:]], axis=0)
            sk = sinks[kv]
            mx = jnp.maximum(jnp.max(sc, axis=0, keepdims=True), sk)
            p = jnp.exp2(sc - mx)
            den = jnp.sum(p, axis=0, keepdims=True) + jnp.exp2(sk - mx)
            outs.append(_dot_tn(vals, p.astype(BF16)) * (1.0 / den))
        o_t = jnp.where(row < HEAD_DIM, outs[0], outs[1])
        for j in range(A_GROUP):
            o_ref[u * tq:(u + 1) * tq, j * LANES:(j + 1) * LANES] = o_t[:, j * tq:(j + 1) * tq].T.astype(BF16)


def _gqa_window(sink, aq, ak, av, cak, cav):
    bsz, n_tok, _ = aq.shape
    n_ctx = cak.shape[1]
    nb = n_tok // BLOCK
    qb = math.gcd(nb, GQA_QB)
    kv_blocks = [pl.BlockSpec((None, BLOCK, A_KV),
                              lambda b, n, off=off: (b, jnp.clip(n * qb + off, 0, nb - 1), 0))
                 for off in range(-1, qb + 1)]
    kv_ctx = pl.BlockSpec((None, n_ctx, A_KV), lambda b, n: (b, 0, 0))
    return pl.pallas_call(
        functools.partial(_gqa_kernel, window=True, nb=nb),
        grid=(bsz, nb // qb),
        in_specs=[pl.BlockSpec(memory_space=pltpu.SMEM),
                  pl.BlockSpec((None, qb * BLOCK, A_Q), lambda b, n: (b, n, 0))]
        + kv_blocks + kv_blocks + [kv_ctx, kv_ctx],
        out_specs=pl.BlockSpec((None, qb * BLOCK, A_Q), lambda b, n: (b, n, 0)),
        out_shape=jax.ShapeDtypeStruct((bsz, n_tok, A_Q), BF16),
        compiler_params=_cparams(("parallel", "parallel")),
        name="gqa_window",
    )(sink, aq, *([ak] * (qb + 2)), *([av] * (qb + 2)), cak, cav)


def _gqa_context(sink, caq, cak, cav):
    bsz, n_ctx, _ = caq.shape
    nb = n_ctx // BLOCK
    kv_ctx = pl.BlockSpec((None, n_ctx, A_KV), lambda b, n: (b, 0, 0))
    return pl.pallas_call(
        functools.partial(_gqa_kernel, window=False, nb=nb),
        grid=(bsz, nb),
        in_specs=[pl.BlockSpec(memory_space=pltpu.SMEM),
                  pl.BlockSpec((None, BLOCK, A_Q), lambda b, n: (b, n, 0)),
                  kv_ctx, kv_ctx],
        out_specs=pl.BlockSpec((None, BLOCK, A_Q), lambda b, n: (b, n, 0)),
        out_shape=jax.ShapeDtypeStruct((bsz, n_ctx, A_Q), BF16),
        compiler_params=_cparams(("parallel", "parallel")),
        name="gqa_context",
    )(sink, caq, cak, cav)


LAG_LIMIT = 12.0
XPOSE_ROWS = 512
ONES_ROWS = 16


def _diff_attn_kernel(lam_ref, g_ref, q0_ref, q1_ref, *refs, nseg, lam_init, kc, lagged):
    k_refs = refs[0:2 * nseg:2]
    v_refs = refs[1:2 * nseg:2]
    if lagged:
        o_ref, ex_ref, vt_ref = refs[2 * nseg:]
    else:
        o_ref, vt_ref = refs[2 * nseg:]
    lv = lam_ref[...]
    lam = (jnp.exp(jnp.sum(lv[0:1] * lv[1:2], axis=-1, keepdims=True))
           - jnp.exp(jnp.sum(lv[2:3] * lv[3:4], axis=-1, keepdims=True)) + lam_init)
    q0 = q0_ref[...]
    q1 = q1_ref[...]
    @pl.when(pl.program_id(2) == 0)
    def _():
        off = 0
        for v_ref in v_refs:
            ns = v_ref.shape[0]
            for c0 in range(0, ns, XPOSE_ROWS):
                n = min(XPOSE_ROWS, ns - c0)
                vt_ref[0:LANES, off + c0:off + c0 + n] = v_ref[c0:c0 + n, :].astype(F32).T.astype(BF16)
            off += ns
        vt_ref[LANES:, :] = jnp.ones((vt_ref.shape[0] - LANES, vt_ref.shape[1]), BF16)

    chunks = []
    off = 0
    for k_ref in k_refs:
        ns = k_ref.shape[0]
        step = min(kc, ns)
        chunks += [(k_ref, c0, step, off + c0) for c0 in range(0, ns, step)]
        off += ns

    qms = (q0, q1)

    def scores(mi, ci):
        k_ref, c0, step, _ = chunks[ci]
        return _dot_nt(k_ref[c0:c0 + step, :], qms[mi])

    run_max = [None, None]
    accs = [None, None]
    excess = [None, None]

    def probs(mi, sc):
        mc = jnp.max(sc, axis=0, keepdims=True)
        if run_max[mi] is None:
            run_max[mi], alpha = (mc, mc), None
        else:
            ref_old, best = run_max[mi]
            new_best = jnp.maximum(best, mc)
            ref = best if lagged else new_best
            alpha = jnp.exp2(ref_old - ref)
            if lagged:
                over = mc - ref
                excess[mi] = over if excess[mi] is None else jnp.maximum(excess[mi], over)
            run_max[mi] = (ref, new_best)
        return jnp.exp2((sc - run_max[mi][0]).astype(BF16)), alpha

    def accumulate(mi, ci, pb, alpha):
        _, _, step, g0 = chunks[ci]
        pv = _dot(vt_ref[:, g0:g0 + step], pb)
        accs[mi] = pv if alpha is None else accs[mi] * alpha + pv

    n_ch = len(chunks)
    sc_q = {ci: [scores(mi, ci) for mi in range(2)] for ci in range(min(2, n_ch))}
    pb_q = {0: [probs(mi, sc_q[0][mi]) for mi in range(2)]}
    for ci in range(n_ch):
        if ci + 2 < n_ch:
            sc_q[ci + 2] = [scores(mi, ci + 2) for mi in range(2)]
        if ci + 1 < n_ch:
            sc_pair = sc_q.pop(ci + 1)
            pb_q[ci + 1] = [probs(mi, sc_pair[mi]) for mi in range(2)]
        for mi, (pb, alpha) in enumerate(pb_q.pop(ci)):
            accumulate(mi, ci, pb, alpha)
    r0 = 1.0 / accs[0][LANES:LANES + 1, :]
    r1 = lam / accs[1][LANES:LANES + 1, :]
    o = (accs[0][0:LANES, :] * r0 - accs[1][0:LANES, :] * r1).T
    ms = jnp.mean(o * o, axis=-1, keepdims=True)
    o_ref[...] = ((o * lax.rsqrt(ms + EPS) * g_ref[...]) * (1.0 - lam_init)).astype(BF16)
    if lagged:
        ex_ref[...] = jnp.maximum(excess[0], excess[1])


def _diff_attn(lam_vec, subln_g, q0, q1, kv_list, *, tq, lam_init, kc=256, lagged=False):
    bsz, n_q, _ = q0.shape
    out_specs = [pl.BlockSpec((None, tq, LANES), lambda b, h, i: (b, i, h))]
    out_shape = [jax.ShapeDtypeStruct((bsz, n_q, B_V), BF16)]
    if lagged:
        out_specs.append(pl.BlockSpec((None, None, 1, tq), lambda b, h, i: (b, h, 0, i)))
        out_shape.append(jax.ShapeDtypeStruct((bsz, B_HEADS, 1, n_q), F32))
    in_specs = [pl.BlockSpec((4, HEAD_DIM), lambda b, h, i: (0, 0)),
                pl.BlockSpec((1, LANES), lambda b, h, i: (0, 0)),
                pl.BlockSpec((None, tq, LANES), lambda b, h, i: (b, i, h)),
                pl.BlockSpec((None, tq, LANES), lambda b, h, i: (b, i, h))]
    args = [lam_vec, subln_g, q0, q1]
    for k, v in kv_list:
        ns = k.shape[1]
        in_specs += [pl.BlockSpec((None, ns, LANES), lambda b, h, i: (b, 0, h))] * 2
        args += [k, v]
    return pl.pallas_call(
        functools.partial(_diff_attn_kernel, nseg=len(kv_list), lam_init=lam_init, kc=kc, lagged=lagged),
        grid=(bsz, B_HEADS, n_q // tq),
        in_specs=in_specs,
        out_specs=out_specs,
        out_shape=out_shape,
        scratch_shapes=[pltpu.VMEM((LANES + ONES_ROWS, sum(k.shape[1] for k, _ in kv_list)), BF16)],
        compiler_params=_cparams(("parallel", "parallel", "arbitrary")),
        name="diff_attn_%dseg%s" % (len(kv_list), "_lagged" if lagged else ""),
    )(*args)


FFN_HALO = 16
FFN_TF = 256


def _ffn_kernel(*refs, n_mix, tiles_per_seq, final_norm):
    a_refs = refs[:3 * n_mix]
    wo_refs = refs[3 * n_mix:4 * n_mix]
    (h_ref, hp_ref, hn_ref, g1_ref, sh_ref, sc_ref, gt_ref, ng_ref, wup_ref, cwu_ref, cwg_ref,
     wd_ref, fg_ref, o_ref, xn_ref, hu_a, hg_a, hu_b, hg_b, acc_ref) = refs[4 * n_mix:]
    i = pl.program_id(0)
    tm = h_ref.shape[0]
    hl = FFN_HALO
    nj, tf = wd_ref.shape[0], wd_ref.shape[1]

    ext = tm + 2 * hl
    half = ext // 2
    pos = i % tiles_per_seq
    a_ext = [jnp.concatenate([a_refs[3 * m + 1][...], a_refs[3 * m][...], a_refs[3 * m + 2][...]], axis=0)
             for m in range(n_mix)]
    h_ext = jnp.concatenate([hp_ref[...], h_ref[...], hn_ref[...]], axis=0)
    ys = []
    for blk in range(2):
        rows = slice(blk * half, (blk + 1) * half)
        y = None
        for m in range(n_mix):
            part = _dot(a_ext[m][rows, :], wo_refs[m][...])
            y = part if y is None else y + part
        ys.append(y)
    for blk in range(2):
        h1 = h_ext[blk * half:(blk + 1) * half, :] + g1_ref[...] * ys[blk]
        xn = _norm_mod(h1, ng_ref[...], sc_ref[...], sh_ref[...])
        if blk == 0:
            o_ref[0:half - hl, :] = h1[hl:, :]
            xn_ref[0:hl, :] = (xn[0:hl, :] * jnp.where(pos > 0, 1.0, 0.0)).astype(BF16)
            xn_ref[hl:half, :] = xn[hl:, :].astype(BF16)
        else:
            o_ref[half - hl:, :] = h1[0:half - hl, :]
            xn_ref[half:ext - hl, :] = xn[0:half - hl, :].astype(BF16)
            xn_ref[ext - hl:, :] = (xn[half - hl:, :] * jnp.where(pos < tiles_per_seq - 1, 1.0, 0.0)).astype(BF16)
    acc_ref[...] = jnp.zeros_like(acc_ref)

    def up(j, hu_ref, hg_ref, row_blocks=1):
        cols = pl.ds(pl.multiple_of(j * tf, tf), tf)
        gate_cols = pl.ds(pl.multiple_of(nj * tf + j * tf, tf), tf)
        for blk in range(row_blocks):
            rows = slice(blk * ext // row_blocks, (blk + 1) * ext // row_blocks)
            xn = xn_ref[rows, :]
            hu_ref[rows, :] = _dot(xn, wup_ref[:, cols])
            hg_ref[rows, :] = _dot(xn, wup_ref[:, gate_cols])

    def conv(ref, cw):
        return (ref[hl - 1:hl - 1 + tm, :] * cw[0:1] + ref[hl:hl + tm, :] * cw[1:2]
                + ref[hl + 1:hl + 1 + tm, :] * cw[2:3] + cw[3:4])

    def act(j, hu_ref, hg_ref):
        u = conv(hu_ref, cwu_ref[j])
        gt = conv(hg_ref, cwg_ref[j])
        return (gt * _sigmoid(gt) * u).astype(BF16)

    up(0, hu_a, hg_a, row_blocks=2)

    def pair(jj, carry):
        j = 2 * jj
        up(j + 1, hu_b, hg_b)
        acc_ref[...] += _dot(act(j, hu_a, hg_a), wd_ref[j])
        up(j + 2, hu_a, hg_a)
        acc_ref[...] += _dot(act(j + 1, hu_b, hg_b), wd_ref[j + 1])
        return carry

    assert nj % 2 == 1
    lax.fori_loop(0, (nj - 1) // 2, pair, 0)
    acc_ref[...] += _dot(act(nj - 1, hu_a, hg_a), wd_ref[nj - 1])

    y = o_ref[...] + gt_ref[...] * acc_ref[...]
    if final_norm:
        ms = jnp.mean(y * y, axis=-1, keepdims=True)
        y = y * lax.rsqrt(ms + EPS) * fg_ref[...]
    o_ref[...] = y


def _ffn_weights(w_up, conv_w, conv_b, w_down, tf):
    depth, d = w_up.shape[0], w_up.shape[1]
    nj = D_FF // tf
    chunked = lambda w: w.reshape(depth, w.shape[1], nj, tf).transpose(0, 2, 1, 3)
    cw = jnp.concatenate([conv_w, conv_b[:, None, :]], axis=1)
    return (w_up.astype(BF16), chunked(cw[:, :, :D_FF]), chunked(cw[:, :, D_FF:]),
            w_down.astype(BF16).reshape(depth, nj, tf, d))


def _mixer_out_ffn(acts, w_outs, h2d, mod, norm_g, weights, layer, final_g, *, tm, row_of, tiles_per_seq,
                   final_norm):
    m, d = h2d.shape
    hl = FFN_HALO
    nhb = m // hl
    r = tm // hl
    w_up, cwu, cwg, w_dn = weights
    tf = w_dn.shape[2]
    resident = lambda a: pl.BlockSpec(a.shape, lambda i: (0,) * a.ndim, pipeline_mode=pl.Buffered(1))
    of_layer = lambda a: pl.BlockSpec((None,) + a.shape[1:], lambda i: (layer,) + (0,) * (a.ndim - 1),
                                      pipeline_mode=pl.Buffered(1))

    def tile_and_halos(width):
        return [pl.BlockSpec((tm, width), lambda i: (i, 0)),
                pl.BlockSpec((hl, width), lambda i: (jnp.maximum(i * r - 1, 0), 0)),
                pl.BlockSpec((hl, width), lambda i: (jnp.minimum((i + 1) * r, nhb - 1), 0))]

    in_specs, args = [], []
    for a in acts:
        in_specs += tile_and_halos(a.shape[1])
        args += [a, a, a]
    in_specs += [resident(w) for w in w_outs]
    args += list(w_outs)
    in_specs += tile_and_halos(d)
    in_specs += [_mod_spec(2, row_of), _mod_spec(3, row_of), _mod_spec(4, row_of), _mod_spec(5, row_of),
                 pl.BlockSpec((1, d), lambda i: (0, 0)),
                 of_layer(w_up), of_layer(cwu), of_layer(cwg), of_layer(w_dn),
                 pl.BlockSpec((1, d), lambda i: (0, 0))]
    args += [h2d, h2d, h2d, mod, mod, mod, mod, norm_g, w_up, cwu, cwg, w_dn, final_g]
    return pl.pallas_call(
        functools.partial(_ffn_kernel, n_mix=len(acts), tiles_per_seq=tiles_per_seq, final_norm=final_norm),
        grid=(m // tm,),
        in_specs=in_specs,
        out_specs=pl.BlockSpec((tm, d), lambda i: (i, 0)),
        out_shape=jax.ShapeDtypeStruct((m, d), F32),
        scratch_shapes=[pltpu.VMEM((tm + 2 * hl, d), BF16)]
        + [pltpu.VMEM((tm + 2 * hl, tf), F32)] * 4
        + [pltpu.VMEM((tm, d), F32)],
        compiler_params=_cparams(("parallel",)),
        name="mixer_out_conv_ffn",
    )(*args)


def _gla_inproj_kernel(x_ref, sh_ref, sc_ref, g_ref, w_ref, w2_ref, gb_ref,
                       q_ref, k_ref, v_ref, sg_ref, laf_ref, lab_ref):
    tm = x_ref.shape[0]
    n_split = 2 if tm % 32 == 0 else 1
    blocks = [slice(s * tm // n_split, (s + 1) * tm // n_split) for s in range(n_split)]
    xns = [_norm_mod(x_ref[rows, :], g_ref[...], sc_ref[...], sh_ref[...]).astype(BF16) for rows in blocks]
    for rows, xn in zip(blocks, xns):
        q_ref[rows, :] = (_dot(xn, w_ref[:, 0:C_QK]) * (C_DK ** -0.5)).astype(BF16)
        k_ref[rows, :] = _dot(xn, w_ref[:, C_QK:2 * C_QK]).astype(BF16)
        for c in range(C_V // C_QK):
            lo = 2 * C_QK + c * C_QK
            v_ref[rows, c * C_QK:(c + 1) * C_QK] = _dot(xn, w_ref[:, lo:lo + C_QK]).astype(BF16)
        for c in range(C_V // C_QK):
            lo = 2 * C_QK + C_V + c * C_QK
            gg = _dot(xn, w_ref[:, lo:lo + C_QK])
            sg_ref[rows, c * C_QK:(c + 1) * C_QK] = (gg * _sigmoid(gg)).astype(BF16)
        lo = 2 * C_QK + 2 * C_V
        r = _dot(xn, w_ref[:, lo:lo + LANES]).astype(BF16)
        for dr, la_ref in enumerate((laf_ref, lab_ref)):
            z = _dot(r, w2_ref[:, dr * C_QK:(dr + 1) * C_QK]) + gb_ref[:, dr * C_QK:(dr + 1) * C_QK]
            la_ref[rows, :] = (jnp.minimum(z, 0.0) - jnp.log(1.0 + jnp.exp(-jnp.abs(z)))) * (1.0 / C_GATE_NORM)


def _gla_inproj(x2d, mod, norm_g, w, w2, gb, *, tm, row_of):
    m, d = x2d.shape
    widths = (C_QK, C_QK, C_V, C_V, C_QK, C_QK)
    dts = (BF16, BF16, BF16, BF16, F32, F32)
    return pl.pallas_call(
        _gla_inproj_kernel,
        grid=(m // tm,),
        in_specs=[
            pl.BlockSpec((tm, d), lambda i: (i, 0)),
            _mod_spec(0, row_of), _mod_spec(1, row_of),
            pl.BlockSpec((1, d), lambda i: (0, 0)),
            pl.BlockSpec(w.shape, lambda i: (0, 0)),
            pl.BlockSpec(w2.shape, lambda i: (0, 0)),
            pl.BlockSpec(gb.shape, lambda i: (0, 0)),
        ],
        out_specs=[pl.BlockSpec((tm, wd), lambda i: (i, 0)) for wd in widths],
        out_shape=[jax.ShapeDtypeStruct((m, wd), dt) for wd, dt in zip(widths, dts)],
        compiler_params=_cparams(("parallel",)),
        name="gla_inproj",
    )(x2d, mod, mod, norm_g, w, w2, gb)


def _tri(n, reverse):
    r = lax.broadcasted_iota(jnp.int32, (n, n), 0)
    c = lax.broadcasted_iota(jnp.int32, (n, n), 1)
    return (c >= r) if reverse else (c <= r)


def _cumsum_rows(la, tri_bf):
    hi = la.astype(BF16)
    r1 = la - hi.astype(F32)
    mid = r1.astype(BF16)
    lo = (r1 - mid.astype(F32)).astype(BF16)
    return _dot(tri_bf, hi) + _dot(tri_bf, mid) + _dot(tri_bf, lo)


def _gla_ctx_state_kernel(k_ref, v_ref, laf_ref, lab_ref, sf_ref, sb_ref):
    n = k_ref.shape[0]
    for reverse, la_ref, s_ref in ((False, laf_ref, sf_ref), (True, lab_ref, sb_ref)):
        tri = jnp.where(_tri(n, reverse), 1.0, 0.0).astype(BF16)
        b = _cumsum_rows(la_ref[...], tri)
        b_end = b[0:1, :] if reverse else b[n - 1:n, :]
        kw = (k_ref[...].astype(F32) * jnp.exp(b_end - b)).astype(BF16)
        for h in range(C_HEADS):
            s_ref[h] = _dot_tn(v_ref[:, h * C_DV:(h + 1) * C_DV], kw[:, h * C_DK:(h + 1) * C_DK])


def _gla_ctx_state(kc, vc, lac_f, lac_b):
    bsz, n, _ = kc.shape
    s_shape = jax.ShapeDtypeStruct((bsz, C_HEADS, C_DV, C_DK), F32)
    s_spec = pl.BlockSpec((None, C_HEADS, C_DV, C_DK), lambda b: (b, 0, 0, 0))
    return pl.pallas_call(
        _gla_ctx_state_kernel,
        grid=(bsz,),
        in_specs=[pl.BlockSpec((None, n, C_QK), lambda b: (b, 0, 0)),
                  pl.BlockSpec((None, n, C_V), lambda b: (b, 0, 0)),
                  pl.BlockSpec((None, n, C_QK), lambda b: (b, 0, 0)),
                  pl.BlockSpec((None, n, C_QK), lambda b: (b, 0, 0))],
        out_specs=[s_spec, s_spec],
        out_shape=[s_shape, s_shape],
        compiler_params=_cparams(("parallel",)),
        name="gla_ctx_state",
    )(kc, vc, lac_f, lac_b)


GLA_BATCH_BLOCK = 4


def _gla_scan_kernel(q_ref, k_ref, v_ref, la_ref, s0_ref, *refs, reverse, final):
    if final:
        ob_ref, sg_ref, ng_ref, o_ref, st_ref = refs
    else:
        o_ref, st_ref = refs
    gidx = pl.program_id(1)

    @pl.when(gidx == 0)
    def _():
        st_ref[...] = s0_ref[...]

    nbb, gt = q_ref.shape[0], q_ref.shape[1]
    nchunk = gt // C_CHUNK
    c = C_CHUNK
    r = lax.broadcasted_iota(jnp.int32, (gt, gt), 0)
    s = lax.broadcasted_iota(jnp.int32, (gt, gt), 1)
    shift = c.bit_length() - 1
    same_chunk = (r >> shift) == (s >> shift)
    tri = same_chunk & ((s >= r) if reverse else (s <= r))
    tri_bf = jnp.where(tri, 1.0, 0.0).astype(BF16)

    def per_chunk_row(x, row):
        return jnp.concatenate([jnp.broadcast_to(x[ci * c + row:ci * c + row + 1, :], (c, x.shape[1]))
                                for ci in range(nchunk)], axis=0)

    order = range(nchunk - 1, -1, -1) if reverse else range(nchunk)
    kcols = [slice(h * C_DK, (h + 1) * C_DK) for h in range(C_HEADS)]
    vcols_of = [slice(h * C_DV, (h + 1) * C_DV) for h in range(C_HEADS)]
    streams = [(bi, h) for bi in range(nbb) for h in range(C_HEADS)]
    q_out, k_out, decay, o_intra, upd = {}, {}, {}, {}, {}
    for bi in range(nbb):
        b = _cumsum_rows(la_ref[bi], tri_bf)
        b_mid = per_chunk_row(b, c // 2)
        b_end = per_chunk_row(b, 0 if reverse else c - 1)
        qf = q_ref[bi].astype(F32)
        kf = k_ref[bi].astype(F32)
        q_in = (qf * jnp.exp(b - b_mid)).astype(BF16)
        k_in = (kf * jnp.exp(b_mid - b)).astype(BF16)
        q_out[bi] = (qf * jnp.exp(b)).astype(BF16)
        k_out[bi] = (kf * jnp.exp(b_end - b)).astype(BF16)
        decay[bi] = jnp.exp(b_end)
        for h in range(C_HEADS):
            vv = v_ref[bi, :, vcols_of[h]]
            sc = jnp.where(tri, _dot_nt(q_in[:, kcols[h]], k_in[:, kcols[h]]), 0.0).astype(BF16)
            o_intra[bi, h] = _dot(sc, vv)
            upd[bi, h] = {ci: _dot_tn(vv[ci * c:(ci + 1) * c, :], k_out[bi][ci * c:(ci + 1) * c, kcols[h]])
                          for ci in order}
    st = {s_: st_ref[s_[0], s_[1]] for s_ in streams}
    o_inter = {s_: {} for s_ in streams}
    for ci in order:
        rows = slice(ci * c, (ci + 1) * c)
        for bi, h in streams:
            o_inter[bi, h][ci] = _dot_nt(q_out[bi][rows, kcols[h]], st[bi, h].astype(BF16))
            st[bi, h] = st[bi, h] * decay[bi][ci * c:ci * c + 1, kcols[h]] + upd[bi, h][ci]
    for bi, h in streams:
        vcols = vcols_of[h]
        st_ref[bi, h] = st[bi, h]
        o = o_intra[bi, h] + jnp.concatenate([o_inter[bi, h][ci] for ci in range(nchunk)], axis=0)
        if final:
            o = o + ob_ref[bi, :, vcols]
            ms = jnp.mean(o * o, axis=-1, keepdims=True)
            o = (o * lax.rsqrt(ms + EPS) * ng_ref[...]) * sg_ref[bi, :, vcols].astype(F32)
            o_ref[bi, :, vcols] = o.astype(BF16)
        else:
            o_ref[bi, :, vcols] = o


def _gla_scan(q, k, v, la, s0, *, gt, reverse, o_other=None, sg=None, norm_g=None):
    bsz, n_tok, _ = q.shape
    ng = n_tok // gt
    final = o_other is not None
    nbb = GLA_BATCH_BLOCK if bsz % GLA_BATCH_BLOCK == 0 else 1
    gi = (lambda b, g: (b, ng - 1 - g, 0)) if reverse else (lambda b, g: (b, g, 0))
    in_specs = [pl.BlockSpec((nbb, gt, C_QK), gi), pl.BlockSpec((nbb, gt, C_QK), gi),
                pl.BlockSpec((nbb, gt, C_V), gi), pl.BlockSpec((nbb, gt, C_QK), gi),
                pl.BlockSpec((nbb, C_HEADS, C_DV, C_DK), lambda b, g: (b, 0, 0, 0))]
    args = [q, k, v, la, s0]
    if final:
        in_specs += [pl.BlockSpec((nbb, gt, C_V), gi), pl.BlockSpec((nbb, gt, C_V), gi),
                     pl.BlockSpec((1, C_DV), lambda b, g: (0, 0))]
        args += [o_other, sg, norm_g]
    return pl.pallas_call(
        functools.partial(_gla_scan_kernel, reverse=reverse, final=final),
        grid=(bsz // nbb, ng),
        in_specs=in_specs,
        out_specs=pl.BlockSpec((nbb, gt, C_V), gi),
        out_shape=jax.ShapeDtypeStruct((bsz, n_tok, C_V), BF16 if final else F32),
        scratch_shapes=[pltpu.VMEM((nbb, C_HEADS, C_DV, C_DK), F32)],
        compiler_params=_cparams(("parallel", "arbitrary")),
        name="gla_scan_fwd_final" if final else "gla_scan_bwd",
    )(*args)


def _pair_split(n_heads):
    base = np.concatenate([np.arange(0, HEAD_DIM, 2), np.arange(1, HEAD_DIM, 2)])
    return np.concatenate([h * HEAD_DIM + base for h in range(n_heads)])


_A_HEAD_ORDER = np.array([kv * A_GROUP + j for j in range(A_GROUP) for kv in range(A_KV_HEADS)])


def _attn_in_cols():
    aq = (_A_HEAD_ORDER[:, None] * HEAD_DIM + _pair_split(1)[None, :]).reshape(-1)
    o_ak, o_av, o_bq = A_Q, A_Q + A_KV, A_Q + 2 * A_KV
    o_bk, o_bv = o_bq + B_QK, o_bq + 2 * B_QK
    return np.concatenate([aq, o_bq + _pair_split(2 * B_HEADS), o_bk + _pair_split(2 * B_HEADS),
                           o_ak + _pair_split(A_KV_HEADS), o_av + np.arange(A_KV), o_bv + np.arange(B_V)])


def _attn_out_rows():
    oa = (_A_HEAD_ORDER[:, None] * HEAD_DIM + np.arange(HEAD_DIM)[None, :]).reshape(-1)
    return oa


def _rope_tables(n_tok):
    rows = n_tok // GRID_W
    row = jnp.repeat(jnp.arange(rows, dtype=F32), GRID_W)
    col = jnp.tile(jnp.arange(GRID_W, dtype=F32), rows)
    axis_dim = HEAD_DIM // 2
    inv_freq = ROPE_THETA ** (-jnp.arange(0, axis_dim, 2, dtype=F32) / axis_dim)
    ang = jnp.concatenate([row[:, None] * inv_freq, col[:, None] * inv_freq], axis=-1)
    cos, sin = jnp.cos(ang), jnp.sin(ang)
    cos_t = jnp.tile(cos, (1, LANES // (HEAD_DIM // 2)))
    sin_t = jnp.tile(jnp.concatenate([-sin, sin], axis=-1), (1, LANES // HEAD_DIM))
    return cos_t, sin_t


def _pick(n, pref):
    return pref if n % pref == 0 else n


def kernel(x, c, ctx, c_ctx, mod_w, mod_b, norm1_g, norm2_g, attn_w_in, attn_w_out, attn_sink, diff_lambda, diff_subln_g, gla_w_in, gla_gate_w1, gla_gate_w2, gla_gate_b, gla_norm_g, gla_w_out, ffn_w_up, ffn_conv_w, ffn_conv_b, ffn_w_down, final_norm_g):
    bsz, n_tok, d = x.shape
    n_ctx = ctx.shape[1]
    assert d == D_MODEL and bsz + 1 <= MOD_ROWS
    m_lat, m_ctx = bsz * n_tok, bsz * n_ctx

    c_rows = jnp.concatenate([c, c_ctx[None, :], jnp.zeros((MOD_ROWS - bsz - 1, d), F32)], axis=0)
    mod_all = _modulation(c_rows, mod_w, mod_b)
    cos_t, sin_t = _rope_tables(n_tok)

    tm = _pick(n_tok, 1024)
    tmc = _pick(n_ctx, 256)
    lat_tiles = n_tok // tm
    lat_row = lambda i: i // lat_tiles
    ctx_row = lambda i: bsz
    tm_ffn = _pick(n_tok, 1024)
    ffn_row = lambda i: i // (n_tok // tm_ffn)

    h = x.reshape(m_lat, d)
    hc = ctx.reshape(m_ctx, d)
    ffn_w = _ffn_weights(ffn_w_up, ffn_conv_w, ffn_conv_b, ffn_w_down, FFN_TF)
    for layer in range(DEPTH):
        need_ctx = layer < DEPTH - 1
        last = layer == DEPTH - 1
        mod = mod_all[layer].reshape(MOD_ROWS, 6, 1, d)
        n1 = norm1_g[layer].reshape(1, d)
        n2 = norm2_g[layer].reshape(1, d)
        i = layer // 2
        if layer % 2 == 0:
            lam_init = 0.8 - 0.6 * math.exp(-B_LAMBDA_DECAY * layer)
            w_in = attn_w_in[i][:, _attn_in_cols()].astype(BF16)
            w_out = attn_w_out[i]
            w_oa = w_out[_attn_out_rows()].astype(BF16)
            w_ob = w_out[A_Q:].astype(BF16)
            sink = attn_sink[i]
            subln = diff_subln_g[i].reshape(1, LANES)
            aq, bq0, bq1, bk, ak, av, bv = _attn_inproj(h, mod, n1, w_in, cos_t, sin_t, tm=tm, row_of=lat_row,
                                                        rope=True, tiles_per_seq=lat_tiles)
            caq, cbq0, cbq1, cbk, cak, cav, cbv = _attn_inproj(hc, mod, n1, w_in, cos_t, sin_t, tm=tmc,
                                                               row_of=ctx_row, rope=False, tiles_per_seq=1)
            r3 = lambda a, n: a.reshape(bsz, n, a.shape[-1])
            cak3, cav3, cbk3, cbv3 = r3(cak, n_ctx), r3(cav, n_ctx), r3(cbk, n_ctx), r3(cbv, n_ctx)
            oa = _gqa_window(sink, r3(aq, n_tok), r3(ak, n_tok), r3(av, n_tok), cak3, cav3)
            b_args = (diff_lambda[i], subln, r3(bq0, n_tok), r3(bq1, n_tok),
                      [(r3(bk, n_tok), r3(bv, n_tok)), (cbk3, cbv3)])
            b_kw = dict(tq=_pick(n_tok, 1024), lam_init=lam_init)
            ob_lagged, over = _diff_attn(*b_args, lagged=True, **b_kw)
            ob = lax.cond(jnp.max(over) <= LAG_LIMIT, lambda: ob_lagged,
                          lambda: _diff_attn(*b_args, **b_kw)[0])
            mix, w_mix = [oa.reshape(m_lat, A_Q), ob.reshape(m_lat, B_V)], [w_oa, w_ob]
            if need_ctx:
                oca = _gqa_context(sink, r3(caq, n_ctx), cak3, cav3)
                ocb = _diff_attn(diff_lambda[i], subln, r3(cbq0, n_ctx), r3(cbq1, n_ctx), [(cbk3, cbv3)],
                                 tq=_pick(n_ctx, 256), lam_init=lam_init)[0]
                mix_c = [oca.reshape(m_ctx, A_Q), ocb.reshape(m_ctx, B_V)]
        else:
            w1 = gla_gate_w1[i]
            pad = jnp.zeros((d, LANES - 2 * C_GATE_RANK), F32)
            w_in = jnp.concatenate([gla_w_in[i], w1[0], w1[1], pad], axis=1).astype(BF16)
            w2 = gla_gate_w2[i]
            w2bd = jnp.zeros((LANES, 2 * C_QK), F32)
            w2bd = w2bd.at[0:C_GATE_RANK, 0:C_QK].set(w2[0]).at[C_GATE_RANK:2 * C_GATE_RANK, C_QK:].set(w2[1])
            w2bd = w2bd.astype(BF16)
            gb = gla_gate_b[i].reshape(1, 2 * C_QK)
            ng = gla_norm_g[i].reshape(1, C_DV)
            q, k, v, sg, la_f, la_b = _gla_inproj(h, mod, n1, w_in, w2bd, gb, tm=tm, row_of=lat_row)
            qc, kc, vc, sgc, lac_f, lac_b = _gla_inproj(hc, mod, n1, w_in, w2bd, gb, tm=tmc, row_of=ctx_row)
            r3 = lambda a, n: a.reshape(bsz, n, a.shape[-1])
            s_f, s_b = _gla_ctx_state(r3(kc, n_ctx), r3(vc, n_ctx), r3(lac_f, n_ctx), r3(lac_b, n_ctx))
            gt = _pick(n_tok, 256)
            q3, k3, v3 = r3(q, n_tok), r3(k, n_tok), r3(v, n_tok)
            o_b = _gla_scan(q3, k3, v3, r3(la_b, n_tok), s_b, gt=gt, reverse=True)
            og = _gla_scan(q3, k3, v3, r3(la_f, n_tok), s_f, gt=gt, reverse=False,
                           o_other=o_b, sg=r3(sg, n_tok), norm_g=ng)
            mix, w_mix = [og.reshape(m_lat, C_V)], [gla_w_out[i].astype(BF16)]
            if need_ctx:
                z = jnp.zeros((bsz, C_HEADS, C_DV, C_DK), F32)
                qc3, kc3, vc3 = r3(qc, n_ctx), r3(kc, n_ctx), r3(vc, n_ctx)
                gtc = _pick(n_ctx, 256)
                oc_b = _gla_scan(qc3, kc3, vc3, r3(lac_b, n_ctx), z, gt=gtc, reverse=True)
                ogc = _gla_scan(qc3, kc3, vc3, r3(lac_f, n_ctx), z, gt=gtc, reverse=False,
                                o_other=oc_b, sg=r3(sgc, n_ctx), norm_g=ng)
                mix_c = [ogc.reshape(m_ctx, C_V)]
        fg = final_norm_g.reshape(1, d)
        h = _mixer_out_ffn(mix, w_mix, h, mod, n2, ffn_w, layer, fg, tm=tm_ffn, row_of=ffn_row,
                           tiles_per_seq=n_tok // tm_ffn, final_norm=last)
        if need_ctx:
            hc = _mixer_out_ffn(mix_c, w_mix, hc, mod, n2, ffn_w, layer, fg, tm=tmc, row_of=ctx_row,
                                tiles_per_seq=n_ctx // tmc, final_norm=False)
    return h.reshape(bsz, n_tok, d)
```

```python
import functools
import math

import jax
import jax.numpy as jnp
from jax import lax
from jax.experimental import pallas as pl
from jax.experimental.pallas import tpu as pltpu

F32 = jnp.float32
BF16 = jnp.bfloat16

D_MODEL = 1024
DEPTH = 2
GRID_W = 64
HEAD_DIM = 64
ROPE_THETA = 10000.0
EPS = 1e-6
BLOCK = 128
A_HEADS = 8
A_KV_HEADS = 2
A_GROUP = A_HEADS // A_KV_HEADS
B_HEADS = 4
B_LAMBDA_DECAY = 0.3
A_Q = A_HEADS * HEAD_DIM
A_KV = A_KV_HEADS * HEAD_DIM
B_QK = B_HEADS * 2 * HEAD_DIM
B_V = B_HEADS * 2 * HEAD_DIM
C_HEADS = 4
C_DK = D_MODEL // 2 // C_HEADS
C_DV = D_MODEL // C_HEADS
C_GATE_RANK = 16
C_GATE_NORM = 16.0
C_CHUNK = 64
C_QK = C_HEADS * C_DK
C_V = C_HEADS * C_DV
D_FF = 2816
LANES = 128
MOD_ROWS = 8
VMEM_LIMIT = 56 * 1024 * 1024


def _cparams(sem):
    return pltpu.CompilerParams(dimension_semantics=sem, vmem_limit_bytes=VMEM_LIMIT)


def _dot(a, b):
    return jnp.dot(a, b, preferred_element_type=F32)


def _dot_nt(a, b):
    return lax.dot_general(a, b, (((1,), (1,)), ((), ())), preferred_element_type=F32)


def _dot_tn(a, b):
    return lax.dot_general(a, b, (((0,), (0,)), ((), ())), preferred_element_type=F32)


def _sigmoid(x):
    return 1.0 / (1.0 + jnp.exp(-x))


def _norm_mod(x, g, sc, sh):
    ms = jnp.mean(x * x, axis=-1, keepdims=True)
    return (x * lax.rsqrt(ms + EPS) * g) * (1.0 + sc) + sh


def _mod_kernel(c_ref, w_ref, b_ref, o_ref):
    c = c_ref[...]
    s = (c * _sigmoid(c)).astype(BF16)
    o_ref[...] = _dot(s, w_ref[...].astype(BF16)) + b_ref[...]


def _modulation(c_rows, mod_w, mod_b):
    d = D_MODEL
    return pl.pallas_call(
        _mod_kernel,
        grid=(DEPTH, 6),
        in_specs=[
            pl.BlockSpec((MOD_ROWS, d), lambda l, n: (0, 0)),
            pl.BlockSpec((None, d, d), lambda l, n: (l, 0, n)),
            pl.BlockSpec((None, 1, d), lambda l, n: (l, 0, n)),
        ],
        out_specs=pl.BlockSpec((None, MOD_ROWS, d), lambda l, n: (l, 0, n)),
        out_shape=jax.ShapeDtypeStruct((DEPTH, MOD_ROWS, 6 * d), F32),
        compiler_params=_cparams(("parallel", "parallel")),
        name="modulation",
    )(c_rows, mod_w, mod_b.reshape(DEPTH, 1, 6 * d))


def _mod_spec(k, row_of):
    return pl.BlockSpec((None, None, 1, D_MODEL), lambda i, *_: (row_of(i), k, 0, 0))


LOG2E = math.log2(math.e)
_Q_SCALE = HEAD_DIM ** -0.5 * LOG2E
_ATTN_GROUPS = (("aq", 0, A_Q, True, _Q_SCALE), ("bq", A_Q + 2 * A_KV, B_QK, True, _Q_SCALE),
                ("bk", A_Q + 2 * A_KV + B_QK, B_QK, True, 1.0), ("ak", A_Q, A_KV, True, 1.0),
                ("av", A_Q + A_KV, A_KV, False, 1.0), ("bv", A_Q + 2 * A_KV + 2 * B_QK, B_V, False, 1.0))
_ATTN_OUT_WIDTHS = tuple(w_ for name, _, width, _, _ in _ATTN_GROUPS
                         for w_ in ((width, width) if name == "bq" else
                                    (2 * width,) if name in ("ak", "av") else (width,)))


def _rope_chunk(v, cos, sin, even):
    partner = jnp.where(even, pltpu.roll(v, LANES - 1, 1), pltpu.roll(v, 1, 1))
    return v * cos + partner * sin


def _attn_inproj_kernel(x_ref, sh_ref, sc_ref, g_ref, w_ref, cos_ref, sin_ref, *out_refs, rope):
    xn = _norm_mod(x_ref[...], g_ref[...], sc_ref[...], sh_ref[...]).astype(BF16)
    tm = xn.shape[0]
    if rope:
        cos = cos_ref[...]
        sin = sin_ref[...]
        even = (lax.broadcasted_iota(jnp.int32, (tm, LANES), 1) & 1) == 0
    lo = lax.broadcasted_iota(jnp.int32, (1, LANES), 1) < HEAD_DIM
    outs = iter(out_refs)
    for name, col, width, roped, scale in _ATTN_GROUPS:
        y = _dot(xn, w_ref[:, col:col + width])
        o_refs = [next(outs), next(outs)] if name == "bq" else [next(outs)]
        for c in range(width // LANES):
            v = y[:, c * LANES:(c + 1) * LANES]
            if rope and roped:
                v = _rope_chunk(v, cos, sin, even)
            if scale != 1.0:
                v = v * scale
            if name == "bq":
                o_refs[0][:, c * LANES:(c + 1) * LANES] = jnp.where(lo, v, 0.0).astype(BF16)
                o_refs[1][:, c * LANES:(c + 1) * LANES] = jnp.where(lo, 0.0, v).astype(BF16)
            elif name in ("ak", "av"):
                swapped = pltpu.roll(v, HEAD_DIM, 1)
                o_refs[0][:, 0:LANES] = jnp.where(lo, v, swapped).astype(BF16)
                o_refs[0][:, LANES:2 * LANES] = jnp.where(lo, swapped, v).astype(BF16)
            else:
                o_refs[0][:, c * LANES:(c + 1) * LANES] = v.astype(BF16)


def _attn_inproj(x2d, mod, norm_g, w, cos_t, sin_t, *, tm, row_of, rope, tiles_per_seq):
    m, d = x2d.shape
    n_all = w.shape[1]
    widths = _ATTN_OUT_WIDTHS
    out_shape = [jax.ShapeDtypeStruct((m, width), BF16) for width in widths]
    out_specs = [pl.BlockSpec((tm, width), lambda i: (i, 0)) for width in widths]
    return pl.pallas_call(
        functools.partial(_attn_inproj_kernel, rope=rope),
        grid=(m // tm,),
        in_specs=[
            pl.BlockSpec((tm, d), lambda i: (i, 0)),
            _mod_spec(0, row_of), _mod_spec(1, row_of),
            pl.BlockSpec((1, d), lambda i: (0, 0)),
            pl.BlockSpec((d, n_all), lambda i: (0, 0)),
            pl.BlockSpec((tm, LANES), lambda i: (i % tiles_per_seq, 0)),
            pl.BlockSpec((tm, LANES), lambda i: (i % tiles_per_seq, 0)),
        ],
        out_specs=out_specs,
        out_shape=out_shape,
        compiler_params=_cparams(("parallel",)),
        name="attn_inproj_rope" if rope else "attn_inproj_ctx",
    )(x2d, mod, mod, norm_g, w, cos_t, sin_t)


GQA_QB = 8


def _gqa_kernel(sink_ref, q_ref, *refs, window, nb):
    tq = BLOCK
    qb = q_ref.shape[0] // tq
    if window:
        k_blk = [r[...] for r in refs[0:qb + 2]]
        v_blk = [r[...] for r in refs[qb + 2:2 * qb + 4]]
        kx_ref, vx_ref, o_ref = refs[2 * qb + 4:]
    else:
        kx_ref, vx_ref, o_ref = refs
    lane = lax.broadcasted_iota(jnp.int32, (1, LANES), 1)
    lo = lane < HEAD_DIM
    half = (jnp.where(lo, 1.0, 0.0).astype(F32), jnp.where(lo, 0.0, 1.0).astype(F32))
    row = lax.broadcasted_iota(jnp.int32, (LANES, 1), 0)
    sinks = [jnp.concatenate([jnp.full((1, tq), sink_ref[kv * A_GROUP + j] * LOG2E, F32)
                              for j in range(A_GROUP)], axis=1) for kv in range(A_KV_HEADS)]
    work = []
    for u in range(qb):
        if window:
            keys = jnp.concatenate(k_blk[u:u + 3] + [kx_ref[...]], axis=0)
            vals = jnp.concatenate(v_blk[u:u + 3] + [vx_ref[...]], axis=0)
        else:
            keys, vals = kx_ref[...], vx_ref[...]
        qf = [q_ref[u * tq:(u + 1) * tq, j * LANES:(j + 1) * LANES].astype(F32) for j in range(A_GROUP)]
        scs = [_dot_nt(keys[:, kv * LANES:(kv + 1) * LANES],
                       jnp.concatenate([(qf[2 * kv + c] * half[s_]).astype(BF16)
                                        for c in range(2) for s_ in range(2)], axis=0))
               for kv in range(A_KV_HEADS)]
        work.append((vals, scs))
    for u, (vals, scs) in enumerate(work):
        if window:
            n = pl.program_id(1) * qb + u
            s = lax.broadcasted_iota(jnp.int32, (BLOCK, tq), 0)
            t = lax.broadcasted_iota(jnp.int32, (BLOCK, tq), 1)
            has_prev = jnp.where(n > 0, 1, 0)
            has_next = jnp.where(n < nb - 1, 1, 0)
            lower = t * has_prev + BLOCK * (1 - has_prev)
            upper = (t + 1) * has_next - 1
            ninf = jnp.float32(-jnp.inf)
            bias_prev = jnp.concatenate([jnp.where(s < lower, ninf, 0.0)] * A_GROUP, axis=1)
            bias_next = jnp.concatenate([jnp.where(s > upper, ninf, 0.0)] * A_GROUP, axis=1)
        outs = []
        for kv in range(A_KV_HEADS):
            sc = scs[kv]
            if window:
                sc = jnp.concatenate([sc[0:BLOCK] + bias_prev, sc[BLOCK:2 * BLOCK],
                                      sc[2 * BLOCK:3 * BLOCK] + bias_next, sc[3 * BLOCK:]], axis=0)
            sk = sinks[kv]
            mx = jnp.maximum(jnp.max(sc, axis=0, keepdims=True), sk)
            p = jnp.exp2(sc - mx)
            den = jnp.sum(p, axis=0, keepdims=True) + jnp.exp2(sk - mx)
            outs.append(_dot_tn(vals[:, kv * LANES:(kv + 1) * LANES], p.astype(BF16)) * (1.0 / den))
        for j in range(A_GROUP):
            acc = outs[j // 2]
            c0 = (j % 2) * 2 * tq
            o_t = jnp.where(row < HEAD_DIM, acc[:, c0:c0 + tq], acc[:, c0 + tq:c0 + 2 * tq])
            o_ref[u * tq:(u + 1) * tq, j * LANES:(j + 1) * LANES] = o_t.T.astype(BF16)


def _gqa_window(sink, aq, ak, av, cak, cav):
    bsz, n_tok, _ = aq.shape
    n_ctx = cak.shape[1]
    nb = n_tok // BLOCK
    qb = math.gcd(nb, GQA_QB)
    kv_blocks = [pl.BlockSpec((None, BLOCK, 2 * A_KV),
                              lambda b, n, off=off: (b, jnp.clip(n * qb + off, 0, nb - 1), 0))
                 for off in range(-1, qb + 1)]
    kv_ctx = pl.BlockSpec((None, n_ctx, 2 * A_KV), lambda b, n: (b, 0, 0))
    return pl.pallas_call(
        functools.partial(_gqa_kernel, window=True, nb=nb),
        grid=(bsz, nb // qb),
        in_specs=[pl.BlockSpec(memory_space=pltpu.SMEM),
                  pl.BlockSpec((None, qb * BLOCK, A_Q), lambda b, n: (b, n, 0))]
        + kv_blocks + kv_blocks + [kv_ctx, kv_ctx],
        out_specs=pl.BlockSpec((None, qb * BLOCK, A_Q), lambda b, n: (b, n, 0)),
        out_shape=jax.ShapeDtypeStruct((bsz, n_tok, A_Q), BF16),
        compiler_params=_cparams(("parallel", "parallel")),
        name="gqa_window",
    )(sink, aq, *([ak] * (qb + 2)), *([av] * (qb + 2)), cak, cav)


def _gqa_context(sink, caq, cak, cav):
    bsz, n_ctx, _ = caq.shape
    nb = n_ctx // BLOCK
    kv_ctx = pl.BlockSpec((None, n_ctx, 2 * A_KV), lambda b, n: (b, 0, 0))
    return pl.pallas_call(
        functools.partial(_gqa_kernel, window=False, nb=nb),
        grid=(bsz, nb),
        in_specs=[pl.BlockSpec(memory_space=pltpu.SMEM),
                  pl.BlockSpec((None, BLOCK, A_Q), lambda b, n: (b, n, 0)),
                  kv_ctx, kv_ctx],
        out_specs=pl.BlockSpec((None, BLOCK, A_Q), lambda b, n: (b, n, 0)),
        out_shape=jax.ShapeDtypeStruct((bsz, n_ctx, A_Q), BF16),
        compiler_params=_cparams(("parallel", "parallel")),
        name="gqa_context",
    )(sink, caq, cak, cav)


LAG_LIMIT = 12.0
XPOSE_ROWS = 512
ONES_ROWS = 16


def _diff_attn_kernel(lam_ref, g_ref, q0_ref, q1_ref, *refs, nseg, lam_init, kc, lagged):
    k_refs = refs[0:2 * nseg:2]
    v_refs = refs[1:2 * nseg:2]
    if lagged:
        o_ref, ex_ref, vt_ref = refs[2 * nseg:]
    else:
        o_ref, vt_ref = refs[2 * nseg:]
    lv = lam_ref[...]
    lam = (jnp.exp(jnp.sum(lv[0:1] * lv[1:2], axis=-1, keepdims=True))
           - jnp.exp(jnp.sum(lv[2:3] * lv[3:4], axis=-1, keepdims=True)) + lam_init)
    q0 = q0_ref[...]
    q1 = q1_ref[...]
    @pl.when(pl.program_id(2) == 0)
    def _():
        off = 0
        for v_ref in v_refs:
            ns = v_ref.shape[0]
            for c0 in range(0, ns, XPOSE_ROWS):
                n = min(XPOSE_ROWS, ns - c0)
                vt_ref[0:LANES, off + c0:off + c0 + n] = v_ref[c0:c0 + n, :].astype(F32).T.astype(BF16)
            off += ns
        vt_ref[LANES:, :] = jnp.ones((vt_ref.shape[0] - LANES, vt_ref.shape[1]), BF16)

    chunks = []
    off = 0
    for k_ref in k_refs:
        ns = k_ref.shape[0]
        step = min(kc, ns)
        chunks += [(k_ref, c0, step, off + c0) for c0 in range(0, ns, step)]
        off += ns

    qms = (q0, q1)

    def scores(mi, ci):
        k_ref, c0, step, _ = chunks[ci]
        return _dot_nt(k_ref[c0:c0 + step, :], qms[mi])

    run_max = [None, None]
    accs = [None, None]
    excess = [None, None]

    def probs(mi, sc):
        mc = jnp.max(sc, axis=0, keepdims=True)
        if run_max[mi] is None:
            run_max[mi], alpha = (mc, mc), None
        else:
            ref_old, best = run_max[mi]
            new_best = jnp.maximum(best, mc)
            ref = best if lagged else new_best
            alpha = jnp.exp2(ref_old - ref)
            if lagged:
                over = mc - ref
                excess[mi] = over if excess[mi] is None else jnp.maximum(excess[mi], over)
            run_max[mi] = (ref, new_best)
        return jnp.exp2((sc - run_max[mi][0]).astype(BF16)), alpha

    def accumulate(mi, ci, pb, alpha):
        _, _, step, g0 = chunks[ci]
        pv = _dot(vt_ref[:, g0:g0 + step], pb)
        accs[mi] = pv if alpha is None else accs[mi] * alpha + pv

    n_ch = len(chunks)
    sc_q = {ci: [scores(mi, ci) for mi in range(2)] for ci in range(min(2, n_ch))}
    pb_q = {0: [probs(mi, sc_q[0][mi]) for mi in range(2)]}
    for ci in range(n_ch):
        if ci + 2 < n_ch:
            sc_q[ci + 2] = [scores(mi, ci + 2) for mi in range(2)]
        if ci + 1 < n_ch:
            sc_pair = sc_q.pop(ci + 1)
            pb_q[ci + 1] = [probs(mi, sc_pair[mi]) for mi in range(2)]
        for mi, (pb, alpha) in enumerate(pb_q.pop(ci)):
            accumulate(mi, ci, pb, alpha)
    r0 = 1.0 / accs[0][LANES:LANES + 1, :]
    r1 = lam / accs[1][LANES:LANES + 1, :]
    o = (accs[0][0:LANES, :] * r0 - accs[1][0:LANES, :] * r1).T
    ms = jnp.mean(o * o, axis=-1, keepdims=True)
    o_ref[...] = ((o * lax.rsqrt(ms + EPS) * g_ref[...]) * (1.0 - lam_init)).astype(BF16)
    if lagged:
        ex_ref[...] = jnp.maximum(excess[0], excess[1])


def _diff_attn(lam_vec, subln_g, q0, q1, kv_list, *, tq, lam_init, kc=256, lagged=False):
    bsz, n_q, _ = q0.shape
    out_specs = [pl.BlockSpec((None, tq, LANES), lambda b, h, i: (b, i, h))]
    out_shape = [jax.ShapeDtypeStruct((bsz, n_q, B_V), BF16)]
    if lagged:
        out_specs.append(pl.BlockSpec((None, None, 1, tq), lambda b, h, i: (b, h, 0, i)))
        out_shape.append(jax.ShapeDtypeStruct((bsz, B_HEADS, 1, n_q), F32))
    in_specs = [pl.BlockSpec((4, HEAD_DIM), lambda b, h, i: (0, 0)),
                pl.BlockSpec((1, LANES), lambda b, h, i: (0, 0)),
                pl.BlockSpec((None, tq, LANES), lambda b, h, i: (b, i, h)),
                pl.BlockSpec((None, tq, LANES), lambda b, h, i: (b, i, h))]
    args = [lam_vec, subln_g, q0, q1]
    for k, v in kv_list:
        ns = k.shape[1]
        in_specs += [pl.BlockSpec((None, ns, LANES), lambda b, h, i: (b, 0, h))] * 2
        args += [k, v]
    return pl.pallas_call(
        functools.partial(_diff_attn_kernel, nseg=len(kv_list), lam_init=lam_init, kc=kc, lagged=lagged),
        grid=(bsz, B_HEADS, n_q // tq),
        in_specs=in_specs,
        out_specs=out_specs,
        out_shape=out_shape,
        scratch_shapes=[pltpu.VMEM((LANES + ONES_ROWS, sum(k.shape[1] for k, _ in kv_list)), BF16)],
        compiler_params=_cparams(("parallel", "parallel", "arbitrary")),
        name="diff_attn_%dseg%s" % (len(kv_list), "_lagged" if lagged else ""),
    )(*args)


FFN_HALO = 16
FFN_TF = 256


def _ffn_kernel(*refs, n_mix, tiles_per_seq, final_norm):
    a_refs = refs[:3 * n_mix]
    wo_refs = refs[3 * n_mix:4 * n_mix]
    (h_ref, hp_ref, hn_ref, g1_ref, sh_ref, sc_ref, gt_ref, ng_ref, wup_ref, cwu_ref, cwg_ref,
     wd_ref, fg_ref, o_ref, xn_ref, hu_a, hg_a, hu_b, hg_b, acc_ref) = refs[4 * n_mix:]
    i = pl.program_id(0)
    tm = h_ref.shape[0]
    hl = FFN_HALO
    nj, tf = wd_ref.shape[0], wd_ref.shape[1]

    ext = tm + 2 * hl
    half = ext // 2
    pos = i % tiles_per_seq
    a_ext = [jnp.concatenate([a_refs[3 * m + 1][...], a_refs[3 * m][...], a_refs[3 * m + 2][...]], axis=0)
             for m in range(n_mix)]
    h_ext = jnp.concatenate([hp_ref[...], h_ref[...], hn_ref[...]], axis=0)
    ys = []
    for blk in range(2):
        rows = slice(blk * half, (blk + 1) * half)
        y = None
        for m in range(n_mix):
            part = _dot(a_ext[m][rows, :], wo_refs[m][...])
            y = part if y is None else y + part
        ys.append(y)
    for blk in range(2):
        h1 = h_ext[blk * half:(blk + 1) * half, :] + g1_ref[...] * ys[blk]
        xn = _norm_mod(h1, ng_ref[...], sc_ref[...], sh_ref[...])
        if blk == 0:
            o_ref[0:half - hl, :] = h1[hl:, :]
            xn_ref[0:hl, :] = (xn[0:hl, :] * jnp.where(pos > 0, 1.0, 0.0)).astype(BF16)
            xn_ref[hl:half, :] = xn[hl:, :].astype(BF16)
        else:
            o_ref[half - hl:, :] = h1[0:half - hl, :]
            xn_ref[half:ext - hl, :] = xn[0:half - hl, :].astype(BF16)
            xn_ref[ext - hl:, :] = (xn[half - hl:, :] * jnp.where(pos < tiles_per_seq - 1, 1.0, 0.0)).astype(BF16)
    acc_ref[...] = jnp.zeros_like(acc_ref)

    def up(j, hu_ref, hg_ref, row_blocks=1):
        cols = pl.ds(pl.multiple_of(j * tf, tf), tf)
        gate_cols = pl.ds(pl.multiple_of(nj * tf + j * tf, tf), tf)
        for blk in range(row_blocks):
            rows = slice(blk * ext // row_blocks, (blk + 1) * ext // row_blocks)
            xn = xn_ref[rows, :]
            hu_ref[rows, :] = _dot(xn, wup_ref[:, cols])
            hg_ref[rows, :] = _dot(xn, wup_ref[:, gate_cols])

    def conv(ref, cw):
        return (ref[hl - 1:hl - 1 + tm, :] * cw[0:1] + ref[hl:hl + tm, :] * cw[1:2]
                + ref[hl + 1:hl + 1 + tm, :] * cw[2:3] + cw[3:4])

    def act(j, hu_ref, hg_ref):
        u = conv(hu_ref, cwu_ref[j])
        gt = conv(hg_ref, cwg_ref[j])
        return (gt * _sigmoid(gt) * u).astype(BF16)

    up(0, hu_a, hg_a, row_blocks=2)

    def pair(jj, carry):
        j = 2 * jj
        up(j + 1, hu_b, hg_b)
        acc_ref[...] += _dot(act(j, hu_a, hg_a), wd_ref[j])
        up(j + 2, hu_a, hg_a)
        acc_ref[...] += _dot(act(j + 1, hu_b, hg_b), wd_ref[j + 1])
        return carry

    assert nj % 2 == 1
    lax.fori_loop(0, (nj - 1) // 2, pair, 0)
    acc_ref[...] += _dot(act(nj - 1, hu_a, hg_a), wd_ref[nj - 1])

    y = o_ref[...] + gt_ref[...] * acc_ref[...]
    if final_norm:
        ms = jnp.mean(y * y, axis=-1, keepdims=True)
        y = y * lax.rsqrt(ms + EPS) * fg_ref[...]
    o_ref[...] = y


def _ffn_weights(w_up, conv_w, conv_b, w_down, tf):
    depth, d = w_up.shape[0], w_up.shape[1]
    nj = D_FF // tf
    chunked = lambda w: w.reshape(depth, w.shape[1], nj, tf).transpose(0, 2, 1, 3)
    cw = jnp.concatenate([conv_w, conv_b[:, None, :]], axis=1)
    return (w_up.astype(BF16), chunked(cw[:, :, :D_FF]), chunked(cw[:, :, D_FF:]),
            w_down.astype(BF16).reshape(depth, nj, tf, d))


def _mixer_out_ffn(acts, w_outs, h2d, mod, norm_g, weights, layer, final_g, *, tm, row_of, tiles_per_seq,
                   final_norm):
    m, d = h2d.shape
    hl = FFN_HALO
    nhb = m // hl
    r = tm // hl
    w_up, cwu, cwg, w_dn = weights
    tf = w_dn.shape[2]
    resident = lambda a: pl.BlockSpec(a.shape, lambda i: (0,) * a.ndim, pipeline_mode=pl.Buffered(1))
    of_layer = lambda a: pl.BlockSpec((None,) + a.shape[1:], lambda i: (layer,) + (0,) * (a.ndim - 1),
                                      pipeline_mode=pl.Buffered(1))

    def tile_and_halos(width):
        return [pl.BlockSpec((tm, width), lambda i: (i, 0)),
                pl.BlockSpec((hl, width), lambda i: (jnp.maximum(i * r - 1, 0), 0)),
                pl.BlockSpec((hl, width), lambda i: (jnp.minimum((i + 1) * r, nhb - 1), 0))]

    in_specs, args = [], []
    for a in acts:
        in_specs += tile_and_halos(a.shape[1])
        args += [a, a, a]
    in_specs += [resident(w) for w in w_outs]
    args += list(w_outs)
    in_specs += tile_and_halos(d)
    in_specs += [_mod_spec(2, row_of), _mod_spec(3, row_of), _mod_spec(4, row_of), _mod_spec(5, row_of),
                 pl.BlockSpec((1, d), lambda i: (0, 0)),
                 of_layer(w_up), of_layer(cwu), of_layer(cwg), of_layer(w_dn),
                 pl.BlockSpec((1, d), lambda i: (0, 0))]
    args += [h2d, h2d, h2d, mod, mod, mod, mod, norm_g, w_up, cwu, cwg, w_dn, final_g]
    return pl.pallas_call(
        functools.partial(_ffn_kernel, n_mix=len(acts), tiles_per_seq=tiles_per_seq, final_norm=final_norm),
        grid=(m // tm,),
        in_specs=in_specs,
        out_specs=pl.BlockSpec((tm, d), lambda i: (i, 0)),
        out_shape=jax.ShapeDtypeStruct((m, d), F32),
        scratch_shapes=[pltpu.VMEM((tm + 2 * hl, d), BF16)]
        + [pltpu.VMEM((tm + 2 * hl, tf), F32)] * 4
        + [pltpu.VMEM((tm, d), F32)],
        compiler_params=_cparams(("parallel",)),
        name="mixer_out_conv_ffn",
    )(*args)


def _gla_inproj_kernel(x_ref, sh_ref, sc_ref, g_ref, w_ref, w2_ref, gb_ref,
                       q_ref, k_ref, v_ref, sg_ref, laf_ref, lab_ref):
    tm = x_ref.shape[0]
    n_split = 2 if tm % 32 == 0 else 1
    blocks = [slice(s * tm // n_split, (s + 1) * tm // n_split) for s in range(n_split)]
    xns = [_norm_mod(x_ref[rows, :], g_ref[...], sc_ref[...], sh_ref[...]).astype(BF16) for rows in blocks]
    for rows, xn in zip(blocks, xns):
        q_ref[rows, :] = (_dot(xn, w_ref[:, 0:C_QK]) * (C_DK ** -0.5)).astype(BF16)
        k_ref[rows, :] = _dot(xn, w_ref[:, C_QK:2 * C_QK]).astype(BF16)
        for c in range(C_V // C_QK):
            lo = 2 * C_QK + c * C_QK
            v_ref[rows, c * C_QK:(c + 1) * C_QK] = _dot(xn, w_ref[:, lo:lo + C_QK]).astype(BF16)
        for c in range(C_V // C_QK):
            lo = 2 * C_QK + C_V + c * C_QK
            gg = _dot(xn, w_ref[:, lo:lo + C_QK])
            sg_ref[rows, c * C_QK:(c + 1) * C_QK] = (gg * _sigmoid(gg)).astype(BF16)
        lo = 2 * C_QK + 2 * C_V
        r = _dot(xn, w_ref[:, lo:lo + LANES]).astype(BF16)
        for dr, la_ref in enumerate((laf_ref, lab_ref)):
            z = _dot(r, w2_ref[:, dr * C_QK:(dr + 1) * C_QK]) + gb_ref[:, dr * C_QK:(dr + 1) * C_QK]
            la_ref[rows, :] = (jnp.minimum(z, 0.0) - jnp.log(1.0 + jnp.exp(-jnp.abs(z)))) * (1.0 / C_GATE_NORM)


def _gla_inproj(x2d, mod, norm_g, w, w2, gb, *, tm, row_of):
    m, d = x2d.shape
    widths = (C_QK, C_QK, C_V, C_V, C_QK, C_QK)
    dts = (BF16, BF16, BF16, BF16, F32, F32)
    return pl.pallas_call(
        _gla_inproj_kernel,
        grid=(m // tm,),
        in_specs=[
            pl.BlockSpec((tm, d), lambda i: (i, 0)),
            _mod_spec(0, row_of), _mod_spec(1, row_of),
            pl.BlockSpec((1, d), lambda i: (0, 0)),
            pl.BlockSpec(w.shape, lambda i: (0, 0)),
            pl.BlockSpec(w2.shape, lambda i: (0, 0)),
            pl.BlockSpec(gb.shape, lambda i: (0, 0)),
        ],
        out_specs=[pl.BlockSpec((tm, wd), lambda i: (i, 0)) for wd in widths],
        out_shape=[jax.ShapeDtypeStruct((m, wd), dt) for wd, dt in zip(widths, dts)],
        compiler_params=_cparams(("parallel",)),
        name="gla_inproj",
    )(x2d, mod, mod, norm_g, w, w2, gb)


def _tri(n, reverse):
    r = lax.broadcasted_iota(jnp.int32, (n, n), 0)
    c = lax.broadcasted_iota(jnp.int32, (n, n), 1)
    return (c >= r) if reverse else (c <= r)


def _cumsum_rows(la, tri_bf):
    hi = la.astype(BF16)
    r1 = la - hi.astype(F32)
    mid = r1.astype(BF16)
    lo = (r1 - mid.astype(F32)).astype(BF16)
    return _dot(tri_bf, hi) + _dot(tri_bf, mid) + _dot(tri_bf, lo)


def _gla_ctx_state_kernel(k_ref, v_ref, laf_ref, lab_ref, sf_ref, sb_ref):
    n = k_ref.shape[0]
    for reverse, la_ref, s_ref in ((False, laf_ref, sf_ref), (True, lab_ref, sb_ref)):
        tri = jnp.where(_tri(n, reverse), 1.0, 0.0).astype(BF16)
        b = _cumsum_rows(la_ref[...], tri)
        b_end = b[0:1, :] if reverse else b[n - 1:n, :]
        kw = (k_ref[...].astype(F32) * jnp.exp(b_end - b)).astype(BF16)
        for h in range(C_HEADS):
            s_ref[h] = _dot_tn(v_ref[:, h * C_DV:(h + 1) * C_DV], kw[:, h * C_DK:(h + 1) * C_DK])


def _gla_ctx_state(kc, vc, lac_f, lac_b):
    bsz, n, _ = kc.shape
    s_shape = jax.ShapeDtypeStruct((bsz, C_HEADS, C_DV, C_DK), F32)
    s_spec = pl.BlockSpec((None, C_HEADS, C_DV, C_DK), lambda b: (b, 0, 0, 0))
    return pl.pallas_call(
        _gla_ctx_state_kernel,
        grid=(bsz,),
        in_specs=[pl.BlockSpec((None, n, C_QK), lambda b: (b, 0, 0)),
                  pl.BlockSpec((None, n, C_V), lambda b: (b, 0, 0)),
                  pl.BlockSpec((None, n, C_QK), lambda b: (b, 0, 0)),
                  pl.BlockSpec((None, n, C_QK), lambda b: (b, 0, 0))],
        out_specs=[s_spec, s_spec],
        out_shape=[s_shape, s_shape],
        compiler_params=_cparams(("parallel",)),
        name="gla_ctx_state",
    )(kc, vc, lac_f, lac_b)


GLA_BATCH_BLOCK = 4


def _gla_scan_kernel(q_ref, k_ref, v_ref, la_ref, s0_ref, *refs, reverse, final):
    if final:
        ob_ref, sg_ref, ng_ref, o_ref, st_ref = refs
    else:
        o_ref, st_ref = refs
    gidx = pl.program_id(1)

    @pl.when(gidx == 0)
    def _():
        st_ref[...] = s0_ref[...]

    nbb, gt = q_ref.shape[0], q_ref.shape[1]
    nchunk = gt // C_CHUNK
    c = C_CHUNK
    r = lax.broadcasted_iota(jnp.int32, (gt, gt), 0)
    s = lax.broadcasted_iota(jnp.int32, (gt, gt), 1)
    shift = c.bit_length() - 1
    same_chunk = (r >> shift) == (s >> shift)
    tri = same_chunk & ((s >= r) if reverse else (s <= r))
    tri_bf = jnp.where(tri, 1.0, 0.0).astype(BF16)

    def per_chunk_row(x, row):
        return jnp.concatenate([jnp.broadcast_to(x[ci * c + row:ci * c + row + 1, :], (c, x.shape[1]))
                                for ci in range(nchunk)], axis=0)

    order = range(nchunk - 1, -1, -1) if reverse else range(nchunk)
    kcols = [slice(h * C_DK, (h + 1) * C_DK) for h in range(C_HEADS)]
    vcols_of = [slice(h * C_DV, (h + 1) * C_DV) for h in range(C_HEADS)]
    streams = [(bi, h) for bi in range(nbb) for h in range(C_HEADS)]
    q_out, k_out, decay, o_intra, upd = {}, {}, {}, {}, {}
    for bi in range(nbb):
        b = _cumsum_rows(la_ref[bi], tri_bf)
        b_mid = per_chunk_row(b, c // 2)
        b_end = per_chunk_row(b, 0 if reverse else c - 1)
        qf = q_ref[bi].astype(F32)
        kf = k_ref[bi].astype(F32)
        q_in = (qf * jnp.exp(b - b_mid)).astype(BF16)
        k_in = (kf * jnp.exp(b_mid - b)).astype(BF16)
        q_out[bi] = (qf * jnp.exp(b)).astype(BF16)
        k_out[bi] = (kf * jnp.exp(b_end - b)).astype(BF16)
        end_row = 0 if reverse else c - 1
        decay[bi] = [jnp.exp(b[ci * c + end_row:ci * c + end_row + 1, :]) for ci in range(nchunk)]
        for h in range(C_HEADS):
            vv = v_ref[bi, :, vcols_of[h]]
            sc = jnp.where(tri, _dot_nt(q_in[:, kcols[h]], k_in[:, kcols[h]]), 0.0).astype(BF16)
            o_intra[bi, h] = _dot(sc, vv)
            upd[bi, h] = {ci: _dot_tn(vv[ci * c:(ci + 1) * c, :], k_out[bi][ci * c:(ci + 1) * c, kcols[h]])
                          for ci in order}
    st = {s_: st_ref[s_[0], s_[1]] for s_ in streams}
    o_inter = {s_: {} for s_ in streams}
    for ci in order:
        rows = slice(ci * c, (ci + 1) * c)
        for bi, h in streams:
            o_inter[bi, h][ci] = _dot_nt(q_out[bi][rows, kcols[h]], st[bi, h].astype(BF16))
            st[bi, h] = st[bi, h] * decay[bi][ci][:, kcols[h]] + upd[bi, h][ci]
    for bi, h in streams:
        vcols = vcols_of[h]
        st_ref[bi, h] = st[bi, h]
        o = o_intra[bi, h] + jnp.concatenate([o_inter[bi, h][ci] for ci in range(nchunk)], axis=0)
        if final:
            o = o + ob_ref[bi, :, vcols]
            ms = jnp.mean(o * o, axis=-1, keepdims=True)
            o = (o * lax.rsqrt(ms + EPS) * ng_ref[...]) * sg_ref[bi, :, vcols].astype(F32)
            o_ref[bi, :, vcols] = o.astype(BF16)
        else:
            o_ref[bi, :, vcols] = o


def _gla_scan(q, k, v, la, s0, *, gt, reverse, o_other=None, sg=None, norm_g=None):
    bsz, n_tok, _ = q.shape
    ng = n_tok // gt
    final = o_other is not None
    nbb = GLA_BATCH_BLOCK if bsz % GLA_BATCH_BLOCK == 0 else 1
    gi = (lambda b, g: (b, ng - 1 - g, 0)) if reverse else (lambda b, g: (b, g, 0))
    in_specs = [pl.BlockSpec((nbb, gt, C_QK), gi), pl.BlockSpec((nbb, gt, C_QK), gi),
                pl.BlockSpec((nbb, gt, C_V), gi), pl.BlockSpec((nbb, gt, C_QK), gi),
                pl.BlockSpec((nbb, C_HEADS, C_DV, C_DK), lambda b, g: (b, 0, 0, 0))]
    args = [q, k, v, la, s0]
    if final:
        in_specs += [pl.BlockSpec((nbb, gt, C_V), gi), pl.BlockSpec((nbb, gt, C_V), gi),
                     pl.BlockSpec((1, C_DV), lambda b, g: (0, 0))]
        args += [o_other, sg, norm_g]
    return pl.pallas_call(
        functools.partial(_gla_scan_kernel, reverse=reverse, final=final),
        grid=(bsz // nbb, ng),
        in_specs=in_specs,
        out_specs=pl.BlockSpec((nbb, gt, C_V), gi),
        out_shape=jax.ShapeDtypeStruct((bsz, n_tok, C_V), BF16 if final else F32),
        scratch_shapes=[pltpu.VMEM((nbb, C_HEADS, C_DV, C_DK), F32)],
        compiler_params=_cparams(("parallel", "arbitrary")),
        name="gla_scan_fwd_final" if final else "gla_scan_bwd",
    )(*args)


def _rope_tables(n_tok):
    rows = n_tok // GRID_W
    row = jnp.repeat(jnp.arange(rows, dtype=F32), GRID_W)
    col = jnp.tile(jnp.arange(GRID_W, dtype=F32), rows)
    axis_dim = HEAD_DIM // 2
    inv_freq = ROPE_THETA ** (-jnp.arange(0, axis_dim, 2, dtype=F32) / axis_dim)
    ang = jnp.concatenate([row[:, None] * inv_freq, col[:, None] * inv_freq], axis=-1)
    cos = jnp.repeat(jnp.cos(ang), 2, axis=1)
    sin = jnp.repeat(jnp.sin(ang), 2, axis=1) * jnp.tile(jnp.array([-1.0, 1.0], F32), HEAD_DIM // 2)
    return jnp.tile(cos, (1, LANES // HEAD_DIM)), jnp.tile(sin, (1, LANES // HEAD_DIM))


def _pick(n, pref):
    return pref if n % pref == 0 else n


def kernel(x, c, ctx, c_ctx, mod_w, mod_b, norm1_g, norm2_g, attn_w_in, attn_w_out, attn_sink, diff_lambda, diff_subln_g, gla_w_in, gla_gate_w1, gla_gate_w2, gla_gate_b, gla_norm_g, gla_w_out, ffn_w_up, ffn_conv_w, ffn_conv_b, ffn_w_down, final_norm_g):
    bsz, n_tok, d = x.shape
    n_ctx = ctx.shape[1]
    assert d == D_MODEL and bsz + 1 <= MOD_ROWS
    m_lat, m_ctx = bsz * n_tok, bsz * n_ctx

    c_rows = jnp.concatenate([c, c_ctx[None, :], jnp.zeros((MOD_ROWS - bsz - 1, d), F32)], axis=0)
    mod_all = _modulation(c_rows, mod_w, mod_b)
    cos_t, sin_t = _rope_tables(n_tok)

    tm = _pick(n_tok, 1024)
    tmc = _pick(n_ctx, 256)
    lat_tiles = n_tok // tm
    lat_row = lambda i: i // lat_tiles
    ctx_row = lambda i: bsz
    tm_ffn = _pick(n_tok, 1024)
    ffn_row = lambda i: i // (n_tok // tm_ffn)

    h = x.reshape(m_lat, d)
    hc = ctx.reshape(m_ctx, d)
    ffn_w = _ffn_weights(ffn_w_up, ffn_conv_w, ffn_conv_b, ffn_w_down, FFN_TF)
    for layer in range(DEPTH):
        need_ctx = layer < DEPTH - 1
        last = layer == DEPTH - 1
        mod = mod_all[layer].reshape(MOD_ROWS, 6, 1, d)
        n1 = norm1_g[layer].reshape(1, d)
        n2 = norm2_g[layer].reshape(1, d)
        i = layer // 2
        if layer % 2 == 0:
            lam_init = 0.8 - 0.6 * math.exp(-B_LAMBDA_DECAY * layer)
            w_in = attn_w_in[i].astype(BF16)
            w_out = attn_w_out[i]
            w_oa = w_out[:A_Q].astype(BF16)
            w_ob = w_out[A_Q:].astype(BF16)
            sink = attn_sink[i]
            subln = diff_subln_g[i].reshape(1, LANES)
            aq, bq0, bq1, bk, ak, av, bv = _attn_inproj(h, mod, n1, w_in, cos_t, sin_t, tm=tm, row_of=lat_row,
                                                        rope=True, tiles_per_seq=lat_tiles)
            caq, cbq0, cbq1, cbk, cak, cav, cbv = _attn_inproj(hc, mod, n1, w_in, cos_t, sin_t, tm=tmc,
                                                               row_of=ctx_row, rope=False, tiles_per_seq=1)
            r3 = lambda a, n: a.reshape(bsz, n, a.shape[-1])
            cak3, cav3, cbk3, cbv3 = r3(cak, n_ctx), r3(cav, n_ctx), r3(cbk, n_ctx), r3(cbv, n_ctx)
            oa = _gqa_window(sink, r3(aq, n_tok), r3(ak, n_tok), r3(av, n_tok), cak3, cav3)
            b_args = (diff_lambda[i], subln, r3(bq0, n_tok), r3(bq1, n_tok),
                      [(r3(bk, n_tok), r3(bv, n_tok)), (cbk3, cbv3)])
            b_kw = dict(tq=_pick(n_tok, 1024), lam_init=lam_init)
            ob_lagged, over = _diff_attn(*b_args, lagged=True, **b_kw)
            ob = lax.cond(jnp.max(over) <= LAG_LIMIT, lambda: ob_lagged,
                          lambda: _diff_attn(*b_args, **b_kw)[0])
            mix, w_mix = [oa.reshape(m_lat, A_Q), ob.reshape(m_lat, B_V)], [w_oa, w_ob]
            if need_ctx:
                oca = _gqa_context(sink, r3(caq, n_ctx), cak3, cav3)
                ocb = _diff_attn(diff_lambda[i], subln, r3(cbq0, n_ctx), r3(cbq1, n_ctx), [(cbk3, cbv3)],
                                 tq=_pick(n_ctx, 256), lam_init=lam_init)[0]
                mix_c = [oca.reshape(m_ctx, A_Q), ocb.reshape(m_ctx, B_V)]
        else:
            w1 = gla_gate_w1[i]
            pad = jnp.zeros((d, LANES - 2 * C_GATE_RANK), F32)
            w_in = jnp.concatenate([gla_w_in[i], w1[0], w1[1], pad], axis=1).astype(BF16)
            w2 = gla_gate_w2[i]
            w2bd = jnp.zeros((LANES, 2 * C_QK), F32)
            w2bd = w2bd.at[0:C_GATE_RANK, 0:C_QK].set(w2[0]).at[C_GATE_RANK:2 * C_GATE_RANK, C_QK:].set(w2[1])
            w2bd = w2bd.astype(BF16)
            gb = gla_gate_b[i].reshape(1, 2 * C_QK)
            ng = gla_norm_g[i].reshape(1, C_DV)
            q, k, v, sg, la_f, la_b = _gla_inproj(h, mod, n1, w_in, w2bd, gb, tm=tm, row_of=lat_row)
            qc, kc, vc, sgc, lac_f, lac_b = _gla_inproj(hc, mod, n1, w_in, w2bd, gb, tm=tmc, row_of=ctx_row)
            r3 = lambda a, n: a.reshape(bsz, n, a.shape[-1])
            s_f, s_b = _gla_ctx_state(r3(kc, n_ctx), r3(vc, n_ctx), r3(lac_f, n_ctx), r3(lac_b, n_ctx))
            gt = _pick(n_tok, 256)
            q3, k3, v3 = r3(q, n_tok), r3(k, n_tok), r3(v, n_tok)
            o_b = _gla_scan(q3, k3, v3, r3(la_b, n_tok), s_b, gt=gt, reverse=True)
            og = _gla_scan(q3, k3, v3, r3(la_f, n_tok), s_f, gt=gt, reverse=False,
                           o_other=o_b, sg=r3(sg, n_tok), norm_g=ng)
            mix, w_mix = [og.reshape(m_lat, C_V)], [gla_w_out[i].astype(BF16)]
            if need_ctx:
                z = jnp.zeros((bsz, C_HEADS, C_DV, C_DK), F32)
                qc3, kc3, vc3 = r3(qc, n_ctx), r3(kc, n_ctx), r3(vc, n_ctx)
                gtc = _pick(n_ctx, 256)
                oc_b = _gla_scan(qc3, kc3, vc3, r3(lac_b, n_ctx), z, gt=gtc, reverse=True)
                ogc = _gla_scan(qc3, kc3, vc3, r3(lac_f, n_ctx), z, gt=gtc, reverse=False,
                                o_other=oc_b, sg=r3(sgc, n_ctx), norm_g=ng)
                mix_c = [ogc.reshape(m_ctx, C_V)]
        fg = final_norm_g.reshape(1, d)
        h = _mixer_out_ffn(mix, w_mix, h, mod, n2, ffn_w, layer, fg, tm=tm_ffn, row_of=ffn_row,
                           tiles_per_seq=n_tok // tm_ffn, final_norm=last)
        if need_ctx:
            hc = _mixer_out_ffn(mix_c, w_mix, hc, mod, n2, ffn_w, layer, fg, tm=tmc, row_of=ctx_row,
                                tiles_per_seq=n_ctx // tmc, final_norm=False)
    return h.reshape(bsz, n_tok, d)
```

```python
import functools
import math

import jax
import jax.numpy as jnp
from jax import lax
from jax.experimental import pallas as pl
from jax.experimental.pallas import tpu as pltpu

F32 = jnp.float32
BF16 = jnp.bfloat16

D_MODEL = 1024
DEPTH = 2
GRID_W = 64
HEAD_DIM = 64
ROPE_THETA = 10000.0
EPS = 1e-6
BLOCK = 128
A_HEADS = 8
A_KV_HEADS = 2
A_GROUP = A_HEADS // A_KV_HEADS
B_HEADS = 4
B_LAMBDA_DECAY = 0.3
A_Q = A_HEADS * HEAD_DIM
A_KV = A_KV_HEADS * HEAD_DIM
B_QK = B_HEADS * 2 * HEAD_DIM
B_V = B_HEADS * 2 * HEAD_DIM
C_HEADS = 4
C_DK = D_MODEL // 2 // C_HEADS
C_DV = D_MODEL // C_HEADS
C_GATE_RANK = 16
C_GATE_NORM = 16.0
C_CHUNK = 64
C_QK = C_HEADS * C_DK
C_V = C_HEADS * C_DV
D_FF = 2816
LANES = 128
MOD_ROWS = 8
VMEM_LIMIT = 56 * 1024 * 1024


def _cparams(sem):
    return pltpu.CompilerParams(dimension_semantics=sem, vmem_limit_bytes=VMEM_LIMIT)


def _dot(a, b):
    return jnp.dot(a, b, preferred_element_type=F32)


def _dot_nt(a, b):
    return lax.dot_general(a, b, (((1,), (1,)), ((), ())), preferred_element_type=F32)


def _dot_tn(a, b):
    return lax.dot_general(a, b, (((0,), (0,)), ((), ())), preferred_element_type=F32)


def _sigmoid(x):
    return 1.0 / (1.0 + jnp.exp(-x))


def _norm_mod(x, g, sc, sh):
    ms = jnp.mean(x * x, axis=-1, keepdims=True)
    return (x * lax.rsqrt(ms + EPS) * g) * (1.0 + sc) + sh


def _mod_kernel(c_ref, w_ref, b_ref, o_ref):
    c = c_ref[...]
    s = (c * _sigmoid(c)).astype(BF16)
    o_ref[...] = _dot(s, w_ref[...].astype(BF16)) + b_ref[...]


def _modulation(c_rows, mod_w, mod_b):
    d = D_MODEL
    return pl.pallas_call(
        _mod_kernel,
        grid=(DEPTH, 6),
        in_specs=[
            pl.BlockSpec((MOD_ROWS, d), lambda l, n: (0, 0)),
            pl.BlockSpec((None, d, d), lambda l, n: (l, 0, n)),
            pl.BlockSpec((None, 1, d), lambda l, n: (l, 0, n)),
        ],
        out_specs=pl.BlockSpec((None, MOD_ROWS, d), lambda l, n: (l, 0, n)),
        out_shape=jax.ShapeDtypeStruct((DEPTH, MOD_ROWS, 6 * d), F32),
        compiler_params=_cparams(("parallel", "parallel")),
        name="modulation",
    )(c_rows, mod_w, mod_b.reshape(DEPTH, 1, 6 * d))


def _mod_spec(k, row_of):
    return pl.BlockSpec((None, None, 1, D_MODEL), lambda i, *_: (row_of(i), k, 0, 0))


LOG2E = math.log2(math.e)
_Q_SCALE = HEAD_DIM ** -0.5 * LOG2E
_ATTN_GROUPS = (("aq", 0, A_Q, True, _Q_SCALE), ("bq", A_Q + 2 * A_KV, B_QK, True, _Q_SCALE),
                ("bk", A_Q + 2 * A_KV + B_QK, B_QK, True, 1.0), ("ak", A_Q, A_KV, True, 1.0),
                ("av", A_Q + A_KV, A_KV, False, 1.0), ("bv", A_Q + 2 * A_KV + 2 * B_QK, B_V, False, 1.0))
_ATTN_OUT_WIDTHS = tuple(w_ for name, _, width, _, _ in _ATTN_GROUPS
                         for w_ in ((width, width) if name == "bq" else
                                    (2 * width,) if name in ("ak", "av") else (width,)))


def _rope_chunk(v, cos, sin, even):
    partner = jnp.where(even, pltpu.roll(v, LANES - 1, 1), pltpu.roll(v, 1, 1))
    return v * cos + partner * sin


def _attn_inproj_kernel(x_ref, sh_ref, sc_ref, g_ref, w_ref, cos_ref, sin_ref, *out_refs, rope):
    xn = _norm_mod(x_ref[...], g_ref[...], sc_ref[...], sh_ref[...]).astype(BF16)
    tm = xn.shape[0]
    if rope:
        cos = cos_ref[...]
        sin = sin_ref[...]
        even = (lax.broadcasted_iota(jnp.int32, (tm, LANES), 1) & 1) == 0
    lo = lax.broadcasted_iota(jnp.int32, (1, LANES), 1) < HEAD_DIM
    outs = iter(out_refs)
    for name, col, width, roped, scale in _ATTN_GROUPS:
        y = _dot(xn, w_ref[:, col:col + width])
        o_refs = [next(outs), next(outs)] if name == "bq" else [next(outs)]
        for c in range(width // LANES):
            v = y[:, c * LANES:(c + 1) * LANES]
            if rope and roped:
                v = _rope_chunk(v, cos, sin, even)
            if scale != 1.0:
                v = v * scale
            if name == "bq":
                o_refs[0][:, c * LANES:(c + 1) * LANES] = jnp.where(lo, v, 0.0).astype(BF16)
                o_refs[1][:, c * LANES:(c + 1) * LANES] = jnp.where(lo, 0.0, v).astype(BF16)
            elif name in ("ak", "av"):
                swapped = pltpu.roll(v, HEAD_DIM, 1)
                o_refs[0][:, 0:LANES] = jnp.where(lo, v, swapped).astype(BF16)
                o_refs[0][:, LANES:2 * LANES] = jnp.where(lo, swapped, v).astype(BF16)
            else:
                o_refs[0][:, c * LANES:(c + 1) * LANES] = v.astype(BF16)


def _attn_inproj(x2d, mod, norm_g, w, cos_t, sin_t, *, tm, row_of, rope, tiles_per_seq):
    m, d = x2d.shape
    n_all = w.shape[1]
    widths = _ATTN_OUT_WIDTHS
    out_shape = [jax.ShapeDtypeStruct((m, width), BF16) for width in widths]
    out_specs = [pl.BlockSpec((tm, width), lambda i: (i, 0)) for width in widths]
    return pl.pallas_call(
        functools.partial(_attn_inproj_kernel, rope=rope),
        grid=(m // tm,),
        in_specs=[
            pl.BlockSpec((tm, d), lambda i: (i, 0)),
            _mod_spec(0, row_of), _mod_spec(1, row_of),
            pl.BlockSpec((1, d), lambda i: (0, 0)),
            pl.BlockSpec((d, n_all), lambda i: (0, 0)),
            pl.BlockSpec((tm, LANES), lambda i: (i % tiles_per_seq, 0)),
            pl.BlockSpec((tm, LANES), lambda i: (i % tiles_per_seq, 0)),
        ],
        out_specs=out_specs,
        out_shape=out_shape,
        compiler_params=_cparams(("parallel",)),
        name="attn_inproj_rope" if rope else "attn_inproj_ctx",
    )(x2d, mod, mod, norm_g, w, cos_t, sin_t)


GQA_QB = 8


def _gqa_kernel(sink_ref, q_ref, *refs, window, nb):
    tq = BLOCK
    qb = q_ref.shape[0] // tq
    if window:
        k_blk = [r[...] for r in refs[0:qb + 2]]
        v_blk = [r[...] for r in refs[qb + 2:2 * qb + 4]]
        kx_ref, vx_ref, o_ref = refs[2 * qb + 4:]
    else:
        kx_ref, vx_ref, o_ref = refs
    lane = lax.broadcasted_iota(jnp.int32, (1, LANES), 1)
    lo = lane < HEAD_DIM
    half = (jnp.where(lo, 1.0, 0.0).astype(F32), jnp.where(lo, 0.0, 1.0).astype(F32))
    row = lax.broadcasted_iota(jnp.int32, (LANES, 1), 0)
    sinks = [jnp.concatenate([jnp.full((1, tq), sink_ref[kv * A_GROUP + j] * LOG2E, F32)
                              for j in range(A_GROUP)], axis=1) for kv in range(A_KV_HEADS)]
    work = []
    for u in range(qb):
        if window:
            keys = jnp.concatenate(k_blk[u:u + 3] + [kx_ref[...]], axis=0)
            vals = jnp.concatenate(v_blk[u:u + 3] + [vx_ref[...]], axis=0)
        else:
            keys, vals = kx_ref[...], vx_ref[...]
        qf = [q_ref[u * tq:(u + 1) * tq, j * LANES:(j + 1) * LANES].astype(F32) for j in range(A_GROUP)]
        scs = [_dot_nt(keys[:, kv * LANES:(kv + 1) * LANES],
                       jnp.concatenate([(qf[2 * kv + c] * half[s_]).astype(BF16)
                                        for c in range(2) for s_ in range(2)], axis=0))
               for kv in range(A_KV_HEADS)]
        work.append((vals, scs))
    for u, (vals, scs) in enumerate(work):
        if window:
            n = pl.program_id(1) * qb + u
            s = lax.broadcasted_iota(jnp.int32, (BLOCK, tq), 0)
            t = lax.broadcasted_iota(jnp.int32, (BLOCK, tq), 1)
            has_prev = jnp.where(n > 0, 1, 0)
            has_next = jnp.where(n < nb - 1, 1, 0)
            lower = t * has_prev + BLOCK * (1 - has_prev)
            upper = (t + 1) * has_next - 1
            ninf = jnp.float32(-jnp.inf)
            bias_prev = jnp.concatenate([jnp.where(s < lower, ninf, 0.0)] * A_GROUP, axis=1)
            bias_next = jnp.concatenate([jnp.where(s > upper, ninf, 0.0)] * A_GROUP, axis=1)
        outs = []
        for kv in range(A_KV_HEADS):
            sc = scs[kv]
            if window:
                sc = jnp.concatenate([sc[0:BLOCK] + bias_prev, sc[BLOCK:2 * BLOCK],
                                      sc[2 * BLOCK:3 * BLOCK] + bias_next, sc[3 * BLOCK:]], axis=0)
            sk = sinks[kv]
            mx = jnp.maximum(jnp.max(sc, axis=0, keepdims=True), sk)
            p = jnp.exp2(sc - mx)
            den = jnp.sum(p, axis=0, keepdims=True) + jnp.exp2(sk - mx)
            outs.append(_dot_tn(vals[:, kv * LANES:(kv + 1) * LANES], p.astype(BF16)) * (1.0 / den))
        for j in range(A_GROUP):
            acc = outs[j // 2]
            c0 = (j % 2) * 2 * tq
            o_t = jnp.where(row < HEAD_DIM, acc[:, c0:c0 + tq], acc[:, c0 + tq:c0 + 2 * tq])
            o_ref[u * tq:(u + 1) * tq, j * LANES:(j + 1) * LANES] = o_t.T.astype(BF16)


def _gqa_window(sink, aq, ak, av, cak, cav):
    bsz, n_tok, _ = aq.shape
    n_ctx = cak.shape[1]
    nb = n_tok // BLOCK
    qb = math.gcd(nb, GQA_QB)
    kv_blocks = [pl.BlockSpec((None, BLOCK, 2 * A_KV),
                              lambda b, n, off=off: (b, jnp.clip(n * qb + off, 0, nb - 1), 0))
                 for off in range(-1, qb + 1)]
    kv_ctx = pl.BlockSpec((None, n_ctx, 2 * A_KV), lambda b, n: (b, 0, 0))
    return pl.pallas_call(
        functools.partial(_gqa_kernel, window=True, nb=nb),
        grid=(bsz, nb // qb),
        in_specs=[pl.BlockSpec(memory_space=pltpu.SMEM),
                  pl.BlockSpec((None, qb * BLOCK, A_Q), lambda b, n: (b, n, 0))]
        + kv_blocks + kv_blocks + [kv_ctx, kv_ctx],
        out_specs=pl.BlockSpec((None, qb * BLOCK, A_Q), lambda b, n: (b, n, 0)),
        out_shape=jax.ShapeDtypeStruct((bsz, n_tok, A_Q), BF16),
        compiler_params=_cparams(("parallel", "parallel")),
        name="gqa_window",
    )(sink, aq, *([ak] * (qb + 2)), *([av] * (qb + 2)), cak, cav)


def _gqa_context(sink, caq, cak, cav):
    bsz, n_ctx, _ = caq.shape
    nb = n_ctx // BLOCK
    kv_ctx = pl.BlockSpec((None, n_ctx, 2 * A_KV), lambda b, n: (b, 0, 0))
    return pl.pallas_call(
        functools.partial(_gqa_kernel, window=False, nb=nb),
        grid=(bsz, nb),
        in_specs=[pl.BlockSpec(memory_space=pltpu.SMEM),
                  pl.BlockSpec((None, BLOCK, A_Q), lambda b, n: (b, n, 0)),
                  kv_ctx, kv_ctx],
        out_specs=pl.BlockSpec((None, BLOCK, A_Q), lambda b, n: (b, n, 0)),
        out_shape=jax.ShapeDtypeStruct((bsz, n_ctx, A_Q), BF16),
        compiler_params=_cparams(("parallel", "parallel")),
        name="gqa_context",
    )(sink, caq, cak, cav)


LAG_LIMIT = 12.0
XPOSE_ROWS = 512
ONES_ROWS = 16


def _diff_attn_kernel(lam_ref, g_ref, q0_ref, q1_ref, *refs, nseg, lam_init, kc, lagged):
    k_refs = refs[0:2 * nseg:2]
    v_refs = refs[1:2 * nseg:2]
    if lagged:
        o_ref, ex_ref, vt_ref = refs[2 * nseg:]
    else:
        o_ref, vt_ref = refs[2 * nseg:]
    lv = lam_ref[...]
    lam = (jnp.exp(jnp.sum(lv[0:1] * lv[1:2], axis=-1, keepdims=True))
           - jnp.exp(jnp.sum(lv[2:3] * lv[3:4], axis=-1, keepdims=True)) + lam_init)
    q0 = q0_ref[...]
    q1 = q1_ref[...]
    @pl.when(pl.program_id(2) == 0)
    def _():
        off = 0
        for v_ref in v_refs:
            ns = v_ref.shape[0]
            for c0 in range(0, ns, XPOSE_ROWS):
                n = min(XPOSE_ROWS, ns - c0)
                vt_ref[0:LANES, off + c0:off + c0 + n] = v_ref[c0:c0 + n, :].astype(F32).T.astype(BF16)
            off += ns
        vt_ref[LANES:, :] = jnp.ones((vt_ref.shape[0] - LANES, vt_ref.shape[1]), BF16)

    chunks = []
    off = 0
    for k_ref in k_refs:
        ns = k_ref.shape[0]
        step = min(kc, ns)
        chunks += [(k_ref, c0, step, off + c0) for c0 in range(0, ns, step)]
        off += ns

    qms = (q0, q1)

    def scores(mi, ci):
        k_ref, c0, step, _ = chunks[ci]
        return _dot_nt(k_ref[c0:c0 + step, :], qms[mi])

    run_max = [None, None]
    accs = [None, None]
    excess = [None, None]

    def probs(mi, sc):
        mc = jnp.max(sc, axis=0, keepdims=True)
        if run_max[mi] is None:
            run_max[mi], alpha = (mc, mc), None
        else:
            ref_old, best = run_max[mi]
            new_best = jnp.maximum(best, mc)
            ref = best if lagged else new_best
            alpha = jnp.exp2(ref_old - ref)
            if lagged:
                over = mc - ref
                excess[mi] = over if excess[mi] is None else jnp.maximum(excess[mi], over)
            run_max[mi] = (ref, new_best)
        return jnp.exp2((sc - run_max[mi][0]).astype(BF16)), alpha

    def accumulate(mi, ci, pb, alpha):
        _, _, step, g0 = chunks[ci]
        pv = _dot(vt_ref[:, g0:g0 + step], pb)
        accs[mi] = pv if alpha is None else accs[mi] * alpha + pv

    n_ch = len(chunks)
    sc_q = {ci: [scores(mi, ci) for mi in range(2)] for ci in range(min(2, n_ch))}
    pb_q = {0: [probs(mi, sc_q[0][mi]) for mi in range(2)]}
    for ci in range(n_ch):
        if ci + 2 < n_ch:
            sc_q[ci + 2] = [scores(mi, ci + 2) for mi in range(2)]
        if ci + 1 < n_ch:
            sc_pair = sc_q.pop(ci + 1)
            pb_q[ci + 1] = [probs(mi, sc_pair[mi]) for mi in range(2)]
        for mi, (pb, alpha) in enumerate(pb_q.pop(ci)):
            accumulate(mi, ci, pb, alpha)
    r0 = 1.0 / accs[0][LANES:LANES + 1, :]
    r1 = lam / accs[1][LANES:LANES + 1, :]
    o = (accs[0][0:LANES, :] * r0 - accs[1][0:LANES, :] * r1).T
    ms = jnp.mean(o * o, axis=-1, keepdims=True)
    o_ref[...] = ((o * lax.rsqrt(ms + EPS) * g_ref[...]) * (1.0 - lam_init)).astype(BF16)
    if lagged:
        ex_ref[...] = jnp.maximum(excess[0], excess[1])


def _diff_attn(lam_vec, subln_g, q0, q1, kv_list, *, tq, lam_init, kc=256, lagged=False):
    bsz, n_q, _ = q0.shape
    out_specs = [pl.BlockSpec((None, tq, LANES), lambda b, h, i: (b, i, h))]
    out_shape = [jax.ShapeDtypeStruct((bsz, n_q, B_V), BF16)]
    if lagged:
        out_specs.append(pl.BlockSpec((None, None, 1, tq), lambda b, h, i: (b, h, 0, i)))
        out_shape.append(jax.ShapeDtypeStruct((bsz, B_HEADS, 1, n_q), F32))
    in_specs = [pl.BlockSpec((4, HEAD_DIM), lambda b, h, i: (0, 0)),
                pl.BlockSpec((1, LANES), lambda b, h, i: (0, 0)),
                pl.BlockSpec((None, tq, LANES), lambda b, h, i: (b, i, h)),
                pl.BlockSpec((None, tq, LANES), lambda b, h, i: (b, i, h))]
    args = [lam_vec, subln_g, q0, q1]
    for k, v in kv_list:
        ns = k.shape[1]
        in_specs += [pl.BlockSpec((None, ns, LANES), lambda b, h, i: (b, 0, h))] * 2
        args += [k, v]
    return pl.pallas_call(
        functools.partial(_diff_attn_kernel, nseg=len(kv_list), lam_init=lam_init, kc=kc, lagged=lagged),
        grid=(bsz, B_HEADS, n_q // tq),
        in_specs=in_specs,
        out_specs=out_specs,
        out_shape=out_shape,
        scratch_shapes=[pltpu.VMEM((LANES + ONES_ROWS, sum(k.shape[1] for k, _ in kv_list)), BF16)],
        compiler_params=_cparams(("parallel", "parallel", "arbitrary")),
        name="diff_attn_%dseg%s" % (len(kv_list), "_lagged" if lagged else ""),
    )(*args)


FFN_HALO = 16
FFN_TF = 256


def _ffn_kernel(*refs, n_mix, tiles_per_seq, final_norm):
    a_refs = refs[:3 * n_mix]
    wo_refs = refs[3 * n_mix:4 * n_mix]
    (h_ref, hp_ref, hn_ref, g1_ref, sh_ref, sc_ref, gt_ref, ng_ref, wup_ref, cwu_ref, cwg_ref,
     wd_ref, fg_ref, o_ref, xn_ref, hu_a, hg_a, hu_b, hg_b, acc_ref) = refs[4 * n_mix:]
    i = pl.program_id(0)
    tm = h_ref.shape[0]
    hl = FFN_HALO
    nj, tf = wd_ref.shape[0], wd_ref.shape[1]

    ext = tm + 2 * hl
    half = ext // 2
    pos = i % tiles_per_seq
    a_ext = [jnp.concatenate([a_refs[3 * m + 1][...], a_refs[3 * m][...], a_refs[3 * m + 2][...]], axis=0)
             for m in range(n_mix)]
    h_ext = jnp.concatenate([hp_ref[...], h_ref[...], hn_ref[...]], axis=0)
    ys = []
    for blk in range(2):
        rows = slice(blk * half, (blk + 1) * half)
        y = None
        for m in range(n_mix):
            part = _dot(a_ext[m][rows, :], wo_refs[m][...])
            y = part if y is None else y + part
        ys.append(y)
    for blk in range(2):
        h1 = h_ext[blk * half:(blk + 1) * half, :] + g1_ref[...] * ys[blk]
        xn = _norm_mod(h1, ng_ref[...], sc_ref[...], sh_ref[...])
        if blk == 0:
            o_ref[0:half - hl, :] = h1[hl:, :]
            xn_ref[0:hl, :] = (xn[0:hl, :] * jnp.where(pos > 0, 1.0, 0.0)).astype(BF16)
            xn_ref[hl:half, :] = xn[hl:, :].astype(BF16)
        else:
            o_ref[half - hl:, :] = h1[0:half - hl, :]
            xn_ref[half:ext - hl, :] = xn[0:half - hl, :].astype(BF16)
            xn_ref[ext - hl:, :] = (xn[half - hl:, :] * jnp.where(pos < tiles_per_seq - 1, 1.0, 0.0)).astype(BF16)
    acc_ref[...] = jnp.zeros_like(acc_ref)

    def up(j, hu_ref, hg_ref, row_blocks=1):
        cols = pl.ds(pl.multiple_of(j * tf, tf), tf)
        gate_cols = pl.ds(pl.multiple_of(nj * tf + j * tf, tf), tf)
        for blk in range(row_blocks):
            rows = slice(blk * ext // row_blocks, (blk + 1) * ext // row_blocks)
            xn = xn_ref[rows, :]
            hu_ref[rows, :] = _dot(xn, wup_ref[:, cols])
            hg_ref[rows, :] = _dot(xn, wup_ref[:, gate_cols])

    def conv(ref, cw):
        return (ref[hl - 1:hl - 1 + tm, :] * cw[0:1] + ref[hl:hl + tm, :] * cw[1:2]
                + ref[hl + 1:hl + 1 + tm, :] * cw[2:3] + cw[3:4])

    def act(j, hu_ref, hg_ref):
        u = conv(hu_ref, cwu_ref[j])
        gt = conv(hg_ref, cwg_ref[j])
        return (gt * _sigmoid(gt) * u).astype(BF16)

    up(0, hu_a, hg_a, row_blocks=2)

    def pair(jj, carry):
        j = 2 * jj
        up(j + 1, hu_b, hg_b)
        acc_ref[...] += _dot(act(j, hu_a, hg_a), wd_ref[j])
        up(j + 2, hu_a, hg_a)
        acc_ref[...] += _dot(act(j + 1, hu_b, hg_b), wd_ref[j + 1])
        return carry

    assert nj % 2 == 1
    lax.fori_loop(0, (nj - 1) // 2, pair, 0)
    acc_ref[...] += _dot(act(nj - 1, hu_a, hg_a), wd_ref[nj - 1])

    y = o_ref[...] + gt_ref[...] * acc_ref[...]
    if final_norm:
        ms = jnp.mean(y * y, axis=-1, keepdims=True)
        y = y * lax.rsqrt(ms + EPS) * fg_ref[...]
    o_ref[...] = y


def _ffn_weights(w_up, conv_w, conv_b, w_down, tf):
    depth, d = w_up.shape[0], w_up.shape[1]
    nj = D_FF // tf
    chunked = lambda w: w.reshape(depth, w.shape[1], nj, tf).transpose(0, 2, 1, 3)
    cw = jnp.concatenate([conv_w, conv_b[:, None, :]], axis=1)
    return (w_up.astype(BF16), chunked(cw[:, :, :D_FF]), chunked(cw[:, :, D_FF:]),
            w_down.astype(BF16).reshape(depth, nj, tf, d))


def _mixer_out_ffn(acts, w_outs, h2d, mod, norm_g, weights, layer, final_g, *, tm, row_of, tiles_per_seq,
                   final_norm):
    m, d = h2d.shape
    hl = FFN_HALO
    nhb = m // hl
    r = tm // hl
    w_up, cwu, cwg, w_dn = weights
    tf = w_dn.shape[2]
    resident = lambda a: pl.BlockSpec(a.shape, lambda i: (0,) * a.ndim, pipeline_mode=pl.Buffered(1))
    of_layer = lambda a: pl.BlockSpec((None,) + a.shape[1:], lambda i: (layer,) + (0,) * (a.ndim - 1),
                                      pipeline_mode=pl.Buffered(1))

    def tile_and_halos(width):
        return [pl.BlockSpec((tm, width), lambda i: (i, 0)),
                pl.BlockSpec((hl, width), lambda i: (jnp.maximum(i * r - 1, 0), 0)),
                pl.BlockSpec((hl, width), lambda i: (jnp.minimum((i + 1) * r, nhb - 1), 0))]

    in_specs, args = [], []
    for a in acts:
        in_specs += tile_and_halos(a.shape[1])
        args += [a, a, a]
    in_specs += [resident(w) for w in w_outs]
    args += list(w_outs)
    in_specs += tile_and_halos(d)
    in_specs += [_mod_spec(2, row_of), _mod_spec(3, row_of), _mod_spec(4, row_of), _mod_spec(5, row_of),
                 pl.BlockSpec((1, d), lambda i: (0, 0)),
                 of_layer(w_up), of_layer(cwu), of_layer(cwg), of_layer(w_dn),
                 pl.BlockSpec((1, d), lambda i: (0, 0))]
    args += [h2d, h2d, h2d, mod, mod, mod, mod, norm_g, w_up, cwu, cwg, w_dn, final_g]
    return pl.pallas_call(
        functools.partial(_ffn_kernel, n_mix=len(acts), tiles_per_seq=tiles_per_seq, final_norm=final_norm),
        grid=(m // tm,),
        in_specs=in_specs,
        out_specs=pl.BlockSpec((tm, d), lambda i: (i, 0)),
        out_shape=jax.ShapeDtypeStruct((m, d), F32),
        scratch_shapes=[pltpu.VMEM((tm + 2 * hl, d), BF16)]
        + [pltpu.VMEM((tm + 2 * hl, tf), F32)] * 4
        + [pltpu.VMEM((tm, d), F32)],
        compiler_params=_cparams(("parallel",)),
        name="mixer_out_conv_ffn",
    )(*args)


def _gla_inproj_kernel(x_ref, sh_ref, sc_ref, g_ref, w_ref, w2_ref, gb_ref, *out_refs, with_queries):
    if with_queries:
        q_ref, k_ref, v_ref, sg_ref, laf_ref, lab_ref = out_refs
    else:
        k_ref, v_ref, laf_ref, lab_ref = out_refs
    tm = x_ref.shape[0]
    n_split = 2 if tm % 32 == 0 else 1
    blocks = [slice(s * tm // n_split, (s + 1) * tm // n_split) for s in range(n_split)]
    xns = [_norm_mod(x_ref[rows, :], g_ref[...], sc_ref[...], sh_ref[...]).astype(BF16) for rows in blocks]
    for rows, xn in zip(blocks, xns):
        if with_queries:
            q_ref[rows, :] = (_dot(xn, w_ref[:, 0:C_QK]) * (C_DK ** -0.5)).astype(BF16)
        k_ref[rows, :] = _dot(xn, w_ref[:, C_QK:2 * C_QK]).astype(BF16)
        for c in range(C_V // C_QK):
            lo = 2 * C_QK + c * C_QK
            v_ref[rows, c * C_QK:(c + 1) * C_QK] = _dot(xn, w_ref[:, lo:lo + C_QK]).astype(BF16)
        for c in range(C_V // C_QK if with_queries else 0):
            lo = 2 * C_QK + C_V + c * C_QK
            gg = _dot(xn, w_ref[:, lo:lo + C_QK])
            sg_ref[rows, c * C_QK:(c + 1) * C_QK] = (gg * _sigmoid(gg)).astype(BF16)
        lo = 2 * C_QK + 2 * C_V
        r = _dot(xn, w_ref[:, lo:lo + LANES]).astype(BF16)
        for dr, la_ref in enumerate((laf_ref, lab_ref)):
            z = _dot(r, w2_ref[:, dr * C_QK:(dr + 1) * C_QK]) + gb_ref[:, dr * C_QK:(dr + 1) * C_QK]
            la_ref[rows, :] = (jnp.minimum(z, 0.0) - jnp.log(1.0 + jnp.exp(-jnp.abs(z)))) * (1.0 / C_GATE_NORM)


def _gla_inproj(x2d, mod, norm_g, w, w2, gb, *, tm, row_of, with_queries=True):
    m, d = x2d.shape
    outs = (("q", C_QK, BF16), ("k", C_QK, BF16), ("v", C_V, BF16), ("sg", C_V, BF16),
            ("la_f", C_QK, F32), ("la_b", C_QK, F32))
    if not with_queries:
        outs = tuple(o for o in outs if o[0] not in ("q", "sg"))
    widths = [o[1] for o in outs]
    dts = [o[2] for o in outs]
    return pl.pallas_call(
        functools.partial(_gla_inproj_kernel, with_queries=with_queries),
        grid=(m // tm,),
        in_specs=[
            pl.BlockSpec((tm, d), lambda i: (i, 0)),
            _mod_spec(0, row_of), _mod_spec(1, row_of),
            pl.BlockSpec((1, d), lambda i: (0, 0)),
            pl.BlockSpec(w.shape, lambda i: (0, 0)),
            pl.BlockSpec(w2.shape, lambda i: (0, 0)),
            pl.BlockSpec(gb.shape, lambda i: (0, 0)),
        ],
        out_specs=[pl.BlockSpec((tm, wd), lambda i: (i, 0)) for wd in widths],
        out_shape=[jax.ShapeDtypeStruct((m, wd), dt) for wd, dt in zip(widths, dts)],
        compiler_params=_cparams(("parallel",)),
        name="gla_inproj",
    )(x2d, mod, mod, norm_g, w, w2, gb)


def _tri(n, reverse):
    r = lax.broadcasted_iota(jnp.int32, (n, n), 0)
    c = lax.broadcasted_iota(jnp.int32, (n, n), 1)
    return (c >= r) if reverse else (c <= r)


def _cumsum_rows(la, tri_bf):
    hi = la.astype(BF16)
    r1 = la - hi.astype(F32)
    mid = r1.astype(BF16)
    lo = (r1 - mid.astype(F32)).astype(BF16)
    return _dot(tri_bf, hi) + _dot(tri_bf, mid) + _dot(tri_bf, lo)


def _gla_ctx_state_kernel(k_ref, v_ref, laf_ref, lab_ref, sf_ref, sb_ref):
    n = k_ref.shape[0]
    for reverse, la_ref, s_ref in ((False, laf_ref, sf_ref), (True, lab_ref, sb_ref)):
        tri = jnp.where(_tri(n, reverse), 1.0, 0.0).astype(BF16)
        b = _cumsum_rows(la_ref[...], tri)
        b_end = b[0:1, :] if reverse else b[n - 1:n, :]
        kw = (k_ref[...].astype(F32) * jnp.exp(b_end - b)).astype(BF16)
        for h in range(C_HEADS):
            s_ref[h] = _dot_tn(v_ref[:, h * C_DV:(h + 1) * C_DV], kw[:, h * C_DK:(h + 1) * C_DK])


def _gla_ctx_state(kc, vc, lac_f, lac_b):
    bsz, n, _ = kc.shape
    s_shape = jax.ShapeDtypeStruct((bsz, C_HEADS, C_DV, C_DK), F32)
    s_spec = pl.BlockSpec((None, C_HEADS, C_DV, C_DK), lambda b: (b, 0, 0, 0))
    return pl.pallas_call(
        _gla_ctx_state_kernel,
        grid=(bsz,),
        in_specs=[pl.BlockSpec((None, n, C_QK), lambda b: (b, 0, 0)),
                  pl.BlockSpec((None, n, C_V), lambda b: (b, 0, 0)),
                  pl.BlockSpec((None, n, C_QK), lambda b: (b, 0, 0)),
                  pl.BlockSpec((None, n, C_QK), lambda b: (b, 0, 0))],
        out_specs=[s_spec, s_spec],
        out_shape=[s_shape, s_shape],
        compiler_params=_cparams(("parallel",)),
        name="gla_ctx_state",
    )(kc, vc, lac_f, lac_b)


GLA_BATCH_BLOCK = 4


def _gla_scan_kernel(q_ref, k_ref, v_ref, la_ref, s0_ref, *refs, reverse, final):
    if final:
        ob_ref, sg_ref, ng_ref, o_ref, st_ref = refs
    else:
        o_ref, st_ref = refs
    gidx = pl.program_id(1)

    @pl.when(gidx == 0)
    def _():
        st_ref[...] = s0_ref[...]

    nbb, gt = q_ref.shape[0], q_ref.shape[1]
    nchunk = gt // C_CHUNK
    c = C_CHUNK
    r = lax.broadcasted_iota(jnp.int32, (gt, gt), 0)
    s = lax.broadcasted_iota(jnp.int32, (gt, gt), 1)
    shift = c.bit_length() - 1
    same_chunk = (r >> shift) == (s >> shift)
    tri = same_chunk & ((s >= r) if reverse else (s <= r))
    tri_bf = jnp.where(tri, 1.0, 0.0).astype(BF16)

    def per_chunk_row(x, row):
        return jnp.concatenate([jnp.broadcast_to(x[ci * c + row:ci * c + row + 1, :], (c, x.shape[1]))
                                for ci in range(nchunk)], axis=0)

    order = range(nchunk - 1, -1, -1) if reverse else range(nchunk)
    kcols = [slice(h * C_DK, (h + 1) * C_DK) for h in range(C_HEADS)]
    vcols_of = [slice(h * C_DV, (h + 1) * C_DV) for h in range(C_HEADS)]
    streams = [(bi, h) for bi in range(nbb) for h in range(C_HEADS)]
    q_out, k_out, decay, o_intra, upd = {}, {}, {}, {}, {}
    for bi in range(nbb):
        b = _cumsum_rows(la_ref[bi], tri_bf)
        b_mid = per_chunk_row(b, c // 2)
        b_end = per_chunk_row(b, 0 if reverse else c - 1)
        qf = q_ref[bi].astype(F32)
        kf = k_ref[bi].astype(F32)
        q_in = (qf * jnp.exp(b - b_mid)).astype(BF16)
        k_in = (kf * jnp.exp(b_mid - b)).astype(BF16)
        q_out[bi] = (qf * jnp.exp(b)).astype(BF16)
        k_out[bi] = (kf * jnp.exp(b_end - b)).astype(BF16)
        end_row = 0 if reverse else c - 1
        decay[bi] = [jnp.exp(b[ci * c + end_row:ci * c + end_row + 1, :]) for ci in range(nchunk)]
        for h in range(C_HEADS):
            vv = v_ref[bi, :, vcols_of[h]]
            sc = jnp.where(tri, _dot_nt(q_in[:, kcols[h]], k_in[:, kcols[h]]), 0.0).astype(BF16)
            o_intra[bi, h] = _dot(sc, vv)
            upd[bi, h] = {ci: _dot_tn(vv[ci * c:(ci + 1) * c, :], k_out[bi][ci * c:(ci + 1) * c, kcols[h]])
                          for ci in order}
    st = {s_: st_ref[s_[0], s_[1]] for s_ in streams}
    o_inter = {s_: {} for s_ in streams}
    for ci in order:
        rows = slice(ci * c, (ci + 1) * c)
        for bi, h in streams:
            o_inter[bi, h][ci] = _dot_nt(q_out[bi][rows, kcols[h]], st[bi, h].astype(BF16))
            st[bi, h] = st[bi, h] * decay[bi][ci][:, kcols[h]] + upd[bi, h][ci]
    for bi, h in streams:
        vcols = vcols_of[h]
        st_ref[bi, h] = st[bi, h]
        o = o_intra[bi, h] + jnp.concatenate([o_inter[bi, h][ci] for ci in range(nchunk)], axis=0)
        if final:
            o = o + ob_ref[bi, :, vcols]
            ms = jnp.mean(o * o, axis=-1, keepdims=True)
            o = (o * lax.rsqrt(ms + EPS) * ng_ref[...]) * sg_ref[bi, :, vcols].astype(F32)
            o_ref[bi, :, vcols] = o.astype(BF16)
        else:
            o_ref[bi, :, vcols] = o


def _gla_scan(q, k, v, la, s0, *, gt, reverse, o_other=None, sg=None, norm_g=None):
    bsz, n_tok, _ = q.shape
    ng = n_tok // gt
    final = o_other is not None
    nbb = GLA_BATCH_BLOCK if bsz % GLA_BATCH_BLOCK == 0 else 1
    gi = (lambda b, g: (b, ng - 1 - g, 0)) if reverse else (lambda b, g: (b, g, 0))
    in_specs = [pl.BlockSpec((nbb, gt, C_QK), gi), pl.BlockSpec((nbb, gt, C_QK), gi),
                pl.BlockSpec((nbb, gt, C_V), gi), pl.BlockSpec((nbb, gt, C_QK), gi),
                pl.BlockSpec((nbb, C_HEADS, C_DV, C_DK), lambda b, g: (b, 0, 0, 0))]
    args = [q, k, v, la, s0]
    if final:
        in_specs += [pl.BlockSpec((nbb, gt, C_V), gi), pl.BlockSpec((nbb, gt, C_V), gi),
                     pl.BlockSpec((1, C_DV), lambda b, g: (0, 0))]
        args += [o_other, sg, norm_g]
    return pl.pallas_call(
        functools.partial(_gla_scan_kernel, reverse=reverse, final=final),
        grid=(bsz // nbb, ng),
        in_specs=in_specs,
        out_specs=pl.BlockSpec((nbb, gt, C_V), gi),
        out_shape=jax.ShapeDtypeStruct((bsz, n_tok, C_V), BF16 if final else F32),
        scratch_shapes=[pltpu.VMEM((nbb, C_HEADS, C_DV, C_DK), F32)],
        compiler_params=_cparams(("parallel", "arbitrary")),
        name="gla_scan_fwd_final" if final else "gla_scan_bwd",
    )(*args)


def _rope_tables(n_tok):
    rows = n_tok // GRID_W
    row = jnp.repeat(jnp.arange(rows, dtype=F32), GRID_W)
    col = jnp.tile(jnp.arange(GRID_W, dtype=F32), rows)
    axis_dim = HEAD_DIM // 2
    inv_freq = ROPE_THETA ** (-jnp.arange(0, axis_dim, 2, dtype=F32) / axis_dim)
    ang = jnp.concatenate([row[:, None] * inv_freq, col[:, None] * inv_freq], axis=-1)
    cos = jnp.repeat(jnp.cos(ang), 2, axis=1)
    sin = jnp.repeat(jnp.sin(ang), 2, axis=1) * jnp.tile(jnp.array([-1.0, 1.0], F32), HEAD_DIM // 2)
    return jnp.tile(cos, (1, LANES // HEAD_DIM)), jnp.tile(sin, (1, LANES // HEAD_DIM))


def _pick(n, pref):
    return pref if n % pref == 0 else n


def kernel(x, c, ctx, c_ctx, mod_w, mod_b, norm1_g, norm2_g, attn_w_in, attn_w_out, attn_sink, diff_lambda, diff_subln_g, gla_w_in, gla_gate_w1, gla_gate_w2, gla_gate_b, gla_norm_g, gla_w_out, ffn_w_up, ffn_conv_w, ffn_conv_b, ffn_w_down, final_norm_g):
    bsz, n_tok, d = x.shape
    n_ctx = ctx.shape[1]
    assert d == D_MODEL and bsz + 1 <= MOD_ROWS
    m_lat, m_ctx = bsz * n_tok, bsz * n_ctx

    c_rows = jnp.concatenate([c, c_ctx[None, :], jnp.zeros((MOD_ROWS - bsz - 1, d), F32)], axis=0)
    mod_all = _modulation(c_rows, mod_w, mod_b)
    cos_t, sin_t = _rope_tables(n_tok)

    tm = _pick(n_tok, 1024)
    tmc = _pick(n_ctx, 256)
    lat_tiles = n_tok // tm
    lat_row = lambda i: i // lat_tiles
    ctx_row = lambda i: bsz
    tm_ffn = _pick(n_tok, 1024)
    ffn_row = lambda i: i // (n_tok // tm_ffn)

    h = x.reshape(m_lat, d)
    hc = ctx.reshape(m_ctx, d)
    ffn_w = _ffn_weights(ffn_w_up, ffn_conv_w, ffn_conv_b, ffn_w_down, FFN_TF)
    for layer in range(DEPTH):
        need_ctx = layer < DEPTH - 1
        last = layer == DEPTH - 1
        mod = mod_all[layer].reshape(MOD_ROWS, 6, 1, d)
        n1 = norm1_g[layer].reshape(1, d)
        n2 = norm2_g[layer].reshape(1, d)
        i = layer // 2
        if layer % 2 == 0:
            lam_init = 0.8 - 0.6 * math.exp(-B_LAMBDA_DECAY * layer)
            w_in = attn_w_in[i].astype(BF16)
            w_out = attn_w_out[i]
            w_oa = w_out[:A_Q].astype(BF16)
            w_ob = w_out[A_Q:].astype(BF16)
            sink = attn_sink[i]
            subln = diff_subln_g[i].reshape(1, LANES)
            aq, bq0, bq1, bk, ak, av, bv = _attn_inproj(h, mod, n1, w_in, cos_t, sin_t, tm=tm, row_of=lat_row,
                                                        rope=True, tiles_per_seq=lat_tiles)
            caq, cbq0, cbq1, cbk, cak, cav, cbv = _attn_inproj(hc, mod, n1, w_in, cos_t, sin_t, tm=tmc,
                                                               row_of=ctx_row, rope=False, tiles_per_seq=1)
            r3 = lambda a, n: a.reshape(bsz, n, a.shape[-1])
            cak3, cav3, cbk3, cbv3 = r3(cak, n_ctx), r3(cav, n_ctx), r3(cbk, n_ctx), r3(cbv, n_ctx)
            oa = _gqa_window(sink, r3(aq, n_tok), r3(ak, n_tok), r3(av, n_tok), cak3, cav3)
            b_args = (diff_lambda[i], subln, r3(bq0, n_tok), r3(bq1, n_tok),
                      [(r3(bk, n_tok), r3(bv, n_tok)), (cbk3, cbv3)])
            b_kw = dict(tq=_pick(n_tok, 1024), lam_init=lam_init)
            ob_lagged, over = _diff_attn(*b_args, lagged=True, **b_kw)
            ob = lax.cond(jnp.max(over) <= LAG_LIMIT, lambda: ob_lagged,
                          lambda: _diff_attn(*b_args, **b_kw)[0])
            mix, w_mix = [oa.reshape(m_lat, A_Q), ob.reshape(m_lat, B_V)], [w_oa, w_ob]
            if need_ctx:
                oca = _gqa_context(sink, r3(caq, n_ctx), cak3, cav3)
                ocb = _diff_attn(diff_lambda[i], subln, r3(cbq0, n_ctx), r3(cbq1, n_ctx), [(cbk3, cbv3)],
                                 tq=_pick(n_ctx, 256), lam_init=lam_init)[0]
                mix_c = [oca.reshape(m_ctx, A_Q), ocb.reshape(m_ctx, B_V)]
        else:
            w1 = gla_gate_w1[i]
            pad = jnp.zeros((d, LANES - 2 * C_GATE_RANK), F32)
            w_in = jnp.concatenate([gla_w_in[i], w1[0], w1[1], pad], axis=1).astype(BF16)
            w2 = gla_gate_w2[i]
            w2bd = jnp.zeros((LANES, 2 * C_QK), F32)
            w2bd = w2bd.at[0:C_GATE_RANK, 0:C_QK].set(w2[0]).at[C_GATE_RANK:2 * C_GATE_RANK, C_QK:].set(w2[1])
            w2bd = w2bd.astype(BF16)
            gb = gla_gate_b[i].reshape(1, 2 * C_QK)
            ng = gla_norm_g[i].reshape(1, C_DV)
            q, k, v, sg, la_f, la_b = _gla_inproj(h, mod, n1, w_in, w2bd, gb, tm=tm, row_of=lat_row)
            ctx_proj = _gla_inproj(hc, mod, n1, w_in, w2bd, gb, tm=tmc, row_of=ctx_row, with_queries=need_ctx)
            if need_ctx:
                qc, kc, vc, sgc, lac_f, lac_b = ctx_proj
            else:
                kc, vc, lac_f, lac_b = ctx_proj
            r3 = lambda a, n: a.reshape(bsz, n, a.shape[-1])
            s_f, s_b = _gla_ctx_state(r3(kc, n_ctx), r3(vc, n_ctx), r3(lac_f, n_ctx), r3(lac_b, n_ctx))
            gt = _pick(n_tok, 256)
            q3, k3, v3 = r3(q, n_tok), r3(k, n_tok), r3(v, n_tok)
            o_b = _gla_scan(q3, k3, v3, r3(la_b, n_tok), s_b, gt=gt, reverse=True)
            og = _gla_scan(q3, k3, v3, r3(la_f, n_tok), s_f, gt=gt, reverse=False,
                           o_other=o_b, sg=r3(sg, n_tok), norm_g=ng)
            mix, w_mix = [og.reshape(m_lat, C_V)], [gla_w_out[i].astype(BF16)]
            if need_ctx:
                z = jnp.zeros((bsz, C_HEADS, C_DV, C_DK), F32)
                qc3, kc3, vc3 = r3(qc, n_ctx), r3(kc, n_ctx), r3(vc, n_ctx)
                gtc = _pick(n_ctx, 256)
                oc_b = _gla_scan(qc3, kc3, vc3, r3(lac_b, n_ctx), z, gt=gtc, reverse=True)
                ogc = _gla_scan(qc3, kc3, vc3, r3(lac_f, n_ctx), z, gt=gtc, reverse=False,
                                o_other=oc_b, sg=r3(sgc, n_ctx), norm_g=ng)
                mix_c = [ogc.reshape(m_ctx, C_V)]
        fg = final_norm_g.reshape(1, d)
        h = _mixer_out_ffn(mix, w_mix, h, mod, n2, ffn_w, layer, fg, tm=tm_ffn, row_of=ffn_row,
                           tiles_per_seq=n_tok // tm_ffn, final_norm=last)
        if need_ctx:
            hc = _mixer_out_ffn(mix_c, w_mix, hc, mod, n2, ffn_w, layer, fg, tm=tmc, row_of=ctx_row,
                                tiles_per_seq=n_ctx // tmc, final_norm=False)
    return h.reshape(bsz, n_tok, d)
```
